```python
import math
import jax, jax.numpy as jnp
from jax import lax
import numpy as np

D_MODEL = 2048
BATCH = 1
SEQ = 8192
DEPTH = 4

GRID_W = 64
CTX_LEN = 256
EPS = 1e-6
ROPE_BASE = 10000.0

D_A = 512
N_BLK_A = 8
BLK_A = D_A // N_BLK_A
CONV_A = 4
LRU_C = 8.0

N_HEADS_B = 4
HEAD_DIM_B = 128
D_B = N_HEADS_B * HEAD_DIM_B
CONV_B = 4
CHUNK_B = 64

N_Q_HEADS_C = 8
N_KV_HEADS_C = 2
HEAD_DIM_C = 128
D_Q_C = N_Q_HEADS_C * HEAD_DIM_C
D_KV_C = N_KV_HEADS_C * HEAD_DIM_C
Q_BLOCK = 128

D_MIX = D_A + D_B + D_Q_C
IN_SIZES = (D_A, D_A, 3 * D_B, D_B, 2 * N_HEADS_B, 2 * N_HEADS_B, D_Q_C, D_KV_C, D_KV_C, D_Q_C)
D_IN = sum(IN_SIZES)

kernel_name = 'hybrid_rglru_gdn_gqa_prefix_trunk'


def rms_norm(x, g):
    xf = x.astype(jnp.float32)
    y = xf * lax.rsqrt(jnp.mean(xf * xf, axis=-1, keepdims=True) + EPS)
    return (y * g.astype(jnp.float32)).astype(x.dtype)


def l2_normalize(x):
    return x * lax.rsqrt(jnp.sum(x * x, axis=-1, keepdims=True) + EPS)


def flip_time(z, axis, rev):
    return jnp.flip(z, axis=axis) if rev else z


def dwconv_centred(x, w):
    k = w.shape[0]
    return lax.conv_general_dilated(
        x, w.astype(x.dtype)[:, None, :], window_strides=(1,),
        padding=[(k // 2, k - 1 - k // 2)],
        dimension_numbers=('NWC', 'WIO', 'NWC'), feature_group_count=x.shape[-1])


def split_cols(z):
    idx = np.cumsum(IN_SIZES)[:-1].tolist()
    return jnp.split(z, idx, axis=-1)


def axial_rope(row, col):
    n_freq = HEAD_DIM_C // 4
    inv_freq = ROPE_BASE ** (-jnp.arange(n_freq, dtype=jnp.float32) / n_freq)
    ang = jnp.concatenate([row.astype(jnp.float32)[:, None] * inv_freq,
                           col.astype(jnp.float32)[:, None] * inv_freq], axis=-1)
    return jnp.cos(ang), jnp.sin(ang)


def apply_rope(x, cos, sin):
    xf = x.astype(jnp.float32)
    half = xf.shape[-1] // 2
    x1, x2 = xf[..., :half], xf[..., half:]
    cs, sn = cos[None, :, None, :], sin[None, :, None, :]
    return jnp.concatenate([x1 * cs - x2 * sn, x2 * cs + x1 * sn], axis=-1).astype(x.dtype)


def _linrec_combine(left, right):
    a_l, b_l = left
    a_r, b_r = right
    return a_l * a_r, a_r * b_l + b_r


def rglru_scan(u, h0, w_r, b_r, w_i, b_i, lam):
    bsz, t_len, _ = u.shape
    ub = u.reshape(bsz, t_len, N_BLK_A, BLK_A)
    gate_r = jax.nn.sigmoid(jnp.einsum('btni,nij->btnj', ub, w_r.astype(jnp.float32)).reshape(bsz, t_len, D_A) + b_r)
    gate_i = jax.nn.sigmoid(jnp.einsum('btni,nij->btnj', ub, w_i.astype(jnp.float32)).reshape(bsz, t_len, D_A) + b_i)
    log_a = -LRU_C * gate_r * jax.nn.softplus(-lam.astype(jnp.float32))
    a = jnp.exp(log_a)
    b = jnp.sqrt(-jnp.expm1(2.0 * log_a)) * (gate_i * u)
    a_cum, h = lax.associative_scan(_linrec_combine, (a, b), axis=1)
    h = h + a_cum * h0[:, None, :]
    return h, h[:, -1]


def rglru_branch(xa_lat, xa_ctx, conv_w, conv_b, w_r, b_r, w_i, b_i, lam):
    u_lat = (dwconv_centred(xa_lat, conv_w) + conv_b).astype(jnp.float32)
    u_ctx = (dwconv_centred(xa_ctx, conv_w) + conv_b).astype(jnp.float32)
    h0 = jnp.zeros((u_ctx.shape[0], D_A), jnp.float32)
    ys_lat, ys_ctx = [], []
    for d in range(2):
        rev = d == 1
        y_c, h_c = rglru_scan(flip_time(u_ctx, 1, rev), h0, w_r[d], b_r[d], w_i[d], b_i[d], lam[d])
        y_l, _ = rglru_scan(flip_time(u_lat, 1, rev), h_c, w_r[d], b_r[d], w_i[d], b_i[d], lam[d])
        ys_ctx.append(flip_time(y_c, 1, rev))
        ys_lat.append(flip_time(y_l, 1, rev))
    return ys_lat[0] + ys_lat[1], ys_ctx[0] + ys_ctx[1]


def gdn_chunked(q, k, v, g, beta, s0):
    bsz, nh, t_len, dk = q.shape
    dv = v.shape[-1]
    cs = CHUNK_B
    n = t_len // cs
    q = (q * dk ** -0.5).reshape(bsz, nh, n, cs, dk)
    k = k.reshape(bsz, nh, n, cs, dk)
    v = v.reshape(bsz, nh, n, cs, dv)
    beta = beta.reshape(bsz, nh, n, cs)
    g = jnp.cumsum(g.reshape(bsz, nh, n, cs), axis=-1)
    causal = jnp.tril(jnp.ones((cs, cs), bool))
    decay = jnp.exp(jnp.where(causal, g[..., :, None] - g[..., None, :], -jnp.inf))
    kb = k * beta[..., None]
    strict = jnp.tril(jnp.einsum('bhnid,bhnjd->bhnij', kb, k) * decay, -1)
    eye = jnp.eye(cs, dtype=jnp.float32)
    t_mat = lax.linalg.triangular_solve(eye + strict, jnp.broadcast_to(eye, strict.shape),
                                        left_side=True, lower=True, unit_diagonal=True)
    u = t_mat @ (v * beta[..., None])
    w = t_mat @ (kb * jnp.exp(g)[..., None])
    qk = jnp.einsum('bhnid,bhnjd->bhnij', q, k) * decay
    g_last = g[..., -1]
    k_tail = k * jnp.exp(g_last[..., None] - g)[..., None]
    q_head = q * jnp.exp(g)[..., None]

    def step(s, inp):
        q_i, k_i, u_i, w_i, qk_i, gl_i = inp
        v_new = u_i - w_i @ s
        o = q_i @ s + qk_i @ v_new
        s = s * jnp.exp(gl_i)[..., None, None] + jnp.swapaxes(k_i, -1, -2) @ v_new
        return s, o

    xs = tuple(jnp.moveaxis(z, 2, 0) for z in (q_head, k_tail, u, w, qk, g_last))
    s_last, o = lax.scan(step, s0, xs)
    o = jnp.moveaxis(o, 0, 2).reshape(bsz, nh, t_len, dv)
    return o, s_last


def _gdn_prep(qkv, conv_w):
    bsz, t_len, _ = qkv.shape
    u = jax.nn.silu(dwconv_centred(qkv, conv_w).astype(jnp.float32))
    q, k, v = jnp.split(u, 3, axis=-1)
    heads = lambda z: z.reshape(bsz, t_len, N_HEADS_B, HEAD_DIM_B).transpose(0, 2, 1, 3)
    return l2_normalize(heads(q)), l2_normalize(heads(k)), heads(v)


def _gdn_gates(beta_raw, alpha_raw, a_log, dt_bias, d):
    sl = slice(d * N_HEADS_B, (d + 1) * N_HEADS_B)
    beta = jax.nn.sigmoid(beta_raw[..., sl].astype(jnp.float32)).transpose(0, 2, 1)
    g = -jnp.exp(a_log[d].astype(jnp.float32)) * jax.nn.softplus(alpha_raw[..., sl].astype(jnp.float32) + dt_bias[d])
    return beta, g.transpose(0, 2, 1)


def gdn_branch(qkv_lat, qkv_ctx, beta_lat, beta_ctx, alpha_lat, alpha_ctx, conv_w, a_log, dt_bias, onorm):
    ql, kl, vl = _gdn_prep(qkv_lat, conv_w)
    qx, kx, vx = _gdn_prep(qkv_ctx, conv_w)
    bsz = ql.shape[0]
    s0 = jnp.zeros((bsz, N_HEADS_B, HEAD_DIM_B, HEAD_DIM_B), jnp.float32)
    os_lat, os_ctx = [], []
    for d in range(2):
        rev = d == 1
        b_l, g_l = _gdn_gates(beta_lat, alpha_lat, a_log, dt_bias, d)
        b_x, g_x = _gdn_gates(beta_ctx, alpha_ctx, a_log, dt_bias, d)
        fl = lambda z: flip_time(z, 2, rev)
        o_x, s_x = gdn_chunked(fl(qx), fl(kx), fl(vx), fl(g_x), fl(b_x), s0)
        o_l, _ = gdn_chunked(fl(ql), fl(kl), fl(vl), fl(g_l), fl(b_l), s_x)
        os_ctx.append(fl(o_x))
        os_lat.append(fl(o_l))

    def finish(o):
        o = rms_norm(o.transpose(0, 2, 1, 3), onorm)
        return o.reshape(o.shape[0], o.shape[1], D_B)

    return finish(os_lat[0] + os_lat[1]), finish(os_ctx[0] + os_ctx[1])


def _attend(q, k, v):
    s = jnp.einsum('bqkgd,bskd->bkgqs', q, k).astype(jnp.float32) * (HEAD_DIM_C ** -0.5)
    p = jax.nn.softmax(s, axis=-1).astype(v.dtype)
    return jnp.einsum('bkgqs,bskd->bqkgd', p, v)


def gqa_branch(q_lat, k_lat, v_lat, q_ctx, k_ctx, v_ctx, qn, kn, cos, sin, need_ctx_out):
    bsz, t_len, _ = q_lat.shape
    grp = N_Q_HEADS_C // N_KV_HEADS_C

    def heads(q, k, v):
        n_tok = q.shape[1]
        q = rms_norm(q.reshape(bsz, n_tok, N_KV_HEADS_C, grp, HEAD_DIM_C), qn)
        k = rms_norm(k.reshape(bsz, n_tok, N_KV_HEADS_C, HEAD_DIM_C), kn)
        return q, k, v.reshape(bsz, n_tok, N_KV_HEADS_C, HEAD_DIM_C)

    ql, kl, vl = heads(q_lat, k_lat, v_lat)
    qx, kx, vx = heads(q_ctx, k_ctx, v_ctx)
    ql = apply_rope(ql.reshape(bsz, t_len, N_Q_HEADS_C, HEAD_DIM_C), cos, sin).reshape(ql.shape)
    kl = apply_rope(kl, cos, sin)
    keys = jnp.concatenate([kl, kx], axis=1)
    vals = jnp.concatenate([vl, vx], axis=1)
    n_blk = t_len // Q_BLOCK
    qb = ql.reshape(bsz, n_blk, Q_BLOCK, N_KV_HEADS_C, grp, HEAD_DIM_C).transpose(1, 0, 2, 3, 4, 5)
    ob = lax.map(lambda qi: _attend(qi, keys, vals), qb)
    o_lat = ob.transpose(1, 0, 2, 3, 4, 5).reshape(bsz, t_len, D_Q_C)
    o_ctx = None
    if need_ctx_out:
        o_ctx = _attend(qx, kx, vx).reshape(bsz, qx.shape[1], D_Q_C)
    return o_lat, o_ctx


def hybrid_layer(h_lat, h_ctx, c, c_ctx, norm_g, w_mod, b_mod, w_in, conv_a_w, conv_a_b, w_ra, b_ra,
                 w_ia, b_ia, lam_a, conv_b_w, a_log_b, dt_bias_b, onorm_b, qn_c, kn_c, w_out, cos, sin, last):
    shift, scale, gate = jnp.split(jax.nn.silu(c) @ w_mod + b_mod, 3, axis=-1)
    shift_x, scale_x, gate_x = jnp.split(jax.nn.silu(c_ctx) @ w_mod + b_mod, 3, axis=-1)
    z_lat = (rms_norm(h_lat, norm_g) * (1.0 + scale[:, None]) + shift[:, None]) @ w_in
    z_ctx = (rms_norm(h_ctx, norm_g) * (1.0 + scale_x) + shift_x) @ w_in
    xa_l, ga_l, qkvb_l, gb_l, beta_l, alpha_l, qc_l, kc_l, vc_l, gc_l = split_cols(z_lat)
    xa_x, ga_x, qkvb_x, gb_x, beta_x, alpha_x, qc_x, kc_x, vc_x, gc_x = split_cols(z_ctx)

    ya_l, ya_x = rglru_branch(xa_l, xa_x, conv_a_w, conv_a_b, w_ra, b_ra, w_ia, b_ia, lam_a)
    yb_l, yb_x = gdn_branch(qkvb_l, qkvb_x, beta_l, beta_x, alpha_l, alpha_x, conv_b_w, a_log_b, dt_bias_b, onorm_b)
    yc_l, yc_x = gqa_branch(qc_l, kc_l, vc_l, qc_x, kc_x, vc_x, qn_c, kn_c, cos, sin, not last)

    mix_lat = jnp.concatenate([ya_l * jax.nn.silu(ga_l), yb_l * jax.nn.silu(gb_l),
                               yc_l * jax.nn.silu(gc_l)], axis=-1) @ w_out
    h_lat = h_lat + (gate[:, None] * mix_lat).astype(h_lat.dtype)
    if not last:
        mix_ctx = jnp.concatenate([ya_x * jax.nn.silu(ga_x), yb_x * jax.nn.silu(gb_x),
                                   yc_x * jax.nn.silu(gc_x)], axis=-1) @ w_out
        h_ctx = h_ctx + (gate_x * mix_ctx).astype(h_ctx.dtype)
    return h_lat, h_ctx


def setup_inputs(seed: int = 0) -> dict:
    key = jax.random.key(seed)
    ks = jax.random.split(key, 24)
    f32 = jnp.float32
    L, D = DEPTH, D_MODEL

    def nrm(k, shape, s):
        return jax.random.normal(k, shape, f32) * s

    def gain(k, shape):
        return 1.0 + 0.02 * jax.random.normal(k, shape, f32)

    u = jax.random.uniform(ks[14], (L, 2, D_A), f32, 0.9, 0.999)
    s = u ** (1.0 / LRU_C)
    lam_a = jnp.log(s) - jnp.log1p(-s)
    a_log_b = jnp.log(jax.random.uniform(ks[16], (L, 2, N_HEADS_B), f32, 1.0, 16.0))
    dt = jnp.exp(jax.random.uniform(ks[17], (L, 2, N_HEADS_B), f32, math.log(1e-3), math.log(1e-1)))
    dt_bias_b = dt + jnp.log(-jnp.expm1(-dt))
    return {
        'x': nrm(ks[0], (BATCH, SEQ, D), 1.0),
        'c': nrm(ks[1], (BATCH, D), 1.0),
        'ctx': nrm(ks[2], (BATCH, CTX_LEN, D), 1.0),
        'c_ctx': nrm(ks[3], (D,), 1.0),
        'norm_g': gain(ks[4], (L, D)),
        'w_mod': nrm(ks[5], (L, D, 3 * D), 0.5 * D ** -0.5),
        'b_mod': nrm(ks[6], (L, 3 * D), 0.02),
        'w_in': nrm(ks[7], (L, D, D_IN), D ** -0.5),
        'conv_a_w': nrm(ks[8], (L, CONV_A, D_A), CONV_A ** -0.5),
        'conv_a_b': nrm(ks[9], (L, D_A), 0.02),
        'w_ra': nrm(ks[10], (L, 2, N_BLK_A, BLK_A, BLK_A), BLK_A ** -0.5),
        'b_ra': nrm(ks[11], (L, 2, D_A), 0.02),
        'w_ia': nrm(ks[12], (L, 2, N_BLK_A, BLK_A, BLK_A), BLK_A ** -0.5),
        'b_ia': nrm(ks[13], (L, 2, D_A), 0.02),
        'lam_a': lam_a,
        'conv_b_w': nrm(ks[15], (L, CONV_B, 3 * D_B), CONV_B ** -0.5),
        'a_log_b': a_log_b,
        'dt_bias_b': dt_bias_b,
        'onorm_b': gain(ks[18], (L, HEAD_DIM_B)),
        'qn_c': gain(ks[19], (L, HEAD_DIM_C)),
        'kn_c': gain(ks[20], (L, HEAD_DIM_C)),
        'w_out': nrm(ks[21], (L, D_MIX, D), D_MIX ** -0.5),
        'final_g': gain(ks[22], (D,)),
    }


def reference(x, c, ctx, c_ctx, norm_g, w_mod, b_mod, w_in, conv_a_w, conv_a_b, w_ra, b_ra, w_ia, b_ia,
              lam_a, conv_b_w, a_log_b, dt_bias_b, onorm_b, qn_c, kn_c, w_out, final_g):
    n_tok = x.shape[1]
    rows = n_tok // GRID_W
    row = jnp.repeat(jnp.arange(rows, dtype=jnp.int32), GRID_W)
    col = jnp.tile(jnp.arange(GRID_W, dtype=jnp.int32), rows)
    cos, sin = axial_rope(row, col)
    h_lat, h_ctx = x, ctx
    for l in range(DEPTH):
        h_lat, h_ctx = hybrid_layer(
            h_lat, h_ctx, c, c_ctx, norm_g[l], w_mod[l], b_mod[l], w_in[l], conv_a_w[l], conv_a_b[l],
            w_ra[l], b_ra[l], w_ia[l], b_ia[l], lam_a[l], conv_b_w[l], a_log_b[l], dt_bias_b[l],
            onorm_b[l], qn_c[l], kn_c[l], w_out[l], cos, sin, l == DEPTH - 1)
    return rms_norm(h_lat, final_g)
```

```python
import functools
import math

import numpy as np
import jax
import jax.numpy as jnp
from jax import lax
from jax.experimental import pallas as pl
from jax.experimental.pallas import tpu as pltpu

F32 = jnp.float32
BF16 = jnp.bfloat16

D_MODEL = 2048
GRID_W = 64
EPS = 1e-6
ROPE_BASE = 10000.0

D_A = 512
N_BLK_A = 8
BLK_A = D_A // N_BLK_A
LRU_C = 8.0

N_HEADS_B = 4
HEAD_DIM_B = 128
D_B = N_HEADS_B * HEAD_DIM_B
CHUNK_B = 64

N_Q_HEADS_C = 8
N_KV_HEADS_C = 2
GROUP_C = N_Q_HEADS_C // N_KV_HEADS_C
HEAD_DIM_C = 128
D_Q_C = N_Q_HEADS_C * HEAD_DIM_C
D_KV_C = N_KV_HEADS_C * HEAD_DIM_C

D_MIX = D_A + D_B + D_Q_C

COL_QC = 0
COL_GC = 1024
COL_KV = 2048
COL_XA = 2560
COL_QKVB = 3072
COL_GA = 4608
COL_GB = 5120
COL_BA = 5632
D_Z = 5760
LANES = 128
SUBLANES = 8

VMEM_LIMIT = 56 * 1024 * 1024


def _cparams(*sem):
    return pltpu.CompilerParams(dimension_semantics=tuple(sem), vmem_limit_bytes=VMEM_LIMIT)


def _dot(a, b):
    return jnp.dot(a, b, preferred_element_type=F32)


def _dot_nt(a, b):
    return lax.dot_general(a, b, (((1,), (1,)), ((), ())), preferred_element_type=F32)


def _split2(a):
    hi = a.astype(BF16)
    lo = (a - hi.astype(F32)).astype(BF16)
    return hi, lo


def _split3(a):
    hi = a.astype(BF16)
    r = a - hi.astype(F32)
    mid = r.astype(BF16)
    lo = (r - mid.astype(F32)).astype(BF16)
    return hi, mid, lo


def _dot3(a, b):
    ah, al = _split2(a)
    bh, bl = _split2(b)
    return _dot(ah, bh) + (_dot(al, bh) + _dot(ah, bl))


def _silu(x):
    return x * jax.nn.sigmoid(x)


def _softplus(x):
    return jnp.maximum(x, 0.0) + jnp.log1p(jnp.exp(-jnp.abs(x)))


def _pick(n, cands):
    for c in cands:
        if n % c == 0:
            return c
    raise ValueError(f"no tile for {n} in {cands}")


def _mod_kernel(c_ref, w_ref, b_ref, o_ref):
    o_ref[0] = _dot3(_silu(c_ref[...]), w_ref[0]) + b_ref[0]


def _modulation(cc, w_mod, b_mod):
    depth, d, d3 = w_mod.shape
    tn = 768
    return pl.pallas_call(
        _mod_kernel,
        grid=(depth, d3 // tn),
        in_specs=[
            pl.BlockSpec((SUBLANES, d), lambda l, j: (0, 0)),
            pl.BlockSpec((1, d, tn), lambda l, j: (l, 0, j)),
            pl.BlockSpec((1, 1, tn), lambda l, j: (l, 0, j)),
        ],
        out_specs=pl.BlockSpec((1, SUBLANES, tn), lambda l, j: (l, 0, j)),
        out_shape=jax.ShapeDtypeStruct((depth, SUBLANES, d3), F32),
        compiler_params=_cparams("parallel", "parallel"),
        name="modulation",
    )(cc, w_mod, b_mod.reshape(depth, 1, d3))


def _inproj_kernel(h_ref, mod_ref, g_ref, w_ref, z_ref, y_scr, *, lc, tm):
    m = pl.program_id(0)

    @pl.when(pl.program_id(1) == 0)
    def _():
        x = h_ref[...]
        xn = x * lax.rsqrt(jnp.mean(x * x, axis=-1, keepdims=True) + EPS) * g_ref[...]
        row = m * tm + lax.broadcasted_iota(jnp.int32, (tm, 1), 0)
        is_ctx = row < lc
        shift = jnp.where(is_ctx, mod_ref[1:2, 0:D_MODEL], mod_ref[0:1, 0:D_MODEL])
        scale = jnp.where(is_ctx, mod_ref[1:2, D_MODEL:2 * D_MODEL], mod_ref[0:1, D_MODEL:2 * D_MODEL])
        y_scr[...] = (xn * (1.0 + scale) + shift).astype(BF16)

    z_ref[...] = _dot(y_scr[...], w_ref[...])


def _inproj(h, mod_l, norm_g_l, w_l, lc):
    n = h.shape[0]
    tm = _pick(n, (768, 512, 384, 256))
    tn = 1152
    return pl.pallas_call(
        functools.partial(_inproj_kernel, lc=lc, tm=tm),
        grid=(n // tm, D_Z // tn),
        in_specs=[
            pl.BlockSpec((tm, D_MODEL), lambda i, j: (i, 0)),
            pl.BlockSpec((SUBLANES, 3 * D_MODEL), lambda i, j: (0, 0)),
            pl.BlockSpec((1, D_MODEL), lambda i, j: (0, 0)),
            pl.BlockSpec((D_MODEL, tn), lambda i, j: (0, j)),
        ],
        out_specs=pl.BlockSpec((tm, tn), lambda i, j: (i, j)),
        out_shape=jax.ShapeDtypeStruct((n, D_Z), F32),
        scratch_shapes=[pltpu.VMEM((tm, D_MODEL), BF16)],
        compiler_params=_cparams("parallel", "arbitrary"),
        name="inproj",
    )(h, mod_l, norm_g_l, w_l)


def _attn_prep_kernel(q_ref, kv_ref, cos_ref, sin_ref, qn_ref, kn_ref, qo_ref, ko_ref, vo_ref):
    cos = cos_ref[...]
    sin = sin_ref[...]

    def prep(x, gain, scale):
        y = x * lax.rsqrt(jnp.mean(x * x, axis=-1, keepdims=True) + EPS) * gain
        rot = pltpu.roll(y, HEAD_DIM_C // 2, 1)
        return (y * cos + rot * sin) * scale

    for hd in range(N_Q_HEADS_C):
        sl = slice(hd * HEAD_DIM_C, (hd + 1) * HEAD_DIM_C)
        qo_ref[:, sl] = prep(q_ref[:, sl], qn_ref[...], HEAD_DIM_C ** -0.5).astype(BF16)
    for hd in range(N_KV_HEADS_C):
        sl = slice(hd * HEAD_DIM_C, (hd + 1) * HEAD_DIM_C)
        ko_ref[:, sl] = prep(kv_ref[:, sl], kn_ref[...], 1.0).astype(BF16)
    vo_ref[...] = kv_ref[:, D_KV_C:2 * D_KV_C].astype(BF16)


def _attn_prep(z, cos2, sin2, qn_l, kn_l):
    n = z.shape[0]
    tm = 256
    return pl.pallas_call(
        _attn_prep_kernel,
        grid=(n // tm,),
        in_specs=[
            pl.BlockSpec((tm, D_Q_C), lambda i: (i, COL_QC // D_Q_C)),
            pl.BlockSpec((tm, 2 * D_KV_C), lambda i: (i, COL_KV // (2 * D_KV_C))),
            pl.BlockSpec((tm, HEAD_DIM_C), lambda i: (i, 0)),
            pl.BlockSpec((tm, HEAD_DIM_C), lambda i: (i, 0)),
            pl.BlockSpec((1, HEAD_DIM_C), lambda i: (0, 0)),
            pl.BlockSpec((1, HEAD_DIM_C), lambda i: (0, 0)),
        ],
        out_specs=[
            pl.BlockSpec((tm, D_Q_C), lambda i: (i, 0)),
            pl.BlockSpec((tm, D_KV_C), lambda i: (i, 0)),
            pl.BlockSpec((tm, D_KV_C), lambda i: (i, 0)),
        ],
        out_shape=[
            jax.ShapeDtypeStruct((n, D_Q_C), BF16),
            jax.ShapeDtypeStruct((n, D_KV_C), BF16),
            jax.ShapeDtypeStruct((n, D_KV_C), BF16),
        ],
        compiler_params=_cparams("parallel"),
        name="attn_prep",
    )(z, z, cos2, sin2, qn_l, kn_l)


def _attn_kernel(q_ref, k_ref, v_ref, o_ref, m_scr, l_scr, acc_scr, *, lc, tq, tk, n_lat_blocks):
    qt = pl.program_id(1)
    q = jnp.concatenate(
        [q_ref[:, g * HEAD_DIM_C:(g + 1) * HEAD_DIM_C] for g in range(GROUP_C)], axis=0)

    s0 = _dot_nt(q, k_ref[0:lc, :])
    m0 = jnp.max(s0, axis=-1, keepdims=True)
    p0 = jnp.exp(s0 - m0)
    m_scr[...] = m0
    l_scr[...] = jnp.sum(p0, axis=-1, keepdims=True)
    acc_scr[...] = _dot(p0.astype(BF16), v_ref[0:lc, :])

    def body(j, carry):
        start = pl.multiple_of(lc + j * tk, tk)
        s = _dot_nt(q, k_ref[pl.ds(start, tk), :])
        m_old = m_scr[...]
        m_new = jnp.maximum(m_old, jnp.max(s, axis=-1, keepdims=True))
        alpha = jnp.exp(m_old - m_new)
        p = jnp.exp(s - m_new)
        l_scr[...] = alpha * l_scr[...] + jnp.sum(p, axis=-1, keepdims=True)
        acc_scr[...] = alpha * acc_scr[...] + _dot(p.astype(BF16), v_ref[pl.ds(start, tk), :])
        m_scr[...] = m_new
        return carry

    n_blocks = jnp.where(qt * tq < lc, 0, n_lat_blocks)
    lax.fori_loop(0, n_blocks, body, 0)

    o = acc_scr[...] / l_scr[...]
    for g in range(GROUP_C):
        o_ref[:, g * HEAD_DIM_C:(g + 1) * HEAD_DIM_C] = o[g * tq:(g + 1) * tq]


def _attention(qh, kh, vh, lc):
    n = qh.shape[0]
    tq = 256
    assert lc % tq == 0
    tk = _pick(n - lc, (512, 256))
    gw = GROUP_C * HEAD_DIM_C
    return pl.pallas_call(
        functools.partial(_attn_kernel, lc=lc, tq=tq, tk=tk, n_lat_blocks=(n - lc) // tk),
        grid=(N_KV_HEADS_C, n // tq),
        in_specs=[
            pl.BlockSpec((tq, gw), lambda j, i: (i, j)),
            pl.BlockSpec((n, HEAD_DIM_C), lambda j, i: (0, j)),
            pl.BlockSpec((n, HEAD_DIM_C), lambda j, i: (0, j)),
        ],
        out_specs=pl.BlockSpec((tq, gw), lambda j, i: (i, j)),
        out_shape=jax.ShapeDtypeStruct((n, D_Q_C), F32),
        scratch_shapes=[
            pltpu.VMEM((GROUP_C * tq, 1), F32),
            pltpu.VMEM((GROUP_C * tq, 1), F32),
            pltpu.VMEM((GROUP_C * tq, HEAD_DIM_C), F32),
        ],
        compiler_params=_cparams("parallel", "parallel"),
        name="attention",
    )(qh, kh, vh)


def _conv4(x, prev8, next8, w_ref, prev_ok, next_ok):
    tm = x.shape[0]
    row = lax.broadcasted_iota(jnp.int32, (tm, 1), 0)
    p = jnp.where(prev_ok, prev8, 0.0)
    nx = jnp.where(next_ok, next8, 0.0)
    xm1 = jnp.where(row == 0, p[7:8], pltpu.roll(x, 1, 0))
    xm2 = jnp.where(row == 0, p[6:7], jnp.where(row == 1, p[7:8], pltpu.roll(x, 2, 0)))
    xp1 = jnp.where(row == tm - 1, nx[0:1], pltpu.roll(x, tm - 1, 0))
    return w_ref[0:1] * xm2 + w_ref[1:2] * xm1 + w_ref[2:3] * x + w_ref[3:4] * xp1


def _seg_flags(t, nbc, nt):
    prev_ok = jnp.logical_and(t != 0, t != nbc)
    next_ok = jnp.logical_and(t != nbc - 1, t != nt - 1)
    return prev_ok, next_ok


def _halo_maps(tile_of, tm, n, col_block):
    r = tm // SUBLANES
    last = n // SUBLANES - 1
    main = lambda i: (tile_of(i), col_block)
    prev = lambda i: (jnp.maximum(tile_of(i) * r - 1, 0), col_block)
    nxt = lambda i: (jnp.minimum((tile_of(i) + 1) * r, last), col_block)
    return main, prev, nxt


def _rglru_kernel(xf_ref, pf_ref, nf_ref, xb_ref, pb_ref, nb_ref, cw_ref, cb_ref, wg_ref, bg_ref,
                  lam_ref, of_ref, ob_ref, a_scr, b_scr, hcar, *, tm, nbc, nt):
    i = pl.program_id(0)
    seg = tm // SUBLANES

    @pl.when(i == 0)
    def _():
        hcar[...] = jnp.zeros_like(hcar)

    tiles = (i, jnp.where(i < nbc, nbc - 1 - i, nt - 1 - (i - nbc)))
    refs = ((xf_ref, pf_ref, nf_ref, of_ref), (xb_ref, pb_ref, nb_ref, ob_ref))
    for d in range(2):
        x_ref, p_ref, n_ref, o_ref = refs[d]
        prev_ok, next_ok = _seg_flags(tiles[d], nbc, nt)
        u = _conv4(x_ref[...], p_ref[...], n_ref[...], cw_ref, prev_ok, next_ok) + cb_ref[...]
        gates = _dot(u.astype(BF16), wg_ref[d]) + bg_ref[d]
        gate_r = jax.nn.sigmoid(gates[:, 0:D_A])
        gate_i = jax.nn.sigmoid(gates[:, D_A:2 * D_A])
        log_a = (-LRU_C) * gate_r * _softplus(-lam_ref[d:d + 1, :])
        a = jnp.exp(log_a)
        b = jnp.sqrt(-jnp.tanh(log_a) * (a * a + 1.0)) * (gate_i * u)

        for cg in range(D_A // LANES):
            lanes = slice(cg * LANES, (cg + 1) * LANES)
            a_scr[cg] = a[:, lanes]
            b_scr[cg] = b[:, lanes]
            hloc = jnp.zeros((SUBLANES, LANES), F32)
            pcum = jnp.ones((SUBLANES, LANES), F32)
            for j in (range(seg) if d == 0 else range(seg - 1, -1, -1)):
                rows = pl.ds(j, SUBLANES, stride=seg)
                aj = a_scr[cg, rows, :]
                hloc = aj * hloc + b_scr[cg, rows, :]
                pcum = aj * pcum
                b_scr[cg, rows, :] = hloc
                a_scr[cg, rows, :] = pcum
            carry = hcar[d:d + 1, lanes]
            seg_in = [None] * SUBLANES
            for s in (range(SUBLANES) if d == 0 else range(SUBLANES - 1, -1, -1)):
                seg_in[s] = carry
                carry = pcum[s:s + 1] * carry + hloc[s:s + 1]
            hcar[d:d + 1, lanes] = carry
            cin = jnp.concatenate(seg_in, axis=0)
            for j in range(seg):
                rows = pl.ds(j, SUBLANES, stride=seg)
                b_scr[cg, rows, :] = b_scr[cg, rows, :] + a_scr[cg, rows, :] * cin
            o_ref[:, lanes] = b_scr[cg]


def _rglru(z, cw, cb, wg, bg, lam, lc):
    n = z.shape[0]
    tm = 256
    assert lc % tm == 0 and n % tm == 0
    nbc, nt = lc // tm, n // tm
    fwd = lambda i: i
    bwd = lambda i: jnp.where(i < nbc, nbc - 1 - i, nt - 1 - (i - nbc))
    cblk = COL_XA // D_A
    fm, fp, fn = _halo_maps(fwd, tm, n, cblk)
    bm, bp, bn = _halo_maps(bwd, tm, n, cblk)
    const2 = lambda i: (0, 0)
    const3 = lambda i: (0, 0, 0)
    return pl.pallas_call(
        functools.partial(_rglru_kernel, tm=tm, nbc=nbc, nt=nt),
        grid=(nt,),
        in_specs=[
            pl.BlockSpec((tm, D_A), fm), pl.BlockSpec((SUBLANES, D_A), fp), pl.BlockSpec((SUBLANES, D_A), fn),
            pl.BlockSpec((tm, D_A), bm), pl.BlockSpec((SUBLANES, D_A), bp), pl.BlockSpec((SUBLANES, D_A), bn),
            pl.BlockSpec((4, D_A), const2), pl.BlockSpec((1, D_A), const2),
            pl.BlockSpec((2, D_A, 2 * D_A), const3), pl.BlockSpec((2, 1, 2 * D_A), const3),
            pl.BlockSpec((2, D_A), const2),
        ],
        out_specs=[pl.BlockSpec((tm, D_A), lambda i: (fwd(i), 0)),
                   pl.BlockSpec((tm, D_A), lambda i: (bwd(i), 0))],
        out_shape=[jax.ShapeDtypeStruct((n, D_A), F32)] * 2,
        scratch_shapes=[pltpu.VMEM((D_A // LANES, tm, LANES), F32), pltpu.VMEM((D_A // LANES, tm, LANES), F32),
                        pltpu.VMEM((SUBLANES, D_A), F32)],
        compiler_params=_cparams("arbitrary"),
        name="rglru",
    )(z, z, z, z, z, z, cw, cb, wg, bg, lam)


def _gdn_prep_kernel(x_ref, p_ref, n_ref, ba_ref, cw_ref, av_ref, dt_ref, q_ref, k_ref, v_ref, bg_ref,
                     *, nbc, nt):
    prev_ok, next_ok = _seg_flags(pl.program_id(0), nbc, nt)
    u = _silu(_conv4(x_ref[...], p_ref[...], n_ref[...], cw_ref, prev_ok, next_ok))
    for hd in range(N_HEADS_B):
        sl = slice(hd * HEAD_DIM_B, (hd + 1) * HEAD_DIM_B)
        qv = u[:, hd * HEAD_DIM_B:(hd + 1) * HEAD_DIM_B]
        kv = u[:, D_B + hd * HEAD_DIM_B:D_B + (hd + 1) * HEAD_DIM_B]
        q_ref[:, sl] = qv * lax.rsqrt(jnp.sum(qv * qv, axis=-1, keepdims=True) + EPS)
        k_ref[:, sl] = kv * lax.rsqrt(jnp.sum(kv * kv, axis=-1, keepdims=True) + EPS)
    v_ref[...] = u[:, 2 * D_B:3 * D_B]
    ba = ba_ref[...]
    lane = lax.broadcasted_iota(jnp.int32, ba.shape, 1)
    beta = jax.nn.sigmoid(ba)
    g = -jnp.exp(av_ref[...]) * _softplus(ba + dt_ref[...])
    bg_ref[...] = jnp.where(lane < 2 * N_HEADS_B, beta, jnp.where(lane < 4 * N_HEADS_B, g, 0.0))


def _gdn_prep(z, cw, avec, dtvec, lc):
    n = z.shape[0]
    tm = 256
    nbc, nt = lc // tm, n // tm
    w3 = 3 * D_B
    assert COL_QKVB % w3 == 0
    m, p, nx = _halo_maps(lambda i: i, tm, n, COL_QKVB // w3)
    const2 = lambda i: (0, 0)
    return pl.pallas_call(
        functools.partial(_gdn_prep_kernel, nbc=nbc, nt=nt),
        grid=(nt,),
        in_specs=[
            pl.BlockSpec((tm, w3), m), pl.BlockSpec((SUBLANES, w3), p), pl.BlockSpec((SUBLANES, w3), nx),
            pl.BlockSpec((tm, LANES), lambda i: (i, COL_BA // LANES)),
            pl.BlockSpec((4, w3), const2), pl.BlockSpec((1, LANES), const2), pl.BlockSpec((1, LANES), const2),
        ],
        out_specs=[pl.BlockSpec((tm, D_B), lambda i: (i, 0))] * 3 + [pl.BlockSpec((tm, LANES), lambda i: (i, 0))],
        out_shape=[jax.ShapeDtypeStruct((n, D_B), F32)] * 3 + [jax.ShapeDtypeStruct((n, LANES), F32)],
        compiler_params=_cparams("parallel"),
        name="gdn_prep",
    )(z, z, z, z, cw, avec, dtvec)


def _bdmm(x, y, lane_lo):
    outs = []
    for pr in range(2):
        xp = x[:, pr * LANES:(pr + 1) * LANES]
        yp = y[:, pr * LANES:(pr + 1) * LANES]
        bd = jnp.concatenate([jnp.where(lane_lo, yp, 0.0), jnp.where(lane_lo, 0.0, yp)], axis=0)
        outs.append(_dot3(xp, bd))
    return jnp.concatenate(outs, axis=1)


def _unit_tri_inverse(mat, row, col, lane_lo):
    def same(shift):
        return (row >> shift) == (col >> shift)

    m8 = jnp.where(same(3), mat, 0.0)
    x = jnp.where(row == col, 1.0, 0.0) - m8
    pw = _bdmm(m8, m8, lane_lo)
    x = x + _bdmm(x, pw, lane_lo)
    pw = _bdmm(pw, pw, lane_lo)
    x = x + _bdmm(x, pw, lane_lo)
    for shift in (4, 5, 6):
        off = jnp.where(same(shift), jnp.where(same(shift - 1), 0.0, mat), 0.0)
        x = x - _bdmm(x, _bdmm(off, x, lane_lo), lane_lo)
    return x


def _gdn_chunk_kernel(q_ref, k_ref, v_ref, bg_ref, e64_ref, eb_ref, eg_ref, wq_ref, u_ref, l2_ref, egl_ref):
    cs = CHUNK_B
    nh = N_HEADS_B
    qn = q_ref[...]
    kn = k_ref[...]
    v = v_ref[...]
    bg_parts = _split3(bg_ref[...])
    ri = lax.broadcasted_iota(jnp.int32, (cs, cs), 0)
    ci = lax.broadcasted_iota(jnp.int32, (cs, cs), 1)
    ones = jnp.ones((cs, cs), BF16)
    row = lax.broadcasted_iota(jnp.int32, (cs, nh * cs), 0)
    col = lax.broadcasted_iota(jnp.int32, (cs, nh * cs), 1) & (cs - 1)
    lane_lo = lax.broadcasted_iota(jnp.int32, (cs, LANES), 1) < cs
    lane_lo2 = lax.broadcasted_iota(jnp.int32, (2 * cs, LANES), 1) < cs
    zeros = jnp.zeros((cs, HEAD_DIM_B), F32)
    qs = qn * (HEAD_DIM_B ** -0.5)

    def expand(e_ref_d):
        return [_dot(part, e_ref_d).astype(BF16) for part in bg_parts]

    def dot_parts(lhs, parts):
        return _dot(lhs, parts[0]) + (_dot(lhs, parts[1]) + _dot(lhs, parts[2]))

    for d in range(2):
        if d == 0:
            tri = jnp.where(ci <= ri, 1.0, 0.0).astype(BF16)
            strict = row > col
            causal = row >= col
        else:
            tri = jnp.where(ci >= ri, 1.0, 0.0).astype(BF16)
            strict = row < col
            causal = row <= col
        g64 = [jnp.where(strict, part, 0.0).astype(BF16) for part in expand(e64_ref[d])]
        decay = jnp.where(causal, jnp.exp(dot_parts(tri, g64)), 0.0)
        beta_parts = expand(eb_ref[d])
        beta = beta_parts[0].astype(F32) + (beta_parts[1].astype(F32) + beta_parts[2].astype(F32))
        g_parts = expand(eg_ref[d])
        gc = dot_parts(tri, g_parts)
        gl = dot_parts(ones, g_parts)
        egc = jnp.exp(gc)
        etail = jnp.exp(gl - gc)
        egl = jnp.exp(gl)

        kb = kn * beta
        vb = v * beta
        kbe = kb * egc
        qhead = qs * egc
        ktail = kn * etail
        lhs = jnp.concatenate([kb, qs], axis=0).astype(BF16)
        prods = []
        for pr in range(2):
            ka = kn[:, (2 * pr) * HEAD_DIM_B:(2 * pr + 1) * HEAD_DIM_B]
            kb_ = kn[:, (2 * pr + 1) * HEAD_DIM_B:(2 * pr + 2) * HEAD_DIM_B]
            rhs_t = jnp.concatenate([jnp.concatenate([ka, zeros], axis=1),
                                     jnp.concatenate([zeros, kb_], axis=1)], axis=0).astype(BF16)
            prods.append(_dot_nt(lhs[:, 2 * pr * HEAD_DIM_B:(2 * pr + 2) * HEAD_DIM_B], rhs_t))
        prod = jnp.concatenate(prods, axis=1)
        mat = jnp.where(strict, prod[0:cs] * decay, 0.0)
        qk = prod[cs:2 * cs] * decay
        tmat = _unit_tri_inverse(mat, row, col, lane_lo)

        for pr in range(2):
            ha, hb = 2 * pr, 2 * pr + 1
            sa = slice(ha * HEAD_DIM_B, (ha + 1) * HEAD_DIM_B)
            sb = slice(hb * HEAD_DIM_B, (hb + 1) * HEAD_DIM_B)
            tp = tmat[:, pr * LANES:(pr + 1) * LANES]
            rhs = jnp.concatenate([jnp.concatenate([vb[:, sa], kbe[:, sa]], axis=1),
                                   jnp.concatenate([vb[:, sb], kbe[:, sb]], axis=1)], axis=0)
            ra = _dot3(jnp.where(lane_lo, tp, 0.0), rhs)
            rb = _dot3(jnp.where(lane_lo, 0.0, tp), rhs)
            for hh, res, sl in ((ha, ra, sa), (hb, rb, sb)):
                u_ref[0, d * nh + hh] = res[:, 0:HEAD_DIM_B]
                wq_ref[0, d * nh + hh] = jnp.concatenate(
                    [res[:, HEAD_DIM_B:2 * HEAD_DIM_B], qhead[:, sl]], axis=0).astype(BF16)
            kt_t = jnp.concatenate([ktail[:, sa], ktail[:, sb]], axis=0).T
            qkp = qk[:, pr * LANES:(pr + 1) * LANES]
            l2_ref[0, d * 2 + pr] = jnp.concatenate(
                [jnp.where(lane_lo, qkp, 0.0), jnp.where(lane_lo, 0.0, qkp),
                 jnp.where(lane_lo2, kt_t, 0.0), jnp.where(lane_lo2, 0.0, kt_t)], axis=0).astype(BF16)
        egl_ref[0, d] = jnp.concatenate(
            [egl[0:1, hh * HEAD_DIM_B:(hh + 1) * HEAD_DIM_B] for hh in range(nh)], axis=0)


def _gdn_expanders():
    nh, cs = N_HEADS_B, CHUNK_B
    e64 = np.zeros((2, LANES, nh * cs), np.float32)
    eb = np.zeros((2, LANES, nh * HEAD_DIM_B), np.float32)
    eg = np.zeros((2, LANES, nh * HEAD_DIM_B), np.float32)
    for d in range(2):
        for hh in range(nh):
            p = d * nh + hh
            e64[d, 2 * nh + p, hh * cs:(hh + 1) * cs] = 1.0
            eb[d, p, hh * HEAD_DIM_B:(hh + 1) * HEAD_DIM_B] = 1.0
            eg[d, 2 * nh + p, hh * HEAD_DIM_B:(hh + 1) * HEAD_DIM_B] = 1.0
    return jnp.asarray(e64, BF16), jnp.asarray(eb, BF16), jnp.asarray(eg, BF16)


def _gdn_chunk(qn, kn, v, bg, expanders):
    n = qn.shape[0]
    cs, nh = CHUNK_B, N_HEADS_B
    nc = n // cs
    e64, eb, eg = expanders
    const3 = lambda c: (0, 0, 0)
    return pl.pallas_call(
        _gdn_chunk_kernel,
        grid=(nc,),
        in_specs=[
            pl.BlockSpec((cs, D_B), lambda c: (c, 0)), pl.BlockSpec((cs, D_B), lambda c: (c, 0)),
            pl.BlockSpec((cs, D_B), lambda c: (c, 0)), pl.BlockSpec((cs, LANES), lambda c: (c, 0)),
            pl.BlockSpec(e64.shape, const3), pl.BlockSpec(eb.shape, const3), pl.BlockSpec(eg.shape, const3),
        ],
        out_specs=[
            pl.BlockSpec((1, 2 * nh, 2 * cs, HEAD_DIM_B), lambda c: (c, 0, 0, 0)),
            pl.BlockSpec((1, 2 * nh, cs, HEAD_DIM_B), lambda c: (c, 0, 0, 0)),
            pl.BlockSpec((1, 4, 6 * cs, LANES), lambda c: (c, 0, 0, 0)),
            pl.BlockSpec((1, 2, nh, HEAD_DIM_B), lambda c: (c, 0, 0, 0)),
        ],
        out_shape=[
            jax.ShapeDtypeStruct((nc, 2 * nh, 2 * cs, HEAD_DIM_B), BF16),
            jax.ShapeDtypeStruct((nc, 2 * nh, cs, HEAD_DIM_B), F32),
            jax.ShapeDtypeStruct((nc, 4, 6 * cs, LANES), BF16),
            jax.ShapeDtypeStruct((nc, 2, nh, HEAD_DIM_B), F32),
        ],
        compiler_params=_cparams("parallel"),
        name="gdn_chunk",
    )(qn, kn, v, bg, e64, eb, eg)


def _gdn_rec_kernel(wqf_ref, uf_ref, l2f_ref, egf_ref, wqb_ref, ub_ref, l2b_ref, egb_ref,
                    of_ref, ob_ref, s_scr):
    cs, nh = CHUNK_B, N_HEADS_B

    @pl.when(pl.program_id(0) == 0)
    def _():
        s_scr[...] = jnp.zeros_like(s_scr)

    dirs = ((wqf_ref, uf_ref, l2f_ref, egf_ref, of_ref), (wqb_ref, ub_ref, l2b_ref, egb_ref, ob_ref))
    for d in range(2):
        wq_ref, u_ref, l2_ref, eg_ref, o_ref = dirs[d]
        for pr in range(2):
            v_new, o_state = [], []
            for e in range(2):
                hh = 2 * pr + e
                r = _dot(wq_ref[0, hh], s_scr[d * nh + hh].astype(BF16))
                v_new.append(u_ref[0, hh] - r[0:cs])
                o_state.append(r[cs:2 * cs])
            r2 = _dot(l2_ref[0, pr], jnp.concatenate(v_new, axis=0).astype(BF16))
            for e in range(2):
                hh = 2 * pr + e
                o_ref[:, hh * HEAD_DIM_B:(hh + 1) * HEAD_DIM_B] = o_state[e] + r2[e * cs:(e + 1) * cs]
                s_scr[d * nh + hh] = (s_scr[d * nh + hh] * eg_ref[0, 0, hh:hh + 1, :]
                                      + r2[2 * cs + e * 2 * cs:2 * cs + (e + 1) * 2 * cs])


def _gdn_rec(wq, u, l2, egl, lc):
    nc = wq.shape[0]
    cs, nh = CHUNK_B, N_HEADS_B
    ncc = lc // cs
    fwd = lambda i: i
    bwd = lambda i: jnp.where(i < ncc, ncc - 1 - i, nc - 1 - (i - ncc))
    specs = []
    for d, order in ((0, fwd), (1, bwd)):
        specs += [
            pl.BlockSpec((1, nh, 2 * cs, HEAD_DIM_B), lambda i, d=d, order=order: (order(i), d, 0, 0)),
            pl.BlockSpec((1, nh, cs, HEAD_DIM_B), lambda i, d=d, order=order: (order(i), d, 0, 0)),
            pl.BlockSpec((1, 2, 6 * cs, LANES), lambda i, d=d, order=order: (order(i), d, 0, 0)),
            pl.BlockSpec((1, 1, nh, HEAD_DIM_B), lambda i, d=d, order=order: (order(i), d, 0, 0)),
        ]
    return pl.pallas_call(
        _gdn_rec_kernel,
        grid=(nc,),
        in_specs=specs,
        out_specs=[pl.BlockSpec((cs, D_B), lambda i: (fwd(i), 0)),
                   pl.BlockSpec((cs, D_B), lambda i: (bwd(i), 0))],
        out_shape=[jax.ShapeDtypeStruct((nc * cs, D_B), F32)] * 2,
        scratch_shapes=[pltpu.VMEM((2 * nh, HEAD_DIM_B, HEAD_DIM_B), F32)],
        compiler_params=_cparams("arbitrary"),
        name="gdn_rec",
    )(wq, u, l2, egl, wq, u, l2, egl)


def _outproj_kernel(yaf_ref, yab_ref, obf_ref, obb_ref, oc_ref, ga_ref, gb_ref, gc_ref, h_ref, mod_ref,
                    on_ref, w_ref, out_ref, *, lc, tm):
    mix_a = (yaf_ref[...] + yab_ref[...]) * _silu(ga_ref[...])
    ob = obf_ref[...] + obb_ref[...]
    gb = _silu(gb_ref[...])
    mix_b = []
    for hd in range(N_HEADS_B):
        sl = slice(hd * HEAD_DIM_B, (hd + 1) * HEAD_DIM_B)
        x = ob[:, sl]
        mix_b.append(x * lax.rsqrt(jnp.mean(x * x, axis=-1, keepdims=True) + EPS) * on_ref[...] * gb[:, sl])
    mix_c = oc_ref[...] * _silu(gc_ref[...])
    mix = jnp.concatenate([mix_a] + mix_b + [mix_c], axis=1).astype(BF16)
    row = pl.program_id(0) * tm + lax.broadcasted_iota(jnp.int32, (tm, 1), 0)
    gate = jnp.where(row < lc, mod_ref[1:2, 2 * D_MODEL:3 * D_MODEL], mod_ref[0:1, 2 * D_MODEL:3 * D_MODEL])
    out_ref[...] = h_ref[...] + gate * _dot(mix, w_ref[...])


def _outproj(ya_f, ya_b, ob_f, ob_b, oc, z, h, mod_l, onorm_l, w_l, lc):
    n = h.shape[0]
    tm = 256
    rows = lambda width, col: pl.BlockSpec((tm, width), lambda i: (i, col // width))
    const2 = lambda i: (0, 0)
    return pl.pallas_call(
        functools.partial(_outproj_kernel, lc=lc, tm=tm),
        grid=(n // tm,),
        in_specs=[
            rows(D_A, 0), rows(D_A, 0), rows(D_B, 0), rows(D_B, 0), rows(D_Q_C, 0),
            rows(D_A, COL_GA), rows(D_B, COL_GB), rows(D_Q_C, COL_GC), rows(D_MODEL, 0),
            pl.BlockSpec((SUBLANES, 3 * D_MODEL), const2), pl.BlockSpec((1, HEAD_DIM_B), const2),
            pl.BlockSpec((D_MIX, D_MODEL), const2),
        ],
        out_specs=rows(D_MODEL, 0),
        out_shape=jax.ShapeDtypeStruct((n, D_MODEL), F32),
        compiler_params=_cparams("parallel"),
        name="outproj",
    )(ya_f, ya_b, ob_f, ob_b, oc, z, z, z, h, mod_l, onorm_l, w_l)


def _final_norm_kernel(h_ref, g_ref, o_ref):
    x = h_ref[...]
    o_ref[...] = x * lax.rsqrt(jnp.mean(x * x, axis=-1, keepdims=True) + EPS) * g_ref[...]


def _final_norm(h, final_g, lc):
    n = h.shape[0]
    tm = 256
    off = lc // tm
    return pl.pallas_call(
        _final_norm_kernel,
        grid=((n - lc) // tm,),
        in_specs=[pl.BlockSpec((tm, D_MODEL), lambda i: (i + off, 0)),
                  pl.BlockSpec((1, D_MODEL), lambda i: (0, 0))],
        out_specs=pl.BlockSpec((tm, D_MODEL), lambda i: (i, 0)),
        out_shape=jax.ShapeDtypeStruct((n - lc, D_MODEL), F32),
        compiler_params=_cparams("parallel"),
        name="final_norm",
    )(h, final_g)


def _rope_tables(t_len, lc):
    rows = t_len // GRID_W
    row = jnp.repeat(jnp.arange(rows, dtype=jnp.int32), GRID_W)
    col = jnp.tile(jnp.arange(GRID_W, dtype=jnp.int32), rows)
    n_freq = HEAD_DIM_C // 4
    inv_freq = ROPE_BASE ** (-jnp.arange(n_freq, dtype=F32) / n_freq)
    ang = jnp.concatenate([row.astype(F32)[:, None] * inv_freq, col.astype(F32)[:, None] * inv_freq], axis=-1)
    cos, sin = jnp.cos(ang), jnp.sin(ang)
    half = HEAD_DIM_C // 2
    cos2 = jnp.concatenate([jnp.ones((lc, 2 * half), F32), jnp.concatenate([cos, cos], axis=-1)], axis=0)
    sin2 = jnp.concatenate([jnp.zeros((lc, 2 * half), F32), jnp.concatenate([-sin, sin], axis=-1)], axis=0)
    return cos2, sin2


def _relayout_w_in(w_in):
    sizes = (D_A, D_A, 3 * D_B, D_B, 2 * N_HEADS_B, 2 * N_HEADS_B, D_Q_C, D_KV_C, D_KV_C, D_Q_C)
    idx = np.cumsum(sizes)[:-1].tolist()
    xa, ga, qkvb, gb, beta, alpha, qc, kc, vc, gc = jnp.split(w_in, idx, axis=-1)
    pad = jnp.zeros(w_in.shape[:-1] + (LANES - 4 * N_HEADS_B,), w_in.dtype)
    w = jnp.concatenate([qc, gc, kc, vc, xa, qkvb, ga, gb, beta, alpha, pad], axis=-1)
    assert w.shape[-1] == D_Z
    return w.astype(BF16)


def _block_diag(w):
    eye = jnp.eye(N_BLK_A, dtype=w.dtype)
    out = jnp.einsum('...nij,nm->...nimj', w, eye)
    return out.reshape(w.shape[:-3] + (D_A, D_A))


def kernel(x, c, ctx, c_ctx, norm_g, w_mod, b_mod, w_in, conv_a_w, conv_a_b, w_ra, b_ra, w_ia, b_ia, lam_a,
           conv_b_w, a_log_b, dt_bias_b, onorm_b, qn_c, kn_c, w_out, final_g):
    bsz, t_len, d = x.shape
    lc = ctx.shape[1]
    depth = w_in.shape[0]
    assert bsz == 1 and d == D_MODEL and t_len % GRID_W == 0

    h = jnp.concatenate([ctx[0], x[0]], axis=0)
    cc = jnp.concatenate([c, c_ctx[None, :], jnp.zeros((SUBLANES - 2, d), F32)], axis=0)
    mod = _modulation(cc, w_mod, b_mod)

    w_in_r = _relayout_w_in(w_in)
    w_out_r = w_out.astype(BF16)
    wg = jnp.concatenate([_block_diag(w_ra), _block_diag(w_ia)], axis=-1).astype(BF16)
    bgate = jnp.concatenate([b_ra, b_ia], axis=-1)[:, :, None, :]
    nh2 = 2 * N_HEADS_B
    lane_pad = lambda v: jnp.pad(v.reshape(depth, 1, nh2), ((0, 0), (0, 0), (nh2, LANES - 2 * nh2)))
    avec = lane_pad(a_log_b)
    dtvec = lane_pad(dt_bias_b)
    cos2, sin2 = _rope_tables(t_len, lc)
    expanders = _gdn_expanders()

    for l in range(depth):
        z = _inproj(h, mod[l], norm_g[l][None, :], w_in_r[l], lc)
        qh, kh, vh = _attn_prep(z, cos2, sin2, qn_c[l][None, :], kn_c[l][None, :])
        oc = _attention(qh, kh, vh, lc)
        ya_f, ya_b = _rglru(z, conv_a_w[l], conv_a_b[l][None, :], wg[l], bgate[l], lam_a[l], lc)
        qn, kn, v, bg = _gdn_prep(z, conv_b_w[l], avec[l], dtvec[l], lc)
        wq, u, l2, egl = _gdn_chunk(qn, kn, v, bg, expanders)
        ob_f, ob_b = _gdn_rec(wq, u, l2, egl, lc)
        h = _outproj(ya_f, ya_b, ob_f, ob_b, oc, z, h, mod[l], onorm_b[l][None, :], w_out_r[l], lc)

    return _final_norm(h, final_g[None, :], lc)[None]
```

```python
import functools
import math

import numpy as np
import jax
import jax.numpy as jnp
from jax import lax
from jax.experimental import pallas as pl
from jax.experimental.pallas import tpu as pltpu

F32 = jnp.float32
BF16 = jnp.bfloat16

D_MODEL = 2048
GRID_W = 64
EPS = 1e-6
ROPE_BASE = 10000.0

D_A = 512
N_BLK_A = 8
BLK_A = D_A // N_BLK_A
LRU_C = 8.0

N_HEADS_B = 4
HEAD_DIM_B = 128
D_B = N_HEADS_B * HEAD_DIM_B
CHUNK_B = 64

N_Q_HEADS_C = 8
N_KV_HEADS_C = 2
GROUP_C = N_Q_HEADS_C // N_KV_HEADS_C
HEAD_DIM_C = 128
D_Q_C = N_Q_HEADS_C * HEAD_DIM_C
D_KV_C = N_KV_HEADS_C * HEAD_DIM_C

D_MIX = D_A + D_B + D_Q_C

COL_QC = 0
COL_GC = 1024
COL_KV = 2048
COL_XA = 2560
COL_QKVB = 3072
COL_GA = 4608
COL_GB = 5120
COL_BA = 5632
D_Z = 5760
LANES = 128
SUBLANES = 8
LOG2_E = math.log2(math.e)
ATTN_TILE = 256
NEG_BIG = -1e30
GDN_CHUNKS_PER_STEP = 2

VMEM_LIMIT = 56 * 1024 * 1024


def _cparams(*sem):
    return pltpu.CompilerParams(dimension_semantics=tuple(sem), vmem_limit_bytes=VMEM_LIMIT)


def _dot(a, b):
    return jnp.dot(a, b, preferred_element_type=F32)


def _dot_nt(a, b):
    return lax.dot_general(a, b, (((1,), (1,)), ((), ())), preferred_element_type=F32)


def _split2(a):
    hi = a.astype(BF16)
    lo = (a - hi.astype(F32)).astype(BF16)
    return hi, lo


def _split3(a):
    hi = a.astype(BF16)
    r = a - hi.astype(F32)
    mid = r.astype(BF16)
    lo = (r - mid.astype(F32)).astype(BF16)
    return hi, mid, lo


def _dot3(a, b):
    ah, al = _split2(a)
    bh, bl = _split2(b)
    return _dot(ah, bh) + (_dot(al, bh) + _dot(ah, bl))


def _silu(x):
    return x * jax.nn.sigmoid(x)


def _softplus(x):
    return jnp.maximum(x, 0.0) + jnp.log1p(jnp.exp(-jnp.abs(x)))


def _pick(n, cands):
    for c in cands:
        if n % c == 0:
            return c
    raise ValueError(f"no tile for {n} in {cands}")


def _mod_kernel(c_ref, w_ref, b_ref, o_ref):
    o_ref[0] = _dot3(_silu(c_ref[...]), w_ref[0]) + b_ref[0]


def _modulation(cc, w_mod, b_mod):
    depth, d, d3 = w_mod.shape
    tn = 768
    return pl.pallas_call(
        _mod_kernel,
        grid=(depth, d3 // tn),
        in_specs=[
            pl.BlockSpec((SUBLANES, d), lambda l, j: (0, 0)),
            pl.BlockSpec((1, d, tn), lambda l, j: (l, 0, j)),
            pl.BlockSpec((1, 1, tn), lambda l, j: (l, 0, j)),
        ],
        out_specs=pl.BlockSpec((1, SUBLANES, tn), lambda l, j: (l, 0, j)),
        out_shape=jax.ShapeDtypeStruct((depth, SUBLANES, d3), F32),
        compiler_params=_cparams("parallel", "parallel"),
        name="modulation",
    )(cc, w_mod, b_mod.reshape(depth, 1, d3))


def _inproj_kernel(h_ref, mod_ref, g_ref, w_ref, z_ref, y_scr, *, lc, tm):
    m = pl.program_id(0)

    @pl.when(pl.program_id(1) == 0)
    def _():
        x = h_ref[...]
        xn = x * lax.rsqrt(jnp.mean(x * x, axis=-1, keepdims=True) + EPS) * g_ref[...]
        row = m * tm + lax.broadcasted_iota(jnp.int32, (tm, 1), 0)
        is_ctx = row < lc
        shift = jnp.where(is_ctx, mod_ref[1:2, 0:D_MODEL], mod_ref[0:1, 0:D_MODEL])
        scale = jnp.where(is_ctx, mod_ref[1:2, D_MODEL:2 * D_MODEL], mod_ref[0:1, D_MODEL:2 * D_MODEL])
        y_scr[...] = (xn * (1.0 + scale) + shift).astype(BF16)

    z_ref[...] = _dot(y_scr[...], w_ref[...])


def _inproj(h, mod_l, norm_g_l, w_l, lc):
    n = h.shape[0]
    tm = _pick(n, (768, 512, 384, 256))
    tn = 1152
    return pl.pallas_call(
        functools.partial(_inproj_kernel, lc=lc, tm=tm),
        grid=(n // tm, D_Z // tn),
        in_specs=[
            pl.BlockSpec((tm, D_MODEL), lambda i, j: (i, 0)),
            pl.BlockSpec((SUBLANES, 3 * D_MODEL), lambda i, j: (0, 0)),
            pl.BlockSpec((1, D_MODEL), lambda i, j: (0, 0)),
            pl.BlockSpec((D_MODEL, tn), lambda i, j: (0, j)),
        ],
        out_specs=pl.BlockSpec((tm, tn), lambda i, j: (i, j)),
        out_shape=jax.ShapeDtypeStruct((n, D_Z), F32),
        scratch_shapes=[pltpu.VMEM((tm, D_MODEL), BF16)],
        compiler_params=_cparams("parallel", "arbitrary"),
        name="inproj",
    )(h, mod_l, norm_g_l, w_l)


def _attn_prep_kernel(q_ref, kv_ref, cos_ref, sin_ref, qn_ref, kn_ref, qo_ref, ko_ref, vo_ref):
    cos = cos_ref[...]
    sin = sin_ref[...]

    def prep(x, gain, scale):
        y = x * lax.rsqrt(jnp.mean(x * x, axis=-1, keepdims=True) + EPS) * gain
        rot = pltpu.roll(y, HEAD_DIM_C // 2, 1)
        return (y * cos + rot * sin) * scale

    for hd in range(N_Q_HEADS_C):
        sl = slice(hd * HEAD_DIM_C, (hd + 1) * HEAD_DIM_C)
        qo_ref[:, sl] = prep(q_ref[:, sl], qn_ref[...], HEAD_DIM_C ** -0.5 * LOG2_E).astype(BF16)
    for hd in range(N_KV_HEADS_C):
        sl = slice(hd * HEAD_DIM_C, (hd + 1) * HEAD_DIM_C)
        ko_ref[:, sl] = prep(kv_ref[:, sl], kn_ref[...], 1.0).astype(BF16)
    vo_ref[0] = kv_ref[:, D_KV_C:2 * D_KV_C].T.astype(BF16)


def _attn_prep(z, cos2, sin2, qn_l, kn_l):
    n = z.shape[0]
    tm = ATTN_TILE
    return pl.pallas_call(
        _attn_prep_kernel,
        grid=(n // tm,),
        in_specs=[
            pl.BlockSpec((tm, D_Q_C), lambda i: (i, COL_QC // D_Q_C)),
            pl.BlockSpec((tm, 2 * D_KV_C), lambda i: (i, COL_KV // (2 * D_KV_C))),
            pl.BlockSpec((tm, HEAD_DIM_C), lambda i: (i, 0)),
            pl.BlockSpec((tm, HEAD_DIM_C), lambda i: (i, 0)),
            pl.BlockSpec((1, HEAD_DIM_C), lambda i: (0, 0)),
            pl.BlockSpec((1, HEAD_DIM_C), lambda i: (0, 0)),
        ],
        out_specs=[
            pl.BlockSpec((tm, D_Q_C), lambda i: (i, 0)),
            pl.BlockSpec((tm, D_KV_C), lambda i: (i, 0)),
            pl.BlockSpec((1, D_KV_C, tm), lambda i: (i, 0, 0)),
        ],
        out_shape=[
            jax.ShapeDtypeStruct((n, D_Q_C), BF16),
            jax.ShapeDtypeStruct((n, D_KV_C), BF16),
            jax.ShapeDtypeStruct((n // tm, D_KV_C, tm), BF16),
        ],
        compiler_params=_cparams("parallel"),
        name="attn_prep",
    )(z, z, cos2, sin2, qn_l, kn_l)


def _attn_kernel(q_ref, k_ref, vt_ref, o_ref, sa_scr, sb_scr, acc_scr, *, lc, tq, tk, n_blocks_all):
    qt = pl.program_id(1)
    acc_scr[...] = jnp.zeros_like(acc_scr)

    def scores(j, s_scr):
        kb = k_ref[pl.ds(pl.multiple_of(j * tk, tk), tk), :]
        for g in range(GROUP_C):
            s_scr[g] = _dot_nt(kb, q_ref[:, g * HEAD_DIM_C:(g + 1) * HEAD_DIM_C])

    def softmax_pv(j, s_scr, carry):
        ms, ls = carry
        vt = vt_ref[j]
        new_m, new_l = [], []
        for g in range(GROUP_C):
            st = s_scr[g]
            m_new = jnp.maximum(ms[g], jnp.max(st, axis=0, keepdims=True))
            alpha = jnp.exp2(ms[g] - m_new)
            p = jnp.exp2(st - m_new)
            new_l.append(alpha * ls[g] + jnp.sum(p, axis=0, keepdims=True))
            acc_scr[g] = alpha * acc_scr[g] + _dot(vt, p.astype(BF16))
            new_m.append(m_new)
        return tuple(new_m), tuple(new_l)

    def body(i, carry):
        j = 2 * i
        scores(j + 1, sb_scr)
        carry = softmax_pv(j, sa_scr, carry)
        scores(j + 2, sa_scr)
        return softmax_pv(j + 1, sb_scr, carry)

    n_blocks = jnp.where(qt * tq < lc, lc // tk, n_blocks_all)
    init = (tuple(jnp.full((1, tq), NEG_BIG, F32) for _ in range(GROUP_C)),
            tuple(jnp.zeros((1, tq), F32) for _ in range(GROUP_C)))
    scores(0, sa_scr)
    carry = lax.fori_loop(0, (n_blocks - 1) // 2, body, init)
    _, ls = softmax_pv(n_blocks - 1, sa_scr, carry)

    for g in range(GROUP_C):
        o_ref[:, g * HEAD_DIM_C:(g + 1) * HEAD_DIM_C] = (acc_scr[g] / ls[g]).T


def _attention(qh, kh, vt, lc):
    n = qh.shape[0]
    tq = tk = ATTN_TILE
    assert lc % tq == 0 and n % tq == 0
    assert (lc // tk) % 2 == 1 and (n // tk) % 2 == 1
    gw = GROUP_C * HEAD_DIM_C
    return pl.pallas_call(
        functools.partial(_attn_kernel, lc=lc, tq=tq, tk=tk, n_blocks_all=n // tk),
        grid=(N_KV_HEADS_C, n // tq),
        in_specs=[
            pl.BlockSpec((tq, gw), lambda j, i: (i, j)),
            pl.BlockSpec((n, HEAD_DIM_C), lambda j, i: (0, j)),
            pl.BlockSpec((n // tk, HEAD_DIM_C, tk), lambda j, i: (0, j, 0)),
        ],
        out_specs=pl.BlockSpec((tq, gw), lambda j, i: (i, j)),
        out_shape=jax.ShapeDtypeStruct((n, D_Q_C), F32),
        scratch_shapes=[pltpu.VMEM((GROUP_C, tk, tq), F32), pltpu.VMEM((GROUP_C, tk, tq), F32),
                        pltpu.VMEM((GROUP_C, HEAD_DIM_C, tq), F32)],
        compiler_params=_cparams("parallel", "parallel"),
        name="attention",
    )(qh, kh, vt)


def _conv4(x, prev8, next8, w_ref, prev_ok, next_ok):
    tm = x.shape[0]
    row = lax.broadcasted_iota(jnp.int32, (tm, 1), 0)
    p = jnp.where(prev_ok, prev8, 0.0)
    nx = jnp.where(next_ok, next8, 0.0)
    xm1 = jnp.where(row == 0, p[7:8], pltpu.roll(x, 1, 0))
    xm2 = jnp.where(row == 0, p[6:7], jnp.where(row == 1, p[7:8], pltpu.roll(x, 2, 0)))
    xp1 = jnp.where(row == tm - 1, nx[0:1], pltpu.roll(x, tm - 1, 0))
    return w_ref[0:1] * xm2 + w_ref[1:2] * xm1 + w_ref[2:3] * x + w_ref[3:4] * xp1


def _seg_flags(t, nbc, nt):
    prev_ok = jnp.logical_and(t != 0, t != nbc)
    next_ok = jnp.logical_and(t != nbc - 1, t != nt - 1)
    return prev_ok, next_ok


def _halo_maps(tile_of, tm, n, col_block):
    r = tm // SUBLANES
    last = n // SUBLANES - 1
    main = lambda i: (tile_of(i), col_block)
    prev = lambda i: (jnp.maximum(tile_of(i) * r - 1, 0), col_block)
    nxt = lambda i: (jnp.minimum((tile_of(i) + 1) * r, last), col_block)
    return main, prev, nxt


def _rglru_kernel(xf_ref, pf_ref, nf_ref, xb_ref, pb_ref, nb_ref, cw_ref, cb_ref, wg_ref, bg_ref,
                  lam_ref, of_ref, ob_ref, a_scr, b_scr, hcar, *, tm, nbc, nt):
    i = pl.program_id(0)
    seg = tm // SUBLANES

    @pl.when(i == 0)
    def _():
        hcar[...] = jnp.zeros_like(hcar)

    tiles = (i, jnp.where(i < nbc, nbc - 1 - i, nt - 1 - (i - nbc)))
    refs = ((xf_ref, pf_ref, nf_ref, of_ref), (xb_ref, pb_ref, nb_ref, ob_ref))
    for d in range(2):
        x_ref, p_ref, n_ref, o_ref = refs[d]
        prev_ok, next_ok = _seg_flags(tiles[d], nbc, nt)
        u = _conv4(x_ref[...], p_ref[...], n_ref[...], cw_ref, prev_ok, next_ok) + cb_ref[...]
        gates = _dot(u.astype(BF16), wg_ref[d]) + bg_ref[d]
        gate_r = jax.nn.sigmoid(gates[:, 0:D_A])
        gate_i = jax.nn.sigmoid(gates[:, D_A:2 * D_A])
        log_a = (-LRU_C) * gate_r * _softplus(-lam_ref[d:d + 1, :])
        a = jnp.exp(log_a)
        b = jnp.sqrt(-jnp.tanh(log_a) * (a * a + 1.0)) * (gate_i * u)

        for cg in range(D_A // LANES):
            lanes = slice(cg * LANES, (cg + 1) * LANES)
            a_scr[cg] = a[:, lanes]
            b_scr[cg] = b[:, lanes]
            hloc = jnp.zeros((SUBLANES, LANES), F32)
            pcum = jnp.ones((SUBLANES, LANES), F32)
            for j in (range(seg) if d == 0 else range(seg - 1, -1, -1)):
                rows = pl.ds(j, SUBLANES, stride=seg)
                aj = a_scr[cg, rows, :]
                hloc = aj * hloc + b_scr[cg, rows, :]
                pcum = aj * pcum
                b_scr[cg, rows, :] = hloc
                a_scr[cg, rows, :] = pcum
            carry = hcar[d:d + 1, lanes]
            seg_in = [None] * SUBLANES
            for s in (range(SUBLANES) if d == 0 else range(SUBLANES - 1, -1, -1)):
                seg_in[s] = carry
                carry = pcum[s:s + 1] * carry + hloc[s:s + 1]
            hcar[d:d + 1, lanes] = carry
            cin = jnp.concatenate(seg_in, axis=0)
            for j in range(seg):
                rows = pl.ds(j, SUBLANES, stride=seg)
                b_scr[cg, rows, :] = b_scr[cg, rows, :] + a_scr[cg, rows, :] * cin
            o_ref[:, lanes] = b_scr[cg]


def _rglru(z, cw, cb, wg, bg, lam, lc):
    n = z.shape[0]
    tm = 256
    assert lc % tm == 0 and n % tm == 0
    nbc, nt = lc // tm, n // tm
    fwd = lambda i: i
    bwd = lambda i: jnp.where(i < nbc, nbc - 1 - i, nt - 1 - (i - nbc))
    cblk = COL_XA // D_A
    fm, fp, fn = _halo_maps(fwd, tm, n, cblk)
    bm, bp, bn = _halo_maps(bwd, tm, n, cblk)
    const2 = lambda i: (0, 0)
    const3 = lambda i: (0, 0, 0)
    return pl.pallas_call(
        functools.partial(_rglru_kernel, tm=tm, nbc=nbc, nt=nt),
        grid=(nt,),
        in_specs=[
            pl.BlockSpec((tm, D_A), fm), pl.BlockSpec((SUBLANES, D_A), fp), pl.BlockSpec((SUBLANES, D_A), fn),
            pl.BlockSpec((tm, D_A), bm), pl.BlockSpec((SUBLANES, D_A), bp), pl.BlockSpec((SUBLANES, D_A), bn),
            pl.BlockSpec((4, D_A), const2), pl.BlockSpec((1, D_A), const2),
            pl.BlockSpec((2, D_A, 2 * D_A), const3), pl.BlockSpec((2, 1, 2 * D_A), const3),
            pl.BlockSpec((2, D_A), const2),
        ],
        out_specs=[pl.BlockSpec((tm, D_A), lambda i: (fwd(i), 0)),
                   pl.BlockSpec((tm, D_A), lambda i: (bwd(i), 0))],
        out_shape=[jax.ShapeDtypeStruct((n, D_A), F32)] * 2,
        scratch_shapes=[pltpu.VMEM((D_A // LANES, tm, LANES), F32), pltpu.VMEM((D_A // LANES, tm, LANES), F32),
                        pltpu.VMEM((SUBLANES, D_A), F32)],
        compiler_params=_cparams("arbitrary"),
        name="rglru",
    )(z, z, z, z, z, z, cw, cb, wg, bg, lam)


def _gdn_prep_kernel(x_ref, p_ref, n_ref, ba_ref, cw_ref, av_ref, dt_ref, q_ref, k_ref, v_ref, bg_ref,
                     *, nbc, nt):
    prev_ok, next_ok = _seg_flags(pl.program_id(0), nbc, nt)
    u = _silu(_conv4(x_ref[...], p_ref[...], n_ref[...], cw_ref, prev_ok, next_ok))
    for hd in range(N_HEADS_B):
        sl = slice(hd * HEAD_DIM_B, (hd + 1) * HEAD_DIM_B)
        qv = u[:, hd * HEAD_DIM_B:(hd + 1) * HEAD_DIM_B]
        kv = u[:, D_B + hd * HEAD_DIM_B:D_B + (hd + 1) * HEAD_DIM_B]
        q_ref[:, sl] = qv * lax.rsqrt(jnp.sum(qv * qv, axis=-1, keepdims=True) + EPS)
        k_ref[:, sl] = kv * lax.rsqrt(jnp.sum(kv * kv, axis=-1, keepdims=True) + EPS)
    v_ref[...] = u[:, 2 * D_B:3 * D_B]
    ba = ba_ref[...]
    lane = lax.broadcasted_iota(jnp.int32, ba.shape, 1)
    beta = jax.nn.sigmoid(ba)
    g = -jnp.exp(av_ref[...]) * _softplus(ba + dt_ref[...])
    bg_ref[...] = jnp.where(lane < 2 * N_HEADS_B, beta, jnp.where(lane < 4 * N_HEADS_B, g, 0.0))


def _gdn_prep(z, cw, avec, dtvec, lc):
    n = z.shape[0]
    tm = 256
    nbc, nt = lc // tm, n // tm
    w3 = 3 * D_B
    assert COL_QKVB % w3 == 0
    m, p, nx = _halo_maps(lambda i: i, tm, n, COL_QKVB // w3)
    const2 = lambda i: (0, 0)
    return pl.pallas_call(
        functools.partial(_gdn_prep_kernel, nbc=nbc, nt=nt),
        grid=(nt,),
        in_specs=[
            pl.BlockSpec((tm, w3), m), pl.BlockSpec((SUBLANES, w3), p), pl.BlockSpec((SUBLANES, w3), nx),
            pl.BlockSpec((tm, LANES), lambda i: (i, COL_BA // LANES)),
            pl.BlockSpec((4, w3), const2), pl.BlockSpec((1, LANES), const2), pl.BlockSpec((1, LANES), const2),
        ],
        out_specs=[pl.BlockSpec((tm, D_B), lambda i: (i, 0))] * 3 + [pl.BlockSpec((tm, LANES), lambda i: (i, 0))],
        out_shape=[jax.ShapeDtypeStruct((n, D_B), F32)] * 3 + [jax.ShapeDtypeStruct((n, LANES), F32)],
        compiler_params=_cparams("parallel"),
        name="gdn_prep",
    )(z, z, z, z, cw, avec, dtvec)


def _unit_tri_inverse(mats, row, col, lane_lo):
    def block_diag(y):
        return jnp.concatenate([jnp.where(lane_lo, y, 0.0), jnp.where(lane_lo, 0.0, y)], axis=0).astype(BF16)

    def mm(xs, ys):
        return [_dot(x.astype(BF16), block_diag(y)) for x, y in zip(xs, ys)]

    def same(shift):
        return (row >> shift) == (col >> shift)

    eye = jnp.where(row == col, 1.0, 0.0)
    m8 = [jnp.where(same(3), m, 0.0) for m in mats]
    xs = [eye - m for m in m8]
    pw = mm(m8, m8)
    xs = [x + y for x, y in zip(xs, mm(xs, pw))]
    pw = mm(pw, pw)
    xs = [x + y for x, y in zip(xs, mm(xs, pw))]
    for shift in (4, 5, 6):
        off = [jnp.where(same(shift), jnp.where(same(shift - 1), 0.0, m), 0.0) for m in mats]
        xs = [x - y for x, y in zip(xs, mm(xs, mm(off, xs)))]
    return xs


def _gdn_chunk_kernel(q_ref, k_ref, v_ref, bg_ref, e64_ref, eb_ref, eg_ref, wq_ref, u_ref, l2_ref, egl_ref):
    cs = CHUNK_B
    nh = N_HEADS_B
    ri = lax.broadcasted_iota(jnp.int32, (cs, cs), 0)
    ci = lax.broadcasted_iota(jnp.int32, (cs, cs), 1)
    ones = jnp.ones((cs, cs), BF16)
    row = lax.broadcasted_iota(jnp.int32, (cs, nh * cs), 0)
    col = lax.broadcasted_iota(jnp.int32, (cs, nh * cs), 1) & (cs - 1)
    lane_lo = lax.broadcasted_iota(jnp.int32, (cs, LANES), 1) < cs
    lane_lo2 = lax.broadcasted_iota(jnp.int32, (2 * cs, LANES), 1) < cs
    zeros = jnp.zeros((cs, HEAD_DIM_B), F32)
    tri = (jnp.where(ci <= ri, 1.0, 0.0).astype(BF16), jnp.where(ci >= ri, 1.0, 0.0).astype(BF16))
    strict = (row > col, row < col)
    causal = (row >= col, row <= col)

    def dot_parts(lhs, parts):
        return _dot(lhs, parts[0]) + (_dot(lhs, parts[1]) + _dot(lhs, parts[2]))

    stage = {}
    mats = []
    for c in range(GDN_CHUNKS_PER_STEP):
        rows = slice(c * cs, (c + 1) * cs)
        qs = q_ref[rows, :] * (HEAD_DIM_B ** -0.5)
        kn = k_ref[rows, :]
        v = v_ref[rows, :]
        bg_parts = _split3(bg_ref[rows, :])
        for d in range(2):
            expand = lambda e: [_dot(part, e).astype(BF16) for part in bg_parts]
            g64 = [jnp.where(strict[d], part, 0.0).astype(BF16) for part in expand(e64_ref[d])]
            decay = jnp.where(causal[d], jnp.exp(dot_parts(tri[d], g64)), 0.0)
            beta_parts = expand(eb_ref[d])
            beta = beta_parts[0].astype(F32) + (beta_parts[1].astype(F32) + beta_parts[2].astype(F32))
            g_parts = expand(eg_ref[d])
            gc = dot_parts(tri[d], g_parts)
            gl = dot_parts(ones, g_parts)
            egc = jnp.exp(gc)
            kb = kn * beta
            lhs = jnp.concatenate([kb, qs], axis=0).astype(BF16)
            prods = []
            for pr in range(2):
                ka = kn[:, (2 * pr) * HEAD_DIM_B:(2 * pr + 1) * HEAD_DIM_B]
                kb_ = kn[:, (2 * pr + 1) * HEAD_DIM_B:(2 * pr + 2) * HEAD_DIM_B]
                rhs_t = jnp.concatenate([jnp.concatenate([ka, zeros], axis=1),
                                         jnp.concatenate([zeros, kb_], axis=1)], axis=0).astype(BF16)
                prods.append(_dot_nt(lhs[:, 2 * pr * HEAD_DIM_B:(2 * pr + 2) * HEAD_DIM_B], rhs_t))
            prod = jnp.concatenate(prods, axis=1)
            mat = jnp.where(strict[d], prod[0:cs] * decay, 0.0)
            mats += [mat[:, 0:LANES], mat[:, LANES:2 * LANES]]
            stage[c, d] = dict(qk=prod[cs:2 * cs] * decay, vb=v * beta, kbe=kb * egc, qhead=qs * egc,
                               ktail=kn * jnp.exp(gl - gc), egl=jnp.exp(gl))

    row2 = lax.broadcasted_iota(jnp.int32, (cs, LANES), 0)
    col2 = lax.broadcasted_iota(jnp.int32, (cs, LANES), 1) & (cs - 1)
    tmats = _unit_tri_inverse(mats, row2, col2, lane_lo)

    for c in range(GDN_CHUNKS_PER_STEP):
        for d in range(2):
            st = stage[c, d]
            for pr in range(2):
                ha, hb = 2 * pr, 2 * pr + 1
                sa = slice(ha * HEAD_DIM_B, (ha + 1) * HEAD_DIM_B)
                sb = slice(hb * HEAD_DIM_B, (hb + 1) * HEAD_DIM_B)
                tp = tmats[(c * 2 + d) * 2 + pr]
                rhs = jnp.concatenate([jnp.concatenate([st["vb"][:, sa], st["kbe"][:, sa]], axis=1),
                                       jnp.concatenate([st["vb"][:, sb], st["kbe"][:, sb]], axis=1)],
                                      axis=0).astype(BF16)
                ra = _dot(jnp.where(lane_lo, tp, 0.0).astype(BF16), rhs)
                rb = _dot(jnp.where(lane_lo, 0.0, tp).astype(BF16), rhs)
                for hh, res, sl in ((ha, ra, sa), (hb, rb, sb)):
                    u_ref[c, d * nh + hh] = res[:, 0:HEAD_DIM_B]
                    wq_ref[c, d * nh + hh] = jnp.concatenate(
                        [res[:, HEAD_DIM_B:2 * HEAD_DIM_B], st["qhead"][:, sl]], axis=0).astype(BF16)
                kt_t = jnp.concatenate([st["ktail"][:, sa], st["ktail"][:, sb]], axis=0).T
                qkp = st["qk"][:, pr * LANES:(pr + 1) * LANES]
                l2_ref[c, d * 2 + pr] = jnp.concatenate(
                    [jnp.where(lane_lo, qkp, 0.0), jnp.where(lane_lo, 0.0, qkp),
                     jnp.where(lane_lo2, kt_t, 0.0), jnp.where(lane_lo2, 0.0, kt_t)], axis=0).astype(BF16)
            egl_ref[c, d] = jnp.concatenate(
                [st["egl"][0:1, hh * HEAD_DIM_B:(hh + 1) * HEAD_DIM_B] for hh in range(nh)]
                + [jnp.zeros((SUBLANES - nh, HEAD_DIM_B), F32)], axis=0)


def _gdn_expanders():
    nh, cs = N_HEADS_B, CHUNK_B
    e64 = np.zeros((2, LANES, nh * cs), np.float32)
    eb = np.zeros((2, LANES, nh * HEAD_DIM_B), np.float32)
    eg = np.zeros((2, LANES, nh * HEAD_DIM_B), np.float32)
    for d in range(2):
        for hh in range(nh):
            p = d * nh + hh
            e64[d, 2 * nh + p, hh * cs:(hh + 1) * cs] = 1.0
            eb[d, p, hh * HEAD_DIM_B:(hh + 1) * HEAD_DIM_B] = 1.0
            eg[d, 2 * nh + p, hh * HEAD_DIM_B:(hh + 1) * HEAD_DIM_B] = 1.0
    return jnp.asarray(e64, BF16), jnp.asarray(eb, BF16), jnp.asarray(eg, BF16)


def _gdn_chunk(qn, kn, v, bg, expanders):
    n = qn.shape[0]
    cs, nh = CHUNK_B, N_HEADS_B
    nc = n // cs
    g = GDN_CHUNKS_PER_STEP
    assert nc % g == 0
    e64, eb, eg = expanders
    const3 = lambda c: (0, 0, 0)
    return pl.pallas_call(
        _gdn_chunk_kernel,
        grid=(nc // g,),
        in_specs=[
            pl.BlockSpec((g * cs, D_B), lambda c: (c, 0)), pl.BlockSpec((g * cs, D_B), lambda c: (c, 0)),
            pl.BlockSpec((g * cs, D_B), lambda c: (c, 0)), pl.BlockSpec((g * cs, LANES), lambda c: (c, 0)),
            pl.BlockSpec(e64.shape, const3), pl.BlockSpec(eb.shape, const3), pl.BlockSpec(eg.shape, const3),
        ],
        out_specs=[
            pl.BlockSpec((g, 2 * nh, 2 * cs, HEAD_DIM_B), lambda c: (c, 0, 0, 0)),
            pl.BlockSpec((g, 2 * nh, cs, HEAD_DIM_B), lambda c: (c, 0, 0, 0)),
            pl.BlockSpec((g, 4, 6 * cs, LANES), lambda c: (c, 0, 0, 0)),
            pl.BlockSpec((g, 2, SUBLANES, HEAD_DIM_B), lambda c: (c, 0, 0, 0)),
        ],
        out_shape=[
            jax.ShapeDtypeStruct((nc, 2 * nh, 2 * cs, HEAD_DIM_B), BF16),
            jax.ShapeDtypeStruct((nc, 2 * nh, cs, HEAD_DIM_B), F32),
            jax.ShapeDtypeStruct((nc, 4, 6 * cs, LANES), BF16),
            jax.ShapeDtypeStruct((nc, 2, SUBLANES, HEAD_DIM_B), F32),
        ],
        compiler_params=_cparams("parallel"),
        name="gdn_chunk",
    )(qn, kn, v, bg, e64, eb, eg)


def _gdn_rec_kernel(wqf_ref, uf_ref, l2f_ref, egf_ref, wqb_ref, ub_ref, l2b_ref, egb_ref,
                    of_ref, ob_ref, s_scr):
    cs, nh = CHUNK_B, N_HEADS_B

    @pl.when(pl.program_id(0) == 0)
    def _():
        s_scr[...] = jnp.zeros_like(s_scr)

    dirs = ((wqf_ref, uf_ref, l2f_ref, egf_ref, of_ref), (wqb_ref, ub_ref, l2b_ref, egb_ref, ob_ref))
    for d in range(2):
        wq_ref, u_ref, l2_ref, eg_ref, o_ref = dirs[d]
        for pr in range(2):
            v_new, o_state = [], []
            for e in range(2):
                hh = 2 * pr + e
                r = _dot(wq_ref[0, hh], s_scr[d * nh + hh].astype(BF16))
                v_new.append(u_ref[0, hh] - r[0:cs])
                o_state.append(r[cs:2 * cs])
            r2 = _dot(l2_ref[0, pr], jnp.concatenate(v_new, axis=0).astype(BF16))
            for e in range(2):
                hh = 2 * pr + e
                o_ref[:, hh * HEAD_DIM_B:(hh + 1) * HEAD_DIM_B] = o_state[e] + r2[e * cs:(e + 1) * cs]
                s_scr[d * nh + hh] = (s_scr[d * nh + hh] * eg_ref[0, 0, hh:hh + 1, :]
                                      + r2[2 * cs + e * 2 * cs:2 * cs + (e + 1) * 2 * cs])


def _gdn_rec(wq, u, l2, egl, lc):
    nc = wq.shape[0]
    cs, nh = CHUNK_B, N_HEADS_B
    ncc = lc // cs
    fwd = lambda i: i
    bwd = lambda i: jnp.where(i < ncc, ncc - 1 - i, nc - 1 - (i - ncc))
    specs = []
    for d, order in ((0, fwd), (1, bwd)):
        specs += [
            pl.BlockSpec((1, nh, 2 * cs, HEAD_DIM_B), lambda i, d=d, order=order: (order(i), d, 0, 0)),
            pl.BlockSpec((1, nh, cs, HEAD_DIM_B), lambda i, d=d, order=order: (order(i), d, 0, 0)),
            pl.BlockSpec((1, 2, 6 * cs, LANES), lambda i, d=d, order=order: (order(i), d, 0, 0)),
            pl.BlockSpec((1, 1, SUBLANES, HEAD_DIM_B), lambda i, d=d, order=order: (order(i), d, 0, 0)),
        ]
    return pl.pallas_call(
        _gdn_rec_kernel,
        grid=(nc,),
        in_specs=specs,
        out_specs=[pl.BlockSpec((cs, D_B), lambda i: (fwd(i), 0)),
                   pl.BlockSpec((cs, D_B), lambda i: (bwd(i), 0))],
        out_shape=[jax.ShapeDtypeStruct((nc * cs, D_B), F32)] * 2,
        scratch_shapes=[pltpu.VMEM((2 * nh, HEAD_DIM_B, HEAD_DIM_B), F32)],
        compiler_params=_cparams("arbitrary"),
        name="gdn_rec",
    )(wq, u, l2, egl, wq, u, l2, egl)


def _outproj_kernel(yaf_ref, yab_ref, obf_ref, obb_ref, oc_ref, ga_ref, gb_ref, gc_ref, h_ref, mod_ref,
                    on_ref, w_ref, out_ref, *, lc, tm):
    mix_a = (yaf_ref[...] + yab_ref[...]) * _silu(ga_ref[...])
    ob = obf_ref[...] + obb_ref[...]
    gb = _silu(gb_ref[...])
    mix_b = []
    for hd in range(N_HEADS_B):
        sl = slice(hd * HEAD_DIM_B, (hd + 1) * HEAD_DIM_B)
        x = ob[:, sl]
        mix_b.append(x * lax.rsqrt(jnp.mean(x * x, axis=-1, keepdims=True) + EPS) * on_ref[...] * gb[:, sl])
    mix_c = oc_ref[...] * _silu(gc_ref[...])
    mix = jnp.concatenate([mix_a] + mix_b + [mix_c], axis=1).astype(BF16)
    row = pl.program_id(0) * tm + lax.broadcasted_iota(jnp.int32, (tm, 1), 0)
    gate = jnp.where(row < lc, mod_ref[1:2, 2 * D_MODEL:3 * D_MODEL], mod_ref[0:1, 2 * D_MODEL:3 * D_MODEL])
    out_ref[...] = h_ref[...] + gate * _dot(mix, w_ref[...])


def _outproj(ya_f, ya_b, ob_f, ob_b, oc, z, h, mod_l, onorm_l, w_l, lc):
    n = h.shape[0]
    tm = 256
    rows = lambda width, col: pl.BlockSpec((tm, width), lambda i: (i, col // width))
    const2 = lambda i: (0, 0)
    return pl.pallas_call(
        functools.partial(_outproj_kernel, lc=lc, tm=tm),
        grid=(n // tm,),
        in_specs=[
            rows(D_A, 0), rows(D_A, 0), rows(D_B, 0), rows(D_B, 0), rows(D_Q_C, 0),
            rows(D_A, COL_GA), rows(D_B, COL_GB), rows(D_Q_C, COL_GC), rows(D_MODEL, 0),
            pl.BlockSpec((SUBLANES, 3 * D_MODEL), const2), pl.BlockSpec((1, HEAD_DIM_B), const2),
            pl.BlockSpec((D_MIX, D_MODEL), const2),
        ],
        out_specs=rows(D_MODEL, 0),
        out_shape=jax.ShapeDtypeStruct((n, D_MODEL), F32),
        compiler_params=_cparams("parallel"),
        name="outproj",
    )(ya_f, ya_b, ob_f, ob_b, oc, z, z, z, h, mod_l, onorm_l, w_l)


def _final_norm_kernel(h_ref, g_ref, o_ref):
    x = h_ref[...]
    o_ref[...] = x * lax.rsqrt(jnp.mean(x * x, axis=-1, keepdims=True) + EPS) * g_ref[...]


def _final_norm(h, final_g, lc):
    n = h.shape[0]
    tm = 256
    off = lc // tm
    return pl.pallas_call(
        _final_norm_kernel,
        grid=((n - lc) // tm,),
        in_specs=[pl.BlockSpec((tm, D_MODEL), lambda i: (i + off, 0)),
                  pl.BlockSpec((1, D_MODEL), lambda i: (0, 0))],
        out_specs=pl.BlockSpec((tm, D_MODEL), lambda i: (i, 0)),
        out_shape=jax.ShapeDtypeStruct((n - lc, D_MODEL), F32),
        compiler_params=_cparams("parallel"),
        name="final_norm",
    )(h, final_g)


def _rope_tables(t_len, lc):
    rows = t_len // GRID_W
    row = jnp.repeat(jnp.arange(rows, dtype=jnp.int32), GRID_W)
    col = jnp.tile(jnp.arange(GRID_W, dtype=jnp.int32), rows)
    n_freq = HEAD_DIM_C // 4
    inv_freq = ROPE_BASE ** (-jnp.arange(n_freq, dtype=F32) / n_freq)
    ang = jnp.concatenate([row.astype(F32)[:, None] * inv_freq, col.astype(F32)[:, None] * inv_freq], axis=-1)
    cos, sin = jnp.cos(ang), jnp.sin(ang)
    half = HEAD_DIM_C // 2
    cos2 = jnp.concatenate([jnp.ones((lc, 2 * half), F32), jnp.concatenate([cos, cos], axis=-1)], axis=0)
    sin2 = jnp.concatenate([jnp.zeros((lc, 2 * half), F32), jnp.concatenate([-sin, sin], axis=-1)], axis=0)
    return cos2, sin2


def _relayout_w_in(w_in):
    sizes = (D_A, D_A, 3 * D_B, D_B, 2 * N_HEADS_B, 2 * N_HEADS_B, D_Q_C, D_KV_C, D_KV_C, D_Q_C)
    idx = np.cumsum(sizes)[:-1].tolist()
    xa, ga, qkvb, gb, beta, alpha, qc, kc, vc, gc = jnp.split(w_in, idx, axis=-1)
    pad = jnp.zeros(w_in.shape[:-1] + (LANES - 4 * N_HEADS_B,), w_in.dtype)
    w = jnp.concatenate([qc, gc, kc, vc, xa, qkvb, ga, gb, beta, alpha, pad], axis=-1)
    assert w.shape[-1] == D_Z
    return w.astype(BF16)


def _block_diag(w):
    eye = jnp.eye(N_BLK_A, dtype=w.dtype)
    out = jnp.einsum('...nij,nm->...nimj', w, eye)
    return out.reshape(w.shape[:-3] + (D_A, D_A))


def kernel(x, c, ctx, c_ctx, norm_g, w_mod, b_mod, w_in, conv_a_w, conv_a_b, w_ra, b_ra, w_ia, b_ia, lam_a,
           conv_b_w, a_log_b, dt_bias_b, onorm_b, qn_c, kn_c, w_out, final_g):
    bsz, t_len, d = x.shape
    lc = ctx.shape[1]
    depth = w_in.shape[0]
    assert bsz == 1 and d == D_MODEL and t_len % GRID_W == 0

    h = jnp.concatenate([ctx[0], x[0]], axis=0)
    cc = jnp.concatenate([c, c_ctx[None, :], jnp.zeros((SUBLANES - 2, d), F32)], axis=0)
    mod = _modulation(cc, w_mod, b_mod)

    w_in_r = _relayout_w_in(w_in)
    w_out_r = w_out.astype(BF16)
    wg = jnp.concatenate([_block_diag(w_ra), _block_diag(w_ia)], axis=-1).astype(BF16)
    bgate = jnp.concatenate([b_ra, b_ia], axis=-1)[:, :, None, :]
    nh2 = 2 * N_HEADS_B
    lane_pad = lambda v: jnp.pad(v.reshape(depth, 1, nh2), ((0, 0), (0, 0), (nh2, LANES - 2 * nh2)))
    avec = lane_pad(a_log_b)
    dtvec = lane_pad(dt_bias_b)
    cos2, sin2 = _rope_tables(t_len, lc)
    expanders = _gdn_expanders()

    for l in range(depth):
        z = _inproj(h, mod[l], norm_g[l][None, :], w_in_r[l], lc)
        qh, kh, vh = _attn_prep(z, cos2, sin2, qn_c[l][None, :], kn_c[l][None, :])
        oc = _attention(qh, kh, vh, lc)
        ya_f, ya_b = _rglru(z, conv_a_w[l], conv_a_b[l][None, :], wg[l], bgate[l], lam_a[l], lc)
        qn, kn, v, bg = _gdn_prep(z, conv_b_w[l], avec[l], dtvec[l], lc)
        wq, u, l2, egl = _gdn_chunk(qn, kn, v, bg, expanders)
        ob_f, ob_b = _gdn_rec(wq, u, l2, egl, lc)
        h = _outproj(ya_f, ya_b, ob_f, ob_b, oc, z, h, mod[l], onorm_b[l][None, :], w_out_r[l], lc)

    return _final_norm(h, final_g[None, :], lc)[None]
```

```python
import functools
import math

import numpy as np
import jax
import jax.numpy as jnp
from jax import lax
from jax.experimental import pallas as pl
from jax.experimental.pallas import tpu as pltpu

F32 = jnp.float32
BF16 = jnp.bfloat16

D_MODEL = 2048
GRID_W = 64
EPS = 1e-6
ROPE_BASE = 10000.0

D_A = 512
N_BLK_A = 8
BLK_A = D_A // N_BLK_A
LRU_C = 8.0

N_HEADS_B = 4
HEAD_DIM_B = 128
D_B = N_HEADS_B * HEAD_DIM_B
CHUNK_B = 64

N_Q_HEADS_C = 8
N_KV_HEADS_C = 2
GROUP_C = N_Q_HEADS_C // N_KV_HEADS_C
HEAD_DIM_C = 128
D_Q_C = N_Q_HEADS_C * HEAD_DIM_C
D_KV_C = N_KV_HEADS_C * HEAD_DIM_C

D_MIX = D_A + D_B + D_Q_C

COL_QC = 0
COL_GC = 1024
COL_KV = 2048
COL_XA = 2560
COL_QKVB = 3072
COL_GA = 4608
COL_GB = 5120
COL_BA = 5632
D_Z = 5760
LANES = 128
SUBLANES = 8
LOG2_E = math.log2(math.e)
ATTN_TILE = 256
V_ROWS = HEAD_DIM_C + 16
NEG_BIG = -1e30
GDN_REC_CHUNKS_PER_STEP = 2
GDN_CHUNKS_PER_STEP = 4

VMEM_LIMIT = 56 * 1024 * 1024


def _cparams(*sem):
    return pltpu.CompilerParams(dimension_semantics=tuple(sem), vmem_limit_bytes=VMEM_LIMIT)


def _dot(a, b):
    return jnp.dot(a, b, preferred_element_type=F32)


def _dot_nt(a, b):
    return lax.dot_general(a, b, (((1,), (1,)), ((), ())), preferred_element_type=F32)


def _split2(a):
    hi = a.astype(BF16)
    lo = (a - hi.astype(F32)).astype(BF16)
    return hi, lo


def _split3(a):
    hi = a.astype(BF16)
    r = a - hi.astype(F32)
    mid = r.astype(BF16)
    lo = (r - mid.astype(F32)).astype(BF16)
    return hi, mid, lo


def _dot3(a, b):
    ah, al = _split2(a)
    bh, bl = _split2(b)
    return _dot(ah, bh) + (_dot(al, bh) + _dot(ah, bl))


def _silu(x):
    return x * jax.nn.sigmoid(x)


def _softplus(x):
    return jnp.maximum(x, 0.0) + jnp.log1p(jnp.exp(-jnp.abs(x)))


def _pick(n, cands):
    for c in cands:
        if n % c == 0:
            return c
    raise ValueError(f"no tile for {n} in {cands}")


def _mod_kernel(c_ref, w_ref, b_ref, o_ref):
    o_ref[0] = _dot3(_silu(c_ref[...]), w_ref[0]) + b_ref[0]


def _modulation(cc, w_mod, b_mod):
    depth, d, d3 = w_mod.shape
    tn = 768
    return pl.pallas_call(
        _mod_kernel,
        grid=(depth, d3 // tn),
        in_specs=[
            pl.BlockSpec((SUBLANES, d), lambda l, j: (0, 0)),
            pl.BlockSpec((1, d, tn), lambda l, j: (l, 0, j)),
            pl.BlockSpec((1, 1, tn), lambda l, j: (l, 0, j)),
        ],
        out_specs=pl.BlockSpec((1, SUBLANES, tn), lambda l, j: (l, 0, j)),
        out_shape=jax.ShapeDtypeStruct((depth, SUBLANES, d3), F32),
        compiler_params=_cparams("parallel", "parallel"),
        name="modulation",
    )(cc, w_mod, b_mod.reshape(depth, 1, d3))


def _adaln(h_new, row0, g_ref, mod_ref, lc):
    xn = h_new * lax.rsqrt(jnp.mean(h_new * h_new, axis=-1, keepdims=True) + EPS) * g_ref[...]
    row = row0 + lax.broadcasted_iota(jnp.int32, (h_new.shape[0], 1), 0)
    is_ctx = row < lc
    shift = jnp.where(is_ctx, mod_ref[1:2, 0:D_MODEL], mod_ref[0:1, 0:D_MODEL])
    scale = jnp.where(is_ctx, mod_ref[1:2, D_MODEL:2 * D_MODEL], mod_ref[0:1, D_MODEL:2 * D_MODEL])
    return (xn * (1.0 + scale) + shift).astype(BF16)


def _prenorm_kernel(h_ref, mod_ref, g_ref, y_ref, *, lc, tm):
    y_ref[...] = _adaln(h_ref[...], pl.program_id(0) * tm, g_ref, mod_ref, lc)


def _prenorm(h, mod_l, norm_g_l, lc):
    n = h.shape[0]
    tm = 256
    return pl.pallas_call(
        functools.partial(_prenorm_kernel, lc=lc, tm=tm),
        grid=(n // tm,),
        in_specs=[
            pl.BlockSpec((tm, D_MODEL), lambda i: (i, 0)),
            pl.BlockSpec((SUBLANES, 3 * D_MODEL), lambda i: (0, 0)),
            pl.BlockSpec((1, D_MODEL), lambda i: (0, 0)),
        ],
        out_specs=pl.BlockSpec((tm, D_MODEL), lambda i: (i, 0)),
        out_shape=jax.ShapeDtypeStruct((n, D_MODEL), BF16),
        compiler_params=_cparams("parallel"),
        name="prenorm",
    )(h, mod_l, norm_g_l)


def _inproj_kernel(y_ref, w_ref, z_ref):
    z_ref[...] = _dot(y_ref[...], w_ref[...])


def _inproj(y, w_l):
    n = y.shape[0]
    tm = _pick(n, (1408, 768, 640, 512, 256))
    tn = 1920
    return pl.pallas_call(
        _inproj_kernel,
        grid=(n // tm, D_Z // tn),
        in_specs=[
            pl.BlockSpec((tm, D_MODEL), lambda i, j: (i, 0)),
            pl.BlockSpec((D_MODEL, tn), lambda i, j: (0, j)),
        ],
        out_specs=pl.BlockSpec((tm, tn), lambda i, j: (i, j)),
        out_shape=jax.ShapeDtypeStruct((n, D_Z), F32),
        compiler_params=_cparams("parallel", "parallel"),
        name="inproj",
    )(y, w_l)


def _attn_prep_kernel(q_ref, kv_ref, cos_ref, sin_ref, qn_ref, kn_ref, qo_ref, ko_ref, vo_ref):
    cos = cos_ref[...]
    sin = sin_ref[...]

    def prep(x, gain, scale):
        y = x * lax.rsqrt(jnp.mean(x * x, axis=-1, keepdims=True) + EPS) * gain
        rot = pltpu.roll(y, HEAD_DIM_C // 2, 1)
        return (y * cos + rot * sin) * scale

    for hd in range(N_Q_HEADS_C):
        sl = slice(hd * HEAD_DIM_C, (hd + 1) * HEAD_DIM_C)
        qo_ref[:, sl] = prep(q_ref[:, sl], qn_ref[...], HEAD_DIM_C ** -0.5 * LOG2_E).astype(BF16)
    for hd in range(N_KV_HEADS_C):
        sl = slice(hd * HEAD_DIM_C, (hd + 1) * HEAD_DIM_C)
        ko_ref[:, sl] = prep(kv_ref[:, sl], kn_ref[...], 1.0).astype(BF16)
    vt = kv_ref[:, D_KV_C:2 * D_KV_C].T
    ones = jnp.ones((V_ROWS - HEAD_DIM_C, vt.shape[1]), F32)
    for hd in range(N_KV_HEADS_C):
        vo_ref[0, hd] = jnp.concatenate([vt[hd * HEAD_DIM_C:(hd + 1) * HEAD_DIM_C], ones], axis=0).astype(BF16)


def _attn_prep(z, cos2, sin2, qn_l, kn_l):
    n = z.shape[0]
    tm = ATTN_TILE
    return pl.pallas_call(
        _attn_prep_kernel,
        grid=(n // tm,),
        in_specs=[
            pl.BlockSpec((tm, D_Q_C), lambda i: (i, COL_QC // D_Q_C)),
            pl.BlockSpec((tm, 2 * D_KV_C), lambda i: (i, COL_KV // (2 * D_KV_C))),
            pl.BlockSpec((tm, HEAD_DIM_C), lambda i: (i, 0)),
            pl.BlockSpec((tm, HEAD_DIM_C), lambda i: (i, 0)),
            pl.BlockSpec((1, HEAD_DIM_C), lambda i: (0, 0)),
            pl.BlockSpec((1, HEAD_DIM_C), lambda i: (0, 0)),
        ],
        out_specs=[
            pl.BlockSpec((tm, D_Q_C), lambda i: (i, 0)),
            pl.BlockSpec((tm, D_KV_C), lambda i: (i, 0)),
            pl.BlockSpec((1, N_KV_HEADS_C, V_ROWS, tm), lambda i: (i, 0, 0, 0)),
        ],
        out_shape=[
            jax.ShapeDtypeStruct((n, D_Q_C), BF16),
            jax.ShapeDtypeStruct((n, D_KV_C), BF16),
            jax.ShapeDtypeStruct((n // tm, N_KV_HEADS_C, V_ROWS, tm), BF16),
        ],
        compiler_params=_cparams("parallel"),
        name="attn_prep",
    )(z, z, cos2, sin2, qn_l, kn_l)


def _attn_kernel(q_ref, k_ref, vt_ref, o_ref, s0_scr, sa_scr, sb_scr, acc_scr, *, lc, tq, tk, n_super):
    nb = lc // tk

    def scores(row0, nrows, s_scr):
        kb = k_ref[pl.ds(row0, nrows), :]
        for g in range(GROUP_C):
            s_scr[g] = _dot_nt(kb, q_ref[:, g * HEAD_DIM_C:(g + 1) * HEAD_DIM_C])

    def softmax_pv(s_scr, blk0, n_blk, ms):
        vts = [vt_ref[blk0 + b, 0] for b in range(n_blk)]
        new_m = []
        for g in range(GROUP_C):
            st = s_scr[g]
            m_blk = jnp.max(st, axis=0, keepdims=True)
            m_new = m_blk if ms is None else jnp.maximum(ms[g], m_blk)
            p = jnp.exp2(st - m_new).astype(BF16)
            pv = _dot(vts[0], p[0:tk])
            for b in range(1, n_blk):
                pv = pv + _dot(vts[b], p[b * tk:(b + 1) * tk])
            acc_scr[g] = pv if ms is None else jnp.exp2(ms[g] - m_new) * acc_scr[g] + pv
            new_m.append(m_new)
        return tuple(new_m)

    def super_rows(j):
        return pl.multiple_of(lc + j * (2 * tk), tk)

    scores(0, lc, s0_scr)
    ms0 = softmax_pv(s0_scr, 0, nb, None)

    @pl.when(pl.program_id(1) * tq >= lc)
    def _():
        def body(i, ms):
            j = 2 * i
            scores(super_rows(j + 1), 2 * tk, sb_scr)
            ms = softmax_pv(sa_scr, nb + 2 * j, 2, ms)
            scores(super_rows(j + 2), 2 * tk, sa_scr)
            return softmax_pv(sb_scr, nb + 2 * j + 2, 2, ms)

        scores(super_rows(0), 2 * tk, sa_scr)
        ms = lax.fori_loop(0, n_super // 2 - 1, body, ms0)
        scores(super_rows(n_super - 1), 2 * tk, sb_scr)
        ms = softmax_pv(sa_scr, nb + 2 * (n_super - 2), 2, ms)
        softmax_pv(sb_scr, nb + 2 * (n_super - 1), 2, ms)

    for g in range(GROUP_C):
        acc = acc_scr[g]
        o_ref[:, g * HEAD_DIM_C:(g + 1) * HEAD_DIM_C] = (
            acc[0:HEAD_DIM_C] / acc[HEAD_DIM_C:HEAD_DIM_C + 1]).T


def _attention(qh, kh, vt, lc):
    n = qh.shape[0]
    tq = tk = ATTN_TILE
    assert lc % tq == 0 and n % tq == 0
    n_super = (n - lc) // (2 * tk)
    assert n_super * 2 * tk == n - lc and n_super % 2 == 0
    gw = GROUP_C * HEAD_DIM_C
    return pl.pallas_call(
        functools.partial(_attn_kernel, lc=lc, tq=tq, tk=tk, n_super=n_super),
        grid=(N_KV_HEADS_C, n // tq),
        in_specs=[
            pl.BlockSpec((tq, gw), lambda j, i: (i, j)),
            pl.BlockSpec((n, HEAD_DIM_C), lambda j, i: (0, j)),
            pl.BlockSpec((n // tk, 1, V_ROWS, tk), lambda j, i: (0, j, 0, 0)),
        ],
        out_specs=pl.BlockSpec((tq, gw), lambda j, i: (i, j)),
        out_shape=jax.ShapeDtypeStruct((n, D_Q_C), F32),
        scratch_shapes=[pltpu.VMEM((GROUP_C, lc, tq), F32),
                        pltpu.VMEM((GROUP_C, 2 * tk, tq), F32), pltpu.VMEM((GROUP_C, 2 * tk, tq), F32),
                        pltpu.VMEM((GROUP_C, V_ROWS, tq), F32)],
        compiler_params=_cparams("parallel", "parallel"),
        name="attention",
    )(qh, kh, vt)


def _conv4(x, prev8, next8, w_ref, prev_ok, next_ok):
    tm = x.shape[0]
    row = lax.broadcasted_iota(jnp.int32, (tm, 1), 0)
    p = jnp.where(prev_ok, prev8, 0.0)
    nx = jnp.where(next_ok, next8, 0.0)
    xm1 = jnp.where(row == 0, p[7:8], pltpu.roll(x, 1, 0))
    xm2 = jnp.where(row == 0, p[6:7], jnp.where(row == 1, p[7:8], pltpu.roll(x, 2, 0)))
    xp1 = jnp.where(row == tm - 1, nx[0:1], pltpu.roll(x, tm - 1, 0))
    return w_ref[0:1] * xm2 + w_ref[1:2] * xm1 + w_ref[2:3] * x + w_ref[3:4] * xp1


def _seg_flags(t, nbc, nt):
    prev_ok = jnp.logical_and(t != 0, t != nbc)
    next_ok = jnp.logical_and(t != nbc - 1, t != nt - 1)
    return prev_ok, next_ok


def _halo_maps(tile_of, tm, n, col_block):
    r = tm // SUBLANES
    last = n // SUBLANES - 1
    main = lambda i: (tile_of(i), col_block)
    prev = lambda i: (jnp.maximum(tile_of(i) * r - 1, 0), col_block)
    nxt = lambda i: (jnp.minimum((tile_of(i) + 1) * r, last), col_block)
    return main, prev, nxt


def _rglru_kernel(xf_ref, pf_ref, nf_ref, xb_ref, pb_ref, nb_ref, cw_ref, cb_ref, wg_ref, bg_ref,
                  lam_ref, of_ref, ob_ref, x_scr, hcar, *, tm, nbc, nt):
    i = pl.program_id(0)
    seg = tm // SUBLANES
    ncg = D_A // LANES
    sub = lax.broadcasted_iota(jnp.int32, (SUBLANES, D_A), 0)

    @pl.when(i == 0)
    def _():
        hcar[...] = jnp.zeros_like(hcar)

    tiles = (i, jnp.where(i < nbc, nbc - 1 - i, nt - 1 - (i - nbc)))
    refs = ((xf_ref, pf_ref, nf_ref, of_ref), (xb_ref, pb_ref, nb_ref, ob_ref))
    for d in range(2):
        x_ref, p_ref, n_ref, o_ref = refs[d]
        prev_ok, next_ok = _seg_flags(tiles[d], nbc, nt)
        p = jnp.where(prev_ok, p_ref[...], 0.0)
        nx = jnp.where(next_ok, n_ref[...], 0.0)
        for cg in range(ncg):
            x_scr[cg] = x_ref[:, cg * LANES:(cg + 1) * LANES]
        xg = [jnp.concatenate([x_scr[cg, pl.ds(j, SUBLANES, stride=seg), :] for cg in range(ncg)], axis=1)
              for j in range(seg)]
        before1 = jnp.where(sub == 0, p[7:8], pltpu.roll(xg[seg - 1], 1, 0))
        before2 = jnp.where(sub == 0, p[6:7], pltpu.roll(xg[seg - 2], 1, 0))
        after1 = jnp.where(sub == SUBLANES - 1, nx[0:1], pltpu.roll(xg[0], SUBLANES - 1, 0))
        xp = jnp.concatenate(xg, axis=0)
        xm1 = jnp.concatenate([before1] + xg[:seg - 1], axis=0)
        xm2 = jnp.concatenate([before2, before1] + xg[:seg - 2], axis=0)
        xp1 = jnp.concatenate(xg[1:] + [after1], axis=0)
        u = (cw_ref[0:1] * xm2 + cw_ref[1:2] * xm1 + cw_ref[2:3] * xp + cw_ref[3:4] * xp1) + cb_ref[...]
        gates = _dot(u.astype(BF16), wg_ref[d]) + bg_ref[d]
        gate_r = jax.nn.sigmoid(gates[:, 0:D_A])
        gate_i = jax.nn.sigmoid(gates[:, D_A:2 * D_A])
        log_a = (-LRU_C) * gate_r * _softplus(-lam_ref[d:d + 1, :])
        a = jnp.exp(log_a)
        b = jnp.sqrt(-jnp.tanh(log_a) * (a * a + 1.0)) * (gate_i * u)

        order = list(range(seg)) if d == 0 else list(range(seg - 1, -1, -1))
        hloc = jnp.zeros((SUBLANES, D_A), F32)
        pcum = jnp.ones((SUBLANES, D_A), F32)
        hs, ps = [None] * seg, [None] * seg
        for j in order:
            aj = a[j * SUBLANES:(j + 1) * SUBLANES]
            hloc = aj * hloc + b[j * SUBLANES:(j + 1) * SUBLANES]
            pcum = aj * pcum
            hs[j], ps[j] = hloc, pcum
        carry = hcar[d:d + 1, :]
        seg_in = [None] * SUBLANES
        for s in (range(SUBLANES) if d == 0 else range(SUBLANES - 1, -1, -1)):
            seg_in[s] = carry
            carry = pcum[s:s + 1] * carry + hloc[s:s + 1]
        hcar[d:d + 1, :] = carry
        cin = jnp.concatenate(seg_in, axis=0)
        for j in range(seg):
            hj = hs[j] + ps[j] * cin
            for cg in range(ncg):
                o_ref[cg, pl.ds(j, SUBLANES, stride=seg), :] = hj[:, cg * LANES:(cg + 1) * LANES]


def _rglru(z, cw, cb, wg, bg, lam, lc):
    n = z.shape[0]
    tm = 256
    assert lc % tm == 0 and n % tm == 0
    nbc, nt = lc // tm, n // tm
    fwd = lambda i: i
    bwd = lambda i: jnp.where(i < nbc, nbc - 1 - i, nt - 1 - (i - nbc))
    cblk = COL_XA // D_A
    fm, fp, fn = _halo_maps(fwd, tm, n, cblk)
    bm, bp, bn = _halo_maps(bwd, tm, n, cblk)
    const2 = lambda i: (0, 0)
    const3 = lambda i: (0, 0, 0)
    return pl.pallas_call(
        functools.partial(_rglru_kernel, tm=tm, nbc=nbc, nt=nt),
        grid=(nt,),
        in_specs=[
            pl.BlockSpec((tm, D_A), fm), pl.BlockSpec((SUBLANES, D_A), fp), pl.BlockSpec((SUBLANES, D_A), fn),
            pl.BlockSpec((tm, D_A), bm), pl.BlockSpec((SUBLANES, D_A), bp), pl.BlockSpec((SUBLANES, D_A), bn),
            pl.BlockSpec((4, D_A), const2), pl.BlockSpec((1, D_A), const2),
            pl.BlockSpec((2, D_A, 2 * D_A), const3), pl.BlockSpec((2, 1, 2 * D_A), const3),
            pl.BlockSpec((2, D_A), const2),
        ],
        out_specs=[pl.BlockSpec((D_A // LANES, tm, LANES), lambda i: (0, fwd(i), 0)),
                   pl.BlockSpec((D_A // LANES, tm, LANES), lambda i: (0, bwd(i), 0))],
        out_shape=[jax.ShapeDtypeStruct((D_A // LANES, n, LANES), F32)] * 2,
        scratch_shapes=[pltpu.VMEM((D_A // LANES, tm, LANES), F32), pltpu.VMEM((SUBLANES, D_A), F32)],
        compiler_params=_cparams("arbitrary"),
        name="rglru",
    )(z, z, z, z, z, z, cw, cb, wg, bg, lam)


def _gdn_prep_kernel(x_ref, p_ref, n_ref, ba_ref, cw_ref, av_ref, dt_ref, q_ref, k_ref, v_ref, bg_ref,
                     *, nbc, nt):
    prev_ok, next_ok = _seg_flags(pl.program_id(0), nbc, nt)
    u = _silu(_conv4(x_ref[...], p_ref[...], n_ref[...], cw_ref, prev_ok, next_ok))
    for hd in range(N_HEADS_B):
        sl = slice(hd * HEAD_DIM_B, (hd + 1) * HEAD_DIM_B)
        qv = u[:, hd * HEAD_DIM_B:(hd + 1) * HEAD_DIM_B]
        kv = u[:, D_B + hd * HEAD_DIM_B:D_B + (hd + 1) * HEAD_DIM_B]
        q_ref[:, sl] = qv * lax.rsqrt(jnp.sum(qv * qv, axis=-1, keepdims=True) + EPS)
        k_ref[:, sl] = kv * lax.rsqrt(jnp.sum(kv * kv, axis=-1, keepdims=True) + EPS)
    v_ref[...] = u[:, 2 * D_B:3 * D_B]
    ba = ba_ref[...]
    lane = lax.broadcasted_iota(jnp.int32, ba.shape, 1)
    beta = jax.nn.sigmoid(ba)
    g = -jnp.exp(av_ref[...]) * _softplus(ba + dt_ref[...])
    bg_ref[...] = jnp.where(lane < 2 * N_HEADS_B, beta, jnp.where(lane < 4 * N_HEADS_B, g, 0.0))


def _gdn_prep(z, cw, avec, dtvec, lc):
    n = z.shape[0]
    tm = 256
    nbc, nt = lc // tm, n // tm
    w3 = 3 * D_B
    assert COL_QKVB % w3 == 0
    m, p, nx = _halo_maps(lambda i: i, tm, n, COL_QKVB // w3)
    const2 = lambda i: (0, 0)
    return pl.pallas_call(
        functools.partial(_gdn_prep_kernel, nbc=nbc, nt=nt),
        grid=(nt,),
        in_specs=[
            pl.BlockSpec((tm, w3), m), pl.BlockSpec((SUBLANES, w3), p), pl.BlockSpec((SUBLANES, w3), nx),
            pl.BlockSpec((tm, LANES), lambda i: (i, COL_BA // LANES)),
            pl.BlockSpec((4, w3), const2), pl.BlockSpec((1, LANES), const2), pl.BlockSpec((1, LANES), const2),
        ],
        out_specs=[pl.BlockSpec((tm, D_B), lambda i: (i, 0))] * 3 + [pl.BlockSpec((tm, LANES), lambda i: (i, 0))],
        out_shape=[jax.ShapeDtypeStruct((n, D_B), F32)] * 3 + [jax.ShapeDtypeStruct((n, LANES), F32)],
        compiler_params=_cparams("parallel"),
        name="gdn_prep",
    )(z, z, z, z, cw, avec, dtvec)


def _unit_tri_inverse(mats, row, col, lane_lo):
    def block_diag(y):
        return jnp.concatenate([jnp.where(lane_lo, y, 0.0), jnp.where(lane_lo, 0.0, y)], axis=0).astype(BF16)

    def mm(xs, ys):
        return [_dot(x.astype(BF16), block_diag(y)) for x, y in zip(xs, ys)]

    def same(shift):
        return (row >> shift) == (col >> shift)

    eye = jnp.where(row == col, 1.0, 0.0)
    m8 = [jnp.where(same(3), m, 0.0) for m in mats]
    xs = [eye - m for m in m8]
    pw = mm(m8, m8)
    xs = [x + y for x, y in zip(xs, mm(xs, pw))]
    pw = mm(pw, pw)
    xs = [x + y for x, y in zip(xs, mm(xs, pw))]
    for shift in (4, 5, 6):
        off = [jnp.where(same(shift), jnp.where(same(shift - 1), 0.0, m), 0.0) for m in mats]
        xs = [x - y for x, y in zip(xs, mm(xs, mm(off, xs)))]
    return xs


def _gdn_chunk_kernel(q_ref, k_ref, v_ref, bg_ref, e64_ref, eb_ref, eg_ref, wq_ref, u_ref, l2_ref, egl_ref):
    cs = CHUNK_B
    nh = N_HEADS_B
    ri = lax.broadcasted_iota(jnp.int32, (cs, cs), 0)
    ci = lax.broadcasted_iota(jnp.int32, (cs, cs), 1)
    ones = jnp.ones((cs, cs), BF16)
    row = lax.broadcasted_iota(jnp.int32, (cs, nh * cs), 0)
    col = lax.broadcasted_iota(jnp.int32, (cs, nh * cs), 1) & (cs - 1)
    lane_lo = lax.broadcasted_iota(jnp.int32, (cs, LANES), 1) < cs
    lane_lo2 = lax.broadcasted_iota(jnp.int32, (2 * cs, LANES), 1) < cs
    zeros = jnp.zeros((cs, HEAD_DIM_B), F32)
    tri = (jnp.where(ci <= ri, 1.0, 0.0).astype(BF16), jnp.where(ci >= ri, 1.0, 0.0).astype(BF16))
    strict = (row > col, row < col)
    causal = (row >= col, row <= col)

    def dot_parts(lhs, parts):
        return _dot(lhs, parts[0]) + (_dot(lhs, parts[1]) + _dot(lhs, parts[2]))

    stage = {}
    mats = []
    for c in range(GDN_CHUNKS_PER_STEP):
        rows = slice(c * cs, (c + 1) * cs)
        qs = q_ref[rows, :] * (HEAD_DIM_B ** -0.5)
        kn = k_ref[rows, :]
        v = v_ref[rows, :]
        bg_parts = _split3(bg_ref[rows, :])
        for d in range(2):
            expand = lambda e: [_dot(part, e).astype(BF16) for part in bg_parts]
            g64 = [jnp.where(strict[d], part, 0.0).astype(BF16) for part in expand(e64_ref[d])]
            decay = jnp.where(causal[d], jnp.exp(dot_parts(tri[d], g64)), 0.0)
            beta_parts = expand(eb_ref[d])
            beta = beta_parts[0].astype(F32) + (beta_parts[1].astype(F32) + beta_parts[2].astype(F32))
            g_parts = expand(eg_ref[d])
            gc = dot_parts(tri[d], g_parts)
            gl = dot_parts(ones, g_parts)
            egc = jnp.exp(gc)
            kb = kn * beta
            lhs = jnp.concatenate([kb, qs], axis=0).astype(BF16)
            prods = []
            for pr in range(2):
                ka = kn[:, (2 * pr) * HEAD_DIM_B:(2 * pr + 1) * HEAD_DIM_B]
                kb_ = kn[:, (2 * pr + 1) * HEAD_DIM_B:(2 * pr + 2) * HEAD_DIM_B]
                rhs_t = jnp.concatenate([jnp.concatenate([ka, zeros], axis=1),
                                         jnp.concatenate([zeros, kb_], axis=1)], axis=0).astype(BF16)
                prods.append(_dot_nt(lhs[:, 2 * pr * HEAD_DIM_B:(2 * pr + 2) * HEAD_DIM_B], rhs_t))
            prod = jnp.concatenate(prods, axis=1)
            mat = jnp.where(strict[d], prod[0:cs] * decay, 0.0)
            mats += [mat[:, 0:LANES], mat[:, LANES:2 * LANES]]
            stage[c, d] = dict(qk=prod[cs:2 * cs] * decay, vb=v * beta, kbe=kb * egc, qhead=qs * egc,
                               ktail=kn * jnp.exp(gl - gc), egl=jnp.exp(gl))

    row2 = lax.broadcasted_iota(jnp.int32, (cs, LANES), 0)
    col2 = lax.broadcasted_iota(jnp.int32, (cs, LANES), 1) & (cs - 1)
    tmats = _unit_tri_inverse(mats, row2, col2, lane_lo)

    for c in range(GDN_CHUNKS_PER_STEP):
        for d in range(2):
            st = stage[c, d]
            for pr in range(2):
                ha, hb = 2 * pr, 2 * pr + 1
                sa = slice(ha * HEAD_DIM_B, (ha + 1) * HEAD_DIM_B)
                sb = slice(hb * HEAD_DIM_B, (hb + 1) * HEAD_DIM_B)
                tp = tmats[(c * 2 + d) * 2 + pr]
                rhs = jnp.concatenate([jnp.concatenate([st["vb"][:, sa], st["kbe"][:, sa]], axis=1),
                                       jnp.concatenate([st["vb"][:, sb], st["kbe"][:, sb]], axis=1)],
                                      axis=0).astype(BF16)
                ra = _dot(jnp.where(lane_lo, tp, 0.0).astype(BF16), rhs)
                rb = _dot(jnp.where(lane_lo, 0.0, tp).astype(BF16), rhs)
                for hh, res, sl in ((ha, ra, sa), (hb, rb, sb)):
                    u_ref[c, d * nh + hh] = res[:, 0:HEAD_DIM_B]
                    wq_ref[c, d * nh + hh] = jnp.concatenate(
                        [res[:, HEAD_DIM_B:2 * HEAD_DIM_B], st["qhead"][:, sl]], axis=0).astype(BF16)
                kt_t = jnp.concatenate([st["ktail"][:, sa], st["ktail"][:, sb]], axis=0).T
                qkp = st["qk"][:, pr * LANES:(pr + 1) * LANES]
                l2_ref[c, d * 2 + pr] = jnp.concatenate(
                    [jnp.where(lane_lo, qkp, 0.0), jnp.where(lane_lo, 0.0, qkp),
                     jnp.where(lane_lo2, kt_t, 0.0), jnp.where(lane_lo2, 0.0, kt_t)], axis=0).astype(BF16)
            egl_ref[c, d] = jnp.concatenate(
                [st["egl"][0:1, hh * HEAD_DIM_B:(hh + 1) * HEAD_DIM_B] for hh in range(nh)]
                + [jnp.zeros((SUBLANES - nh, HEAD_DIM_B), F32)], axis=0)


def _gdn_expanders():
    nh, cs = N_HEADS_B, CHUNK_B
    e64 = np.zeros((2, LANES, nh * cs), np.float32)
    eb = np.zeros((2, LANES, nh * HEAD_DIM_B), np.float32)
    eg = np.zeros((2, LANES, nh * HEAD_DIM_B), np.float32)
    for d in range(2):
        for hh in range(nh):
            p = d * nh + hh
            e64[d, 2 * nh + p, hh * cs:(hh + 1) * cs] = 1.0
            eb[d, p, hh * HEAD_DIM_B:(hh + 1) * HEAD_DIM_B] = 1.0
            eg[d, 2 * nh + p, hh * HEAD_DIM_B:(hh + 1) * HEAD_DIM_B] = 1.0
    return jnp.asarray(e64, BF16), jnp.asarray(eb, BF16), jnp.asarray(eg, BF16)


def _gdn_chunk(qn, kn, v, bg, expanders):
    n = qn.shape[0]
    cs, nh = CHUNK_B, N_HEADS_B
    nc = n // cs
    g = GDN_CHUNKS_PER_STEP
    assert nc % g == 0
    e64, eb, eg = expanders
    const3 = lambda c: (0, 0, 0)
    return pl.pallas_call(
        _gdn_chunk_kernel,
        grid=(nc // g,),
        in_specs=[
            pl.BlockSpec((g * cs, D_B), lambda c: (c, 0)), pl.BlockSpec((g * cs, D_B), lambda c: (c, 0)),
            pl.BlockSpec((g * cs, D_B), lambda c: (c, 0)), pl.BlockSpec((g * cs, LANES), lambda c: (c, 0)),
            pl.BlockSpec(e64.shape, const3), pl.BlockSpec(eb.shape, const3), pl.BlockSpec(eg.shape, const3),
        ],
        out_specs=[
            pl.BlockSpec((g, 2 * nh, 2 * cs, HEAD_DIM_B), lambda c: (c, 0, 0, 0)),
            pl.BlockSpec((g, 2 * nh, cs, HEAD_DIM_B), lambda c: (c, 0, 0, 0)),
            pl.BlockSpec((g, 4, 6 * cs, LANES), lambda c: (c, 0, 0, 0)),
            pl.BlockSpec((g, 2, SUBLANES, HEAD_DIM_B), lambda c: (c, 0, 0, 0)),
        ],
        out_shape=[
            jax.ShapeDtypeStruct((nc, 2 * nh, 2 * cs, HEAD_DIM_B), BF16),
            jax.ShapeDtypeStruct((nc, 2 * nh, cs, HEAD_DIM_B), F32),
            jax.ShapeDtypeStruct((nc, 4, 6 * cs, LANES), BF16),
            jax.ShapeDtypeStruct((nc, 2, SUBLANES, HEAD_DIM_B), F32),
        ],
        compiler_params=_cparams("parallel"),
        name="gdn_chunk",
    )(qn, kn, v, bg, e64, eb, eg)


def _gdn_rec_kernel(wqf_ref, uf_ref, l2f_ref, egf_ref, wqb_ref, ub_ref, l2b_ref, egb_ref,
                    of_ref, ob_ref, s_scr):
    cs, nh = CHUNK_B, N_HEADS_B

    @pl.when(pl.program_id(0) == 0)
    def _():
        s_scr[...] = jnp.zeros_like(s_scr)

    dirs = ((wqf_ref, uf_ref, l2f_ref, egf_ref, of_ref), (wqb_ref, ub_ref, l2b_ref, egb_ref, ob_ref))
    for step in range(GDN_REC_CHUNKS_PER_STEP):
        for d in range(2):
            wq_ref, u_ref, l2_ref, eg_ref, o_ref = dirs[d]
            c = step if d == 0 else GDN_REC_CHUNKS_PER_STEP - 1 - step
            for pr in range(2):
                v_new, o_state = [], []
                for e in range(2):
                    hh = 2 * pr + e
                    r = _dot(wq_ref[c, hh], s_scr[d * nh + hh].astype(BF16))
                    v_new.append(u_ref[c, hh] - r[0:cs])
                    o_state.append(r[cs:2 * cs])
                r2 = _dot(l2_ref[c, pr], jnp.concatenate(v_new, axis=0).astype(BF16))
                for e in range(2):
                    hh = 2 * pr + e
                    o_ref[c * cs:(c + 1) * cs, hh * HEAD_DIM_B:(hh + 1) * HEAD_DIM_B] = (
                        o_state[e] + r2[e * cs:(e + 1) * cs])
                    s_scr[d * nh + hh] = (s_scr[d * nh + hh] * eg_ref[c, 0, hh:hh + 1, :]
                                          + r2[2 * cs + e * 2 * cs:2 * cs + (e + 1) * 2 * cs])


def _gdn_rec(wq, u, l2, egl, lc):
    cs, nh = CHUNK_B, N_HEADS_B
    g = GDN_REC_CHUNKS_PER_STEP
    assert wq.shape[0] % g == 0 and (lc // cs) % g == 0
    nc = wq.shape[0] // g
    ncc = lc // cs // g
    fwd = lambda i: i
    bwd = lambda i: jnp.where(i < ncc, ncc - 1 - i, nc - 1 - (i - ncc))
    specs = []
    for d, order in ((0, fwd), (1, bwd)):
        specs += [
            pl.BlockSpec((g, nh, 2 * cs, HEAD_DIM_B), lambda i, d=d, order=order: (order(i), d, 0, 0)),
            pl.BlockSpec((g, nh, cs, HEAD_DIM_B), lambda i, d=d, order=order: (order(i), d, 0, 0)),
            pl.BlockSpec((g, 2, 6 * cs, LANES), lambda i, d=d, order=order: (order(i), d, 0, 0)),
            pl.BlockSpec((g, 1, SUBLANES, HEAD_DIM_B), lambda i, d=d, order=order: (order(i), d, 0, 0)),
        ]
    return pl.pallas_call(
        _gdn_rec_kernel,
        grid=(nc,),
        in_specs=specs,
        out_specs=[pl.BlockSpec((g * cs, D_B), lambda i: (fwd(i), 0)),
                   pl.BlockSpec((g * cs, D_B), lambda i: (bwd(i), 0))],
        out_shape=[jax.ShapeDtypeStruct((nc * g * cs, D_B), F32)] * 2,
        scratch_shapes=[pltpu.VMEM((2 * nh, HEAD_DIM_B, HEAD_DIM_B), F32)],
        compiler_params=_cparams("arbitrary"),
        name="gdn_rec",
    )(wq, u, l2, egl, wq, u, l2, egl)


def _outproj_kernel(yaf_ref, yab_ref, obf_ref, obb_ref, oc_ref, ga_ref, gb_ref, gc_ref, h_ref, mod_ref,
                    on_ref, w_ref, gn_ref, modn_ref, *out_refs, lc, tm, row_off, last):
    row0 = (pl.program_id(0) + row_off) * tm
    ya = jnp.concatenate([yaf_ref[cg] + yab_ref[cg] for cg in range(D_A // LANES)], axis=1)
    mix_a = ya * _silu(ga_ref[...])
    ob = obf_ref[...] + obb_ref[...]
    gb = _silu(gb_ref[...])
    mix_b = []
    for hd in range(N_HEADS_B):
        sl = slice(hd * HEAD_DIM_B, (hd + 1) * HEAD_DIM_B)
        x = ob[:, sl]
        mix_b.append(x * lax.rsqrt(jnp.mean(x * x, axis=-1, keepdims=True) + EPS) * on_ref[...] * gb[:, sl])
    mix_c = oc_ref[...] * _silu(gc_ref[...])
    mix = jnp.concatenate([mix_a] + mix_b + [mix_c], axis=1).astype(BF16)
    row = row0 + lax.broadcasted_iota(jnp.int32, (tm, 1), 0)
    gate = jnp.where(row < lc, mod_ref[1:2, 2 * D_MODEL:3 * D_MODEL], mod_ref[0:1, 2 * D_MODEL:3 * D_MODEL])
    h_new = h_ref[...] + gate * _dot(mix, w_ref[...])
    if last:
        out_refs[0][...] = (h_new * lax.rsqrt(jnp.mean(h_new * h_new, axis=-1, keepdims=True) + EPS)
                            * gn_ref[...])
    else:
        out_refs[0][...] = h_new
        out_refs[1][...] = _adaln(h_new, row0, gn_ref, modn_ref, lc)


def _outproj(ya_f, ya_b, ob_f, ob_b, oc, z, h, mod_l, onorm_l, w_l, gain_next, mod_next, lc, last):
    n = h.shape[0]
    tm = ATTN_TILE if last else _pick(n, (384, 256))
    assert lc % tm == 0 or not last
    row_off = lc // tm if last else 0
    rows = lambda width, col: pl.BlockSpec((tm, width), lambda i: (i + row_off, col // width))
    const2 = lambda i: (0, 0)
    out_block = pl.BlockSpec((tm, D_MODEL), lambda i: (i, 0))
    if last:
        out_specs = [out_block]
        out_shape = [jax.ShapeDtypeStruct((n - lc, D_MODEL), F32)]
    else:
        out_specs = [out_block, out_block]
        out_shape = [jax.ShapeDtypeStruct((n, D_MODEL), F32), jax.ShapeDtypeStruct((n, D_MODEL), BF16)]
    return pl.pallas_call(
        functools.partial(_outproj_kernel, lc=lc, tm=tm, row_off=row_off, last=last),
        grid=((n - row_off * tm) // tm,),
        in_specs=[
            pl.BlockSpec((D_A // LANES, tm, LANES), lambda i: (0, i + row_off, 0)),
            pl.BlockSpec((D_A // LANES, tm, LANES), lambda i: (0, i + row_off, 0)),
            rows(D_B, 0), rows(D_B, 0), rows(D_Q_C, 0),
            rows(D_A, COL_GA), rows(D_B, COL_GB), rows(D_Q_C, COL_GC), rows(D_MODEL, 0),
            pl.BlockSpec((SUBLANES, 3 * D_MODEL), const2), pl.BlockSpec((1, HEAD_DIM_B), const2),
            pl.BlockSpec((D_MIX, D_MODEL), const2),
            pl.BlockSpec((1, D_MODEL), const2), pl.BlockSpec((SUBLANES, 3 * D_MODEL), const2),
        ],
        out_specs=out_specs,
        out_shape=out_shape,
        compiler_params=_cparams("parallel"),
        name="outproj_last" if last else "outproj",
    )(ya_f, ya_b, ob_f, ob_b, oc, z, z, z, h, mod_l, onorm_l, w_l, gain_next, mod_next)


def _rope_tables(t_len, lc):
    rows = t_len // GRID_W
    row = jnp.repeat(jnp.arange(rows, dtype=jnp.int32), GRID_W)
    col = jnp.tile(jnp.arange(GRID_W, dtype=jnp.int32), rows)
    n_freq = HEAD_DIM_C // 4
    inv_freq = ROPE_BASE ** (-jnp.arange(n_freq, dtype=F32) / n_freq)
    ang = jnp.concatenate([row.astype(F32)[:, None] * inv_freq, col.astype(F32)[:, None] * inv_freq], axis=-1)
    cos, sin = jnp.cos(ang), jnp.sin(ang)
    half = HEAD_DIM_C // 2
    cos2 = jnp.concatenate([jnp.ones((lc, 2 * half), F32), jnp.concatenate([cos, cos], axis=-1)], axis=0)
    sin2 = jnp.concatenate([jnp.zeros((lc, 2 * half), F32), jnp.concatenate([-sin, sin], axis=-1)], axis=0)
    return cos2, sin2


def _relayout_w_in_kernel(w_ref, o_ref):
    n_ba = 4 * N_HEADS_B
    src_tail = 2 * D_A + 4 * D_B
    copies = ((0, COL_XA, D_A), (D_A, COL_GA, D_A), (2 * D_A, COL_QKVB, 3 * D_B),
              (2 * D_A + 3 * D_B, COL_GB, D_B))
    for src, dst, width in copies:
        o_ref[0, :, dst:dst + width] = w_ref[0, :, src:src + width].astype(BF16)
    tail = w_ref[0, :, src_tail:]
    rows = tail.shape[0]
    o_ref[0, :, COL_BA:D_Z] = jnp.concatenate(
        [tail[:, 0:n_ba], jnp.zeros((rows, D_Z - COL_BA - n_ba), F32)], axis=1).astype(BF16)
    attn = tail[:, n_ba:].astype(BF16)
    o_ref[0, :, COL_QC:COL_QC + D_Q_C] = attn[:, 0:D_Q_C]
    o_ref[0, :, COL_KV:COL_KV + 2 * D_KV_C] = attn[:, D_Q_C:D_Q_C + 2 * D_KV_C]
    o_ref[0, :, COL_GC:COL_GC + D_Q_C] = attn[:, D_Q_C + 2 * D_KV_C:]


def _relayout_w_in(w_in):
    depth, d, d_in = w_in.shape
    tr = 256
    return pl.pallas_call(
        _relayout_w_in_kernel,
        grid=(depth, d // tr),
        in_specs=[pl.BlockSpec((1, tr, d_in), lambda l, i: (l, i, 0))],
        out_specs=pl.BlockSpec((1, tr, D_Z), lambda l, i: (l, i, 0)),
        out_shape=jax.ShapeDtypeStruct((depth, d, D_Z), BF16),
        compiler_params=_cparams("parallel", "parallel"),
        name="relayout_w_in",
    )(w_in)


def _block_diag(w):
    eye = jnp.eye(N_BLK_A, dtype=w.dtype)
    out = jnp.einsum('...nij,nm->...nimj', w, eye)
    return out.reshape(w.shape[:-3] + (D_A, D_A))


def kernel(x, c, ctx, c_ctx, norm_g, w_mod, b_mod, w_in, conv_a_w, conv_a_b, w_ra, b_ra, w_ia, b_ia, lam_a,
           conv_b_w, a_log_b, dt_bias_b, onorm_b, qn_c, kn_c, w_out, final_g):
    bsz, t_len, d = x.shape
    lc = ctx.shape[1]
    depth = w_in.shape[0]
    assert bsz == 1 and d == D_MODEL and t_len % GRID_W == 0

    h = jnp.concatenate([ctx[0], x[0]], axis=0)
    cc = jnp.concatenate([c, c_ctx[None, :], jnp.zeros((SUBLANES - 2, d), F32)], axis=0)
    mod = _modulation(cc, w_mod, b_mod)

    w_in_r = _relayout_w_in(w_in)
    w_out_r = w_out.astype(BF16)
    wg = jnp.concatenate([_block_diag(w_ra), _block_diag(w_ia)], axis=-1).astype(BF16)
    bgate = jnp.concatenate([b_ra, b_ia], axis=-1)[:, :, None, :]
    nh2 = 2 * N_HEADS_B
    lane_pad = lambda v: jnp.pad(v.reshape(depth, 1, nh2), ((0, 0), (0, 0), (nh2, LANES - 2 * nh2)))
    avec = lane_pad(a_log_b)
    dtvec = lane_pad(dt_bias_b)
    cos2, sin2 = _rope_tables(t_len, lc)
    expanders = _gdn_expanders()

    y = _prenorm(h, mod[0], norm_g[0][None, :], lc)
    for l in range(depth):
        last = l == depth - 1
        z = _inproj(y, w_in_r[l])
        qh, kh, vt = _attn_prep(z, cos2, sin2, qn_c[l][None, :], kn_c[l][None, :])
        oc = _attention(qh, kh, vt, lc)
        ya_f, ya_b = _rglru(z, conv_a_w[l], conv_a_b[l][None, :], wg[l], bgate[l], lam_a[l], lc)
        qn, kn, v, bg = _gdn_prep(z, conv_b_w[l], avec[l], dtvec[l], lc)
        wq, u, l2, egl = _gdn_chunk(qn, kn, v, bg, expanders)
        ob_f, ob_b = _gdn_rec(wq, u, l2, egl, lc)
        gain_next = final_g[None, :] if last else norm_g[l + 1][None, :]
        mod_next = mod[l] if last else mod[l + 1]
        outs = _outproj(ya_f, ya_b, ob_f, ob_b, oc, z, h, mod[l], onorm_b[l][None, :], w_out_r[l],
                        gain_next, mod_next, lc, last)
        if last:
            return outs[0][None]
        h, y = outs
```

```python
import functools
import math

import numpy as np
import jax
import jax.numpy as jnp
from jax import lax
from jax.experimental import pallas as pl
from jax.experimental.pallas import tpu as pltpu

F32 = jnp.float32
BF16 = jnp.bfloat16

D_MODEL = 2048
GRID_W = 64
EPS = 1e-6
ROPE_BASE = 10000.0

D_A = 512
N_BLK_A = 8
BLK_A = D_A // N_BLK_A
LRU_C = 8.0

N_HEADS_B = 4
HEAD_DIM_B = 128
D_B = N_HEADS_B * HEAD_DIM_B
CHUNK_B = 64

N_Q_HEADS_C = 8
N_KV_HEADS_C = 2
GROUP_C = N_Q_HEADS_C // N_KV_HEADS_C
HEAD_DIM_C = 128
D_Q_C = N_Q_HEADS_C * HEAD_DIM_C
D_KV_C = N_KV_HEADS_C * HEAD_DIM_C

D_MIX = D_A + D_B + D_Q_C

COL_QC = 0
COL_GC = 1024
COL_KV = 2048
COL_XA = 2560
COL_QKVB = 3072
COL_GA = 4608
COL_GB = 5120
COL_BA = 5632
D_Z = 5760
LANES = 128
SUBLANES = 8
LOG2_E = math.log2(math.e)
ATTN_TILE = 256
V_ROWS = HEAD_DIM_C + 16
NEG_BIG = -1e30
GDN_REC_CHUNKS_PER_STEP = 2
GDN_CHUNKS_PER_STEP = 4

VMEM_LIMIT = 56 * 1024 * 1024


def _cparams(*sem):
    return pltpu.CompilerParams(dimension_semantics=tuple(sem), vmem_limit_bytes=VMEM_LIMIT)


def _dot(a, b):
    return jnp.dot(a, b, preferred_element_type=F32)


def _dot_nt(a, b):
    return lax.dot_general(a, b, (((1,), (1,)), ((), ())), preferred_element_type=F32)


def _split2(a):
    hi = a.astype(BF16)
    lo = (a - hi.astype(F32)).astype(BF16)
    return hi, lo


def _split3(a):
    hi = a.astype(BF16)
    r = a - hi.astype(F32)
    mid = r.astype(BF16)
    lo = (r - mid.astype(F32)).astype(BF16)
    return hi, mid, lo


def _dot3(a, b):
    ah, al = _split2(a)
    bh, bl = _split2(b)
    return _dot(ah, bh) + (_dot(al, bh) + _dot(ah, bl))


def _silu(x):
    return x * jax.nn.sigmoid(x)


def _softplus(x):
    return jnp.maximum(x, 0.0) + jnp.log1p(jnp.exp(-jnp.abs(x)))


def _pick(n, cands):
    for c in cands:
        if n % c == 0:
            return c
    raise ValueError(f"no tile for {n} in {cands}")


def _mod_kernel(c_ref, w_ref, b_ref, o_ref):
    o_ref[0] = _dot3(_silu(c_ref[...]), w_ref[0]) + b_ref[0]


def _modulation(cc, w_mod, b_mod):
    depth, d, d3 = w_mod.shape
    tn = 768
    return pl.pallas_call(
        _mod_kernel,
        grid=(depth, d3 // tn),
        in_specs=[
            pl.BlockSpec((SUBLANES, d), lambda l, j: (0, 0)),
            pl.BlockSpec((1, d, tn), lambda l, j: (l, 0, j)),
            pl.BlockSpec((1, 1, tn), lambda l, j: (l, 0, j)),
        ],
        out_specs=pl.BlockSpec((1, SUBLANES, tn), lambda l, j: (l, 0, j)),
        out_shape=jax.ShapeDtypeStruct((depth, SUBLANES, d3), F32),
        compiler_params=_cparams("parallel", "parallel"),
        name="modulation",
    )(cc, w_mod, b_mod.reshape(depth, 1, d3))


def _adaln(h_new, row0, g_ref, mod_ref, lc):
    xn = h_new * lax.rsqrt(jnp.mean(h_new * h_new, axis=-1, keepdims=True) + EPS) * g_ref[...]
    row = row0 + lax.broadcasted_iota(jnp.int32, (h_new.shape[0], 1), 0)
    is_ctx = row < lc
    shift = jnp.where(is_ctx, mod_ref[1:2, 0:D_MODEL], mod_ref[0:1, 0:D_MODEL])
    scale = jnp.where(is_ctx, mod_ref[1:2, D_MODEL:2 * D_MODEL], mod_ref[0:1, D_MODEL:2 * D_MODEL])
    return (xn * (1.0 + scale) + shift).astype(BF16)


def _prenorm_kernel(h_ref, mod_ref, g_ref, y_ref, *, lc, tm):
    y_ref[...] = _adaln(h_ref[...], pl.program_id(0) * tm, g_ref, mod_ref, lc)


def _prenorm(h, mod_l, norm_g_l, lc):
    n = h.shape[0]
    tm = 256
    return pl.pallas_call(
        functools.partial(_prenorm_kernel, lc=lc, tm=tm),
        grid=(n // tm,),
        in_specs=[
            pl.BlockSpec((tm, D_MODEL), lambda i: (i, 0)),
            pl.BlockSpec((SUBLANES, 3 * D_MODEL), lambda i: (0, 0)),
            pl.BlockSpec((1, D_MODEL), lambda i: (0, 0)),
        ],
        out_specs=pl.BlockSpec((tm, D_MODEL), lambda i: (i, 0)),
        out_shape=jax.ShapeDtypeStruct((n, D_MODEL), BF16),
        compiler_params=_cparams("parallel"),
        name="prenorm",
    )(h, mod_l, norm_g_l)


def _inproj_kernel(y_ref, wt_ref, z_ref):
    z_ref[...] = _dot_nt(y_ref[...], wt_ref[0])


def _inproj(y, w_in_t, layer):
    n = y.shape[0]
    tm = _pick(n, (1408, 768, 640, 512, 256))
    tn = 1920
    return pl.pallas_call(
        _inproj_kernel,
        grid=(n // tm, D_Z // tn),
        in_specs=[
            pl.BlockSpec((tm, D_MODEL), lambda i, j: (i, 0)),
            pl.BlockSpec((1, tn, D_MODEL), lambda i, j: (layer, j, 0)),
        ],
        out_specs=pl.BlockSpec((tm, tn), lambda i, j: (i, j)),
        out_shape=jax.ShapeDtypeStruct((n, D_Z), F32),
        compiler_params=_cparams("parallel", "parallel"),
        name="inproj",
    )(y, w_in_t)


def _attn_prep_kernel(q_ref, kv_ref, cos_ref, sin_ref, qn_ref, kn_ref, qo_ref, ko_ref, vo_ref):
    cos = cos_ref[...]
    sin = sin_ref[...]

    def prep(x, gain, scale):
        y = x * lax.rsqrt(jnp.mean(x * x, axis=-1, keepdims=True) + EPS) * gain
        rot = pltpu.roll(y, HEAD_DIM_C // 2, 1)
        return (y * cos + rot * sin) * scale

    for hd in range(N_Q_HEADS_C):
        sl = slice(hd * HEAD_DIM_C, (hd + 1) * HEAD_DIM_C)
        qo_ref[:, sl] = prep(q_ref[:, sl], qn_ref[...], HEAD_DIM_C ** -0.5 * LOG2_E).astype(BF16)
    for hd in range(N_KV_HEADS_C):
        sl = slice(hd * HEAD_DIM_C, (hd + 1) * HEAD_DIM_C)
        ko_ref[:, sl] = prep(kv_ref[:, sl], kn_ref[...], 1.0).astype(BF16)
    vt = kv_ref[:, D_KV_C:2 * D_KV_C].T
    ones = jnp.ones((V_ROWS - HEAD_DIM_C, vt.shape[1]), F32)
    for hd in range(N_KV_HEADS_C):
        vo_ref[0, hd] = jnp.concatenate([vt[hd * HEAD_DIM_C:(hd + 1) * HEAD_DIM_C], ones], axis=0).astype(BF16)


def _attn_prep(z, cos2, sin2, qn_l, kn_l):
    n = z.shape[0]
    tm = ATTN_TILE
    return pl.pallas_call(
        _attn_prep_kernel,
        grid=(n // tm,),
        in_specs=[
            pl.BlockSpec((tm, D_Q_C), lambda i: (i, COL_QC // D_Q_C)),
            pl.BlockSpec((tm, 2 * D_KV_C), lambda i: (i, COL_KV // (2 * D_KV_C))),
            pl.BlockSpec((tm, HEAD_DIM_C), lambda i: (i, 0)),
            pl.BlockSpec((tm, HEAD_DIM_C), lambda i: (i, 0)),
            pl.BlockSpec((1, HEAD_DIM_C), lambda i: (0, 0)),
            pl.BlockSpec((1, HEAD_DIM_C), lambda i: (0, 0)),
        ],
        out_specs=[
            pl.BlockSpec((tm, D_Q_C), lambda i: (i, 0)),
            pl.BlockSpec((tm, D_KV_C), lambda i: (i, 0)),
            pl.BlockSpec((1, N_KV_HEADS_C, V_ROWS, tm), lambda i: (i, 0, 0, 0)),
        ],
        out_shape=[
            jax.ShapeDtypeStruct((n, D_Q_C), BF16),
            jax.ShapeDtypeStruct((n, D_KV_C), BF16),
            jax.ShapeDtypeStruct((n // tm, N_KV_HEADS_C, V_ROWS, tm), BF16),
        ],
        compiler_params=_cparams("parallel"),
        name="attn_prep",
    )(z, z, cos2, sin2, qn_l, kn_l)


def _attn_kernel(q_ref, k_ref, vt_ref, o_ref, s0_scr, sa_scr, sb_scr, acc_scr, *, lc, tq, tk, n_super):
    nb = lc // tk

    def scores(row0, nrows, s_scr):
        kb = k_ref[pl.ds(row0, nrows), :]
        for g in range(GROUP_C):
            s_scr[g] = _dot_nt(kb, q_ref[:, g * HEAD_DIM_C:(g + 1) * HEAD_DIM_C])

    def softmax_pv(s_scr, blk0, n_blk, ms):
        vts = [vt_ref[blk0 + b, 0] for b in range(n_blk)]
        new_m = []
        for g in range(GROUP_C):
            st = s_scr[g]
            m_blk = jnp.max(st, axis=0, keepdims=True)
            m_new = m_blk if ms is None else jnp.maximum(ms[g], m_blk)
            p = jnp.exp2(st - m_new).astype(BF16)
            pv = _dot(vts[0], p[0:tk])
            for b in range(1, n_blk):
                pv = pv + _dot(vts[b], p[b * tk:(b + 1) * tk])
            acc_scr[g] = pv if ms is None else jnp.exp2(ms[g] - m_new) * acc_scr[g] + pv
            new_m.append(m_new)
        return tuple(new_m)

    def super_rows(j):
        return pl.multiple_of(lc + j * (2 * tk), tk)

    scores(0, lc, s0_scr)
    if n_super:
        def body(i, ms):
            j = 2 * i
            scores(super_rows(j + 1), 2 * tk, sb_scr)
            ms = softmax_pv(sa_scr, nb + 2 * j, 2, ms)
            scores(super_rows(j + 2), 2 * tk, sa_scr)
            return softmax_pv(sb_scr, nb + 2 * j + 2, 2, ms)

        scores(super_rows(0), 2 * tk, sa_scr)
        ms = softmax_pv(s0_scr, 0, nb, None)
        ms = lax.fori_loop(0, n_super // 2 - 1, body, ms)
        scores(super_rows(n_super - 1), 2 * tk, sb_scr)
        ms = softmax_pv(sa_scr, nb + 2 * (n_super - 2), 2, ms)
        softmax_pv(sb_scr, nb + 2 * (n_super - 1), 2, ms)
    else:
        softmax_pv(s0_scr, 0, nb, None)

    for g in range(GROUP_C):
        acc = acc_scr[g]
        o_ref[:, g * HEAD_DIM_C:(g + 1) * HEAD_DIM_C] = (
            acc[0:HEAD_DIM_C] / acc[HEAD_DIM_C:HEAD_DIM_C + 1]).T


def _attention(qh, kh, vt, lc, latent):
    n = qh.shape[0]
    tq = tk = ATTN_TILE
    assert lc % tq == 0 and n % tq == 0
    if latent:
        n_keys, n_q, q_off = n, n - lc, lc // tq
        n_super = (n - lc) // (2 * tk)
        assert n_super * 2 * tk == n - lc and n_super % 2 == 0
        super_shape = (GROUP_C, 2 * tk, tq)
    else:
        n_keys, n_q, q_off, n_super = lc, lc, 0, 0
        super_shape = (1, SUBLANES, LANES)
    gw = GROUP_C * HEAD_DIM_C
    return pl.pallas_call(
        functools.partial(_attn_kernel, lc=lc, tq=tq, tk=tk, n_super=n_super),
        grid=(N_KV_HEADS_C, n_q // tq),
        in_specs=[
            pl.BlockSpec((tq, gw), lambda j, i: (i + q_off, j)),
            pl.BlockSpec((n_keys, HEAD_DIM_C), lambda j, i: (0, j)),
            pl.BlockSpec((n_keys // tk, 1, V_ROWS, tk), lambda j, i: (0, j, 0, 0)),
        ],
        out_specs=pl.BlockSpec((tq, gw), lambda j, i: (i, j)),
        out_shape=jax.ShapeDtypeStruct((n_q, D_Q_C), F32),
        scratch_shapes=[pltpu.VMEM((GROUP_C, lc, tq), F32),
                        pltpu.VMEM(super_shape, F32), pltpu.VMEM(super_shape, F32),
                        pltpu.VMEM((GROUP_C, V_ROWS, tq), F32)],
        compiler_params=_cparams("parallel", "parallel"),
        name="attention" if latent else "attention_ctx",
    )(qh, kh, vt)


def _conv4(x, prev8, next8, w_ref, prev_ok, next_ok):
    tm = x.shape[0]
    row = lax.broadcasted_iota(jnp.int32, (tm, 1), 0)
    p = jnp.where(prev_ok, prev8, 0.0)
    nx = jnp.where(next_ok, next8, 0.0)
    xm1 = jnp.where(row == 0, p[7:8], pltpu.roll(x, 1, 0))
    xm2 = jnp.where(row == 0, p[6:7], jnp.where(row == 1, p[7:8], pltpu.roll(x, 2, 0)))
    xp1 = jnp.where(row == tm - 1, nx[0:1], pltpu.roll(x, tm - 1, 0))
    return w_ref[0:1] * xm2 + w_ref[1:2] * xm1 + w_ref[2:3] * x + w_ref[3:4] * xp1


def _seg_flags(t, nbc, nt):
    prev_ok = jnp.logical_and(t != 0, t != nbc)
    next_ok = jnp.logical_and(t != nbc - 1, t != nt - 1)
    return prev_ok, next_ok


def _halo_maps(tile_of, tm, n, col_block):
    r = tm // SUBLANES
    last = n // SUBLANES - 1
    main = lambda i: (tile_of(i), col_block)
    prev = lambda i: (jnp.maximum(tile_of(i) * r - 1, 0), col_block)
    nxt = lambda i: (jnp.minimum((tile_of(i) + 1) * r, last), col_block)
    return main, prev, nxt


def _rglru_kernel(xf_ref, pf_ref, nf_ref, xb_ref, pb_ref, nb_ref, cw_ref, cb_ref, wg_ref, bg_ref,
                  lam_ref, of_ref, ob_ref, x_scr, hcar, *, tm, nbc, nt):
    i = pl.program_id(0)
    seg = tm // SUBLANES
    ncg = D_A // LANES
    sub = lax.broadcasted_iota(jnp.int32, (SUBLANES, D_A), 0)

    @pl.when(i == 0)
    def _():
        hcar[...] = jnp.zeros_like(hcar)

    tiles = (i, jnp.where(i < nbc, nbc - 1 - i, nt - 1 - (i - nbc)))
    refs = ((xf_ref, pf_ref, nf_ref, of_ref), (xb_ref, pb_ref, nb_ref, ob_ref))
    for d in range(2):
        x_ref, p_ref, n_ref, o_ref = refs[d]
        prev_ok, next_ok = _seg_flags(tiles[d], nbc, nt)
        p = jnp.where(prev_ok, p_ref[...], 0.0)
        nx = jnp.where(next_ok, n_ref[...], 0.0)
        for cg in range(ncg):
            x_scr[cg] = x_ref[:, cg * LANES:(cg + 1) * LANES]
        xg = [jnp.concatenate([x_scr[cg, pl.ds(j, SUBLANES, stride=seg), :] for cg in range(ncg)], axis=1)
              for j in range(seg)]
        before1 = jnp.where(sub == 0, p[7:8], pltpu.roll(xg[seg - 1], 1, 0))
        before2 = jnp.where(sub == 0, p[6:7], pltpu.roll(xg[seg - 2], 1, 0))
        after1 = jnp.where(sub == SUBLANES - 1, nx[0:1], pltpu.roll(xg[0], SUBLANES - 1, 0))
        xp = jnp.concatenate(xg, axis=0)
        xm1 = jnp.concatenate([before1] + xg[:seg - 1], axis=0)
        xm2 = jnp.concatenate([before2, before1] + xg[:seg - 2], axis=0)
        xp1 = jnp.concatenate(xg[1:] + [after1], axis=0)
        u = (cw_ref[0:1] * xm2 + cw_ref[1:2] * xm1 + cw_ref[2:3] * xp + cw_ref[3:4] * xp1) + cb_ref[...]
        gates = _dot(u.astype(BF16), wg_ref[d]) + bg_ref[d]
        gate_r = jax.nn.sigmoid(gates[:, 0:D_A])
        gate_i = jax.nn.sigmoid(gates[:, D_A:2 * D_A])
        log_a = (-LRU_C) * gate_r * _softplus(-lam_ref[d:d + 1, :])
        a = jnp.exp(log_a)
        b = jnp.sqrt(-jnp.tanh(log_a) * (a * a + 1.0)) * (gate_i * u)

        order = list(range(seg)) if d == 0 else list(range(seg - 1, -1, -1))
        hloc = jnp.zeros((SUBLANES, D_A), F32)
        pcum = jnp.ones((SUBLANES, D_A), F32)
        hs, ps = [None] * seg, [None] * seg
        for j in order:
            aj = a[j * SUBLANES:(j + 1) * SUBLANES]
            hloc = aj * hloc + b[j * SUBLANES:(j + 1) * SUBLANES]
            pcum = aj * pcum
            hs[j], ps[j] = hloc, pcum
        carry = hcar[d:d + 1, :]
        seg_in = [None] * SUBLANES
        for s in (range(SUBLANES) if d == 0 else range(SUBLANES - 1, -1, -1)):
            seg_in[s] = carry
            carry = pcum[s:s + 1] * carry + hloc[s:s + 1]
        hcar[d:d + 1, :] = carry
        cin = jnp.concatenate(seg_in, axis=0)
        for j in range(seg):
            hj = hs[j] + ps[j] * cin
            for cg in range(ncg):
                o_ref[cg, pl.ds(j, SUBLANES, stride=seg), :] = hj[:, cg * LANES:(cg + 1) * LANES]


def _rglru(z, cw, cb, wg, bg, lam, lc):
    n = z.shape[0]
    tm = 256
    assert lc % tm == 0 and n % tm == 0
    nbc, nt = lc // tm, n // tm
    fwd = lambda i: i
    bwd = lambda i: jnp.where(i < nbc, nbc - 1 - i, nt - 1 - (i - nbc))
    cblk = COL_XA // D_A
    fm, fp, fn = _halo_maps(fwd, tm, n, cblk)
    bm, bp, bn = _halo_maps(bwd, tm, n, cblk)
    const2 = lambda i: (0, 0)
    const3 = lambda i: (0, 0, 0)
    return pl.pallas_call(
        functools.partial(_rglru_kernel, tm=tm, nbc=nbc, nt=nt),
        grid=(nt,),
        in_specs=[
            pl.BlockSpec((tm, D_A), fm), pl.BlockSpec((SUBLANES, D_A), fp), pl.BlockSpec((SUBLANES, D_A), fn),
            pl.BlockSpec((tm, D_A), bm), pl.BlockSpec((SUBLANES, D_A), bp), pl.BlockSpec((SUBLANES, D_A), bn),
            pl.BlockSpec((4, D_A), const2), pl.BlockSpec((1, D_A), const2),
            pl.BlockSpec((2, D_A, 2 * D_A), const3), pl.BlockSpec((2, 1, 2 * D_A), const3),
            pl.BlockSpec((2, D_A), const2),
        ],
        out_specs=[pl.BlockSpec((D_A // LANES, tm, LANES), lambda i: (0, fwd(i), 0)),
                   pl.BlockSpec((D_A // LANES, tm, LANES), lambda i: (0, bwd(i), 0))],
        out_shape=[jax.ShapeDtypeStruct((D_A // LANES, n, LANES), F32)] * 2,
        scratch_shapes=[pltpu.VMEM((D_A // LANES, tm, LANES), F32), pltpu.VMEM((SUBLANES, D_A), F32)],
        compiler_params=_cparams("arbitrary"),
        name="rglru",
    )(z, z, z, z, z, z, cw, cb, wg, bg, lam)


def _gdn_prep_kernel(x_ref, p_ref, n_ref, ba_ref, cw_ref, av_ref, dt_ref, q_ref, k_ref, v_ref, bg_ref,
                     *, nbc, nt):
    prev_ok, next_ok = _seg_flags(pl.program_id(0), nbc, nt)
    u = _silu(_conv4(x_ref[...], p_ref[...], n_ref[...], cw_ref, prev_ok, next_ok))
    for hd in range(N_HEADS_B):
        sl = slice(hd * HEAD_DIM_B, (hd + 1) * HEAD_DIM_B)
        qv = u[:, hd * HEAD_DIM_B:(hd + 1) * HEAD_DIM_B]
        kv = u[:, D_B + hd * HEAD_DIM_B:D_B + (hd + 1) * HEAD_DIM_B]
        q_ref[:, sl] = qv * lax.rsqrt(jnp.sum(qv * qv, axis=-1, keepdims=True) + EPS)
        k_ref[:, sl] = kv * lax.rsqrt(jnp.sum(kv * kv, axis=-1, keepdims=True) + EPS)
    v_ref[...] = u[:, 2 * D_B:3 * D_B]
    ba = ba_ref[...]
    lane = lax.broadcasted_iota(jnp.int32, ba.shape, 1)
    beta = jax.nn.sigmoid(ba)
    g = -jnp.exp(av_ref[...]) * _softplus(ba + dt_ref[...])
    bg_ref[...] = jnp.where(lane < 2 * N_HEADS_B, beta, jnp.where(lane < 4 * N_HEADS_B, g, 0.0))


def _gdn_prep(z, cw, avec, dtvec, lc):
    n = z.shape[0]
    tm = 256
    nbc, nt = lc // tm, n // tm
    w3 = 3 * D_B
    assert COL_QKVB % w3 == 0
    m, p, nx = _halo_maps(lambda i: i, tm, n, COL_QKVB // w3)
    const2 = lambda i: (0, 0)
    return pl.pallas_call(
        functools.partial(_gdn_prep_kernel, nbc=nbc, nt=nt),
        grid=(nt,),
        in_specs=[
            pl.BlockSpec((tm, w3), m), pl.BlockSpec((SUBLANES, w3), p), pl.BlockSpec((SUBLANES, w3), nx),
            pl.BlockSpec((tm, LANES), lambda i: (i, COL_BA // LANES)),
            pl.BlockSpec((4, w3), const2), pl.BlockSpec((1, LANES), const2), pl.BlockSpec((1, LANES), const2),
        ],
        out_specs=[pl.BlockSpec((tm, D_B), lambda i: (i, 0))] * 3 + [pl.BlockSpec((tm, LANES), lambda i: (i, 0))],
        out_shape=[jax.ShapeDtypeStruct((n, D_B), F32)] * 3 + [jax.ShapeDtypeStruct((n, LANES), F32)],
        compiler_params=_cparams("parallel"),
        name="gdn_prep",
    )(z, z, z, z, cw, avec, dtvec)


def _unit_tri_inverse(mats, row, col, lane_lo):
    def block_diag(y):
        return jnp.concatenate([jnp.where(lane_lo, y, 0.0), jnp.where(lane_lo, 0.0, y)], axis=0).astype(BF16)

    def mm(xs, ys):
        return [_dot(x.astype(BF16), block_diag(y)) for x, y in zip(xs, ys)]

    def same(shift):
        return (row >> shift) == (col >> shift)

    eye = jnp.where(row == col, 1.0, 0.0)
    m8 = [jnp.where(same(3), m, 0.0) for m in mats]
    xs = [eye - m for m in m8]
    pw = mm(m8, m8)
    xs = [x + y for x, y in zip(xs, mm(xs, pw))]
    pw = mm(pw, pw)
    xs = [x + y for x, y in zip(xs, mm(xs, pw))]
    for shift in (4, 5, 6):
        off = [jnp.where(same(shift), jnp.where(same(shift - 1), 0.0, m), 0.0) for m in mats]
        xs = [x - y for x, y in zip(xs, mm(xs, mm(off, xs)))]
    return xs


def _gdn_chunk_kernel(q_ref, k_ref, v_ref, bg_ref, e64_ref, eb_ref, eg_ref, wq_ref, u_ref, l2_ref, egl_ref):
    cs = CHUNK_B
    nh = N_HEADS_B
    ri = lax.broadcasted_iota(jnp.int32, (cs, cs), 0)
    ci = lax.broadcasted_iota(jnp.int32, (cs, cs), 1)
    ones = jnp.ones((cs, cs), BF16)
    row = lax.broadcasted_iota(jnp.int32, (cs, nh * cs), 0)
    col = lax.broadcasted_iota(jnp.int32, (cs, nh * cs), 1) & (cs - 1)
    lane_lo = lax.broadcasted_iota(jnp.int32, (cs, LANES), 1) < cs
    lane_lo2 = lax.broadcasted_iota(jnp.int32, (2 * cs, LANES), 1) < cs
    zeros = jnp.zeros((cs, HEAD_DIM_B), F32)
    tri = (jnp.where(ci <= ri, 1.0, 0.0).astype(BF16), jnp.where(ci >= ri, 1.0, 0.0).astype(BF16))
    strict = (row > col, row < col)
    causal = (row >= col, row <= col)

    def dot_parts(lhs, parts):
        return _dot(lhs, parts[0]) + (_dot(lhs, parts[1]) + _dot(lhs, parts[2]))

    stage = {}
    mats = []
    for c in range(GDN_CHUNKS_PER_STEP):
        rows = slice(c * cs, (c + 1) * cs)
        qs = q_ref[rows, :] * (HEAD_DIM_B ** -0.5)
        kn = k_ref[rows, :]
        v = v_ref[rows, :]
        bg_parts = _split3(bg_ref[rows, :])
        for d in range(2):
            expand = lambda e: [_dot(part, e).astype(BF16) for part in bg_parts]
            g64 = [jnp.where(strict[d], part, 0.0).astype(BF16) for part in expand(e64_ref[d])]
            decay = jnp.where(causal[d], jnp.exp(dot_parts(tri[d], g64)), 0.0)
            beta_parts = expand(eb_ref[d])
            beta = beta_parts[0].astype(F32) + (beta_parts[1].astype(F32) + beta_parts[2].astype(F32))
            g_parts = expand(eg_ref[d])
            gc = dot_parts(tri[d], g_parts)
            gl = dot_parts(ones, g_parts)
            egc = jnp.exp(gc)
            kb = kn * beta
            lhs = jnp.concatenate([kb, qs], axis=0).astype(BF16)
            prods = []
            for pr in range(2):
                ka = kn[:, (2 * pr) * HEAD_DIM_B:(2 * pr + 1) * HEAD_DIM_B]
                kb_ = kn[:, (2 * pr + 1) * HEAD_DIM_B:(2 * pr + 2) * HEAD_DIM_B]
                rhs_t = jnp.concatenate([jnp.concatenate([ka, zeros], axis=1),
                                         jnp.concatenate([zeros, kb_], axis=1)], axis=0).astype(BF16)
                prods.append(_dot_nt(lhs[:, 2 * pr * HEAD_DIM_B:(2 * pr + 2) * HEAD_DIM_B], rhs_t))
            prod = jnp.concatenate(prods, axis=1)
            mat = jnp.where(strict[d], prod[0:cs] * decay, 0.0)
            mats += [mat[:, 0:LANES], mat[:, LANES:2 * LANES]]
            stage[c, d] = dict(qk=prod[cs:2 * cs] * decay, vb=v * beta, kbe=kb * egc, qhead=qs * egc,
                               ktail=kn * jnp.exp(gl - gc), egl=jnp.exp(gl))

    row2 = lax.broadcasted_iota(jnp.int32, (cs, LANES), 0)
    col2 = lax.broadcasted_iota(jnp.int32, (cs, LANES), 1) & (cs - 1)
    tmats = _unit_tri_inverse(mats, row2, col2, lane_lo)

    for c in range(GDN_CHUNKS_PER_STEP):
        for d in range(2):
            st = stage[c, d]
            for pr in range(2):
                ha, hb = 2 * pr, 2 * pr + 1
                sa = slice(ha * HEAD_DIM_B, (ha + 1) * HEAD_DIM_B)
                sb = slice(hb * HEAD_DIM_B, (hb + 1) * HEAD_DIM_B)
                tp = tmats[(c * 2 + d) * 2 + pr]
                rhs = jnp.concatenate([jnp.concatenate([st["vb"][:, sa], st["kbe"][:, sa]], axis=1),
                                       jnp.concatenate([st["vb"][:, sb], st["kbe"][:, sb]], axis=1)],
                                      axis=0).astype(BF16)
                ra = _dot(jnp.where(lane_lo, tp, 0.0).astype(BF16), rhs)
                rb = _dot(jnp.where(lane_lo, 0.0, tp).astype(BF16), rhs)
                for hh, res, sl in ((ha, ra, sa), (hb, rb, sb)):
                    u_ref[c, d * nh + hh] = res[:, 0:HEAD_DIM_B]
                    wq_ref[c, d * nh + hh] = jnp.concatenate(
                        [res[:, HEAD_DIM_B:2 * HEAD_DIM_B], st["qhead"][:, sl]], axis=0).astype(BF16)
                kt_t = jnp.concatenate([st["ktail"][:, sa], st["ktail"][:, sb]], axis=0).T
                qkp = st["qk"][:, pr * LANES:(pr + 1) * LANES]
                l2_ref[c, d * 2 + pr] = jnp.concatenate(
                    [jnp.where(lane_lo, qkp, 0.0), jnp.where(lane_lo, 0.0, qkp),
                     jnp.where(lane_lo2, kt_t, 0.0), jnp.where(lane_lo2, 0.0, kt_t)], axis=0).astype(BF16)
            egl_ref[c, d] = jnp.concatenate(
                [st["egl"][0:1, hh * HEAD_DIM_B:(hh + 1) * HEAD_DIM_B] for hh in range(nh)]
                + [jnp.zeros((SUBLANES - nh, HEAD_DIM_B), F32)], axis=0)


def _gdn_expanders():
    nh, cs = N_HEADS_B, CHUNK_B
    e64 = np.zeros((2, LANES, nh * cs), np.float32)
    eb = np.zeros((2, LANES, nh * HEAD_DIM_B), np.float32)
    eg = np.zeros((2, LANES, nh * HEAD_DIM_B), np.float32)
    for d in range(2):
        for hh in range(nh):
            p = d * nh + hh
            e64[d, 2 * nh + p, hh * cs:(hh + 1) * cs] = 1.0
            eb[d, p, hh * HEAD_DIM_B:(hh + 1) * HEAD_DIM_B] = 1.0
            eg[d, 2 * nh + p, hh * HEAD_DIM_B:(hh + 1) * HEAD_DIM_B] = 1.0
    return jnp.asarray(e64, BF16), jnp.asarray(eb, BF16), jnp.asarray(eg, BF16)


def _gdn_chunk(qn, kn, v, bg, expanders):
    n = qn.shape[0]
    cs, nh = CHUNK_B, N_HEADS_B
    nc = n // cs
    g = GDN_CHUNKS_PER_STEP
    assert nc % g == 0
    e64, eb, eg = expanders
    const3 = lambda c: (0, 0, 0)
    return pl.pallas_call(
        _gdn_chunk_kernel,
        grid=(nc // g,),
        in_specs=[
            pl.BlockSpec((g * cs, D_B), lambda c: (c, 0)), pl.BlockSpec((g * cs, D_B), lambda c: (c, 0)),
            pl.BlockSpec((g * cs, D_B), lambda c: (c, 0)), pl.BlockSpec((g * cs, LANES), lambda c: (c, 0)),
            pl.BlockSpec(e64.shape, const3), pl.BlockSpec(eb.shape, const3), pl.BlockSpec(eg.shape, const3),
        ],
        out_specs=[
            pl.BlockSpec((g, 2 * nh, 2 * cs, HEAD_DIM_B), lambda c: (c, 0, 0, 0)),
            pl.BlockSpec((g, 2 * nh, cs, HEAD_DIM_B), lambda c: (c, 0, 0, 0)),
            pl.BlockSpec((g, 4, 6 * cs, LANES), lambda c: (c, 0, 0, 0)),
            pl.BlockSpec((g, 2, SUBLANES, HEAD_DIM_B), lambda c: (c, 0, 0, 0)),
        ],
        out_shape=[
            jax.ShapeDtypeStruct((nc, 2 * nh, 2 * cs, HEAD_DIM_B), BF16),
            jax.ShapeDtypeStruct((nc, 2 * nh, cs, HEAD_DIM_B), F32),
            jax.ShapeDtypeStruct((nc, 4, 6 * cs, LANES), BF16),
            jax.ShapeDtypeStruct((nc, 2, SUBLANES, HEAD_DIM_B), F32),
        ],
        compiler_params=_cparams("parallel"),
        name="gdn_chunk",
    )(qn, kn, v, bg, e64, eb, eg)


def _gdn_rec_kernel(wqf_ref, uf_ref, l2f_ref, egf_ref, wqb_ref, ub_ref, l2b_ref, egb_ref,
                    of_ref, ob_ref, s_scr):
    cs, nh = CHUNK_B, N_HEADS_B

    @pl.when(pl.program_id(0) == 0)
    def _():
        s_scr[...] = jnp.zeros_like(s_scr)

    dirs = ((wqf_ref, uf_ref, l2f_ref, egf_ref, of_ref), (wqb_ref, ub_ref, l2b_ref, egb_ref, ob_ref))
    for step in range(GDN_REC_CHUNKS_PER_STEP):
        for d in range(2):
            wq_ref, u_ref, l2_ref, eg_ref, o_ref = dirs[d]
            c = step if d == 0 else GDN_REC_CHUNKS_PER_STEP - 1 - step
            for pr in range(2):
                v_new, o_state = [], []
                for e in range(2):
                    hh = 2 * pr + e
                    r = _dot(wq_ref[c, hh], s_scr[d * nh + hh].astype(BF16))
                    v_new.append(u_ref[c, hh] - r[0:cs])
                    o_state.append(r[cs:2 * cs])
                r2 = _dot(l2_ref[c, pr], jnp.concatenate(v_new, axis=0).astype(BF16))
                for e in range(2):
                    hh = 2 * pr + e
                    o_ref[c * cs:(c + 1) * cs, hh * HEAD_DIM_B:(hh + 1) * HEAD_DIM_B] = (
                        o_state[e] + r2[e * cs:(e + 1) * cs])
                    s_scr[d * nh + hh] = (s_scr[d * nh + hh] * eg_ref[c, 0, hh:hh + 1, :]
                                          + r2[2 * cs + e * 2 * cs:2 * cs + (e + 1) * 2 * cs])


def _gdn_rec(wq, u, l2, egl, lc):
    cs, nh = CHUNK_B, N_HEADS_B
    g = GDN_REC_CHUNKS_PER_STEP
    assert wq.shape[0] % g == 0 and (lc // cs) % g == 0
    nc = wq.shape[0] // g
    ncc = lc // cs // g
    fwd = lambda i: i
    bwd = lambda i: jnp.where(i < ncc, ncc - 1 - i, nc - 1 - (i - ncc))
    specs = []
    for d, order in ((0, fwd), (1, bwd)):
        specs += [
            pl.BlockSpec((g, nh, 2 * cs, HEAD_DIM_B), lambda i, d=d, order=order: (order(i), d, 0, 0)),
            pl.BlockSpec((g, nh, cs, HEAD_DIM_B), lambda i, d=d, order=order: (order(i), d, 0, 0)),
            pl.BlockSpec((g, 2, 6 * cs, LANES), lambda i, d=d, order=order: (order(i), d, 0, 0)),
            pl.BlockSpec((g, 1, SUBLANES, HEAD_DIM_B), lambda i, d=d, order=order: (order(i), d, 0, 0)),
        ]
    return pl.pallas_call(
        _gdn_rec_kernel,
        grid=(nc,),
        in_specs=specs,
        out_specs=[pl.BlockSpec((g * cs, D_B), lambda i: (fwd(i), 0)),
                   pl.BlockSpec((g * cs, D_B), lambda i: (bwd(i), 0))],
        out_shape=[jax.ShapeDtypeStruct((nc * g * cs, D_B), F32)] * 2,
        scratch_shapes=[pltpu.VMEM((2 * nh, HEAD_DIM_B, HEAD_DIM_B), F32)],
        compiler_params=_cparams("arbitrary"),
        name="gdn_rec",
    )(wq, u, l2, egl, wq, u, l2, egl)


def _outproj_kernel(yaf_ref, yab_ref, obf_ref, obb_ref, octx_ref, olat_ref, ga_ref, gb_ref, gc_ref, h_ref,
                    mod_ref, on_ref, w_ref, gn_ref, modn_ref, *out_refs, lc, tm, row_off, last):
    half = tm // 2
    for part in range(2):
        r = slice(part * half, (part + 1) * half)
        row0 = (pl.program_id(0) + row_off) * tm + part * half
        ya = jnp.concatenate([yaf_ref[cg, r, :] + yab_ref[cg, r, :] for cg in range(D_A // LANES)], axis=1)
        mix_a = ya * _silu(ga_ref[r, :])
        ob = obf_ref[r, :] + obb_ref[r, :]
        gb = _silu(gb_ref[r, :])
        mix_b = []
        for hd in range(N_HEADS_B):
            sl = slice(hd * HEAD_DIM_B, (hd + 1) * HEAD_DIM_B)
            x = ob[:, sl]
            mix_b.append(x * lax.rsqrt(jnp.mean(x * x, axis=-1, keepdims=True) + EPS) * on_ref[...] * gb[:, sl])
        oc = olat_ref[r, :] if last else jnp.where(row0 < lc, octx_ref[r, :], olat_ref[r, :])
        mix_c = oc * _silu(gc_ref[r, :])
        mix = jnp.concatenate([mix_a] + mix_b + [mix_c], axis=1).astype(BF16)
        row = row0 + lax.broadcasted_iota(jnp.int32, (half, 1), 0)
        gate = jnp.where(row < lc, mod_ref[1:2, 2 * D_MODEL:3 * D_MODEL], mod_ref[0:1, 2 * D_MODEL:3 * D_MODEL])
        h_new = h_ref[r, :] + gate * _dot(mix, w_ref[0])
        if last:
            out_refs[0][r, :] = (h_new * lax.rsqrt(jnp.mean(h_new * h_new, axis=-1, keepdims=True) + EPS)
                                 * gn_ref[...])
        else:
            out_refs[0][r, :] = h_new
            out_refs[1][r, :] = _adaln(h_new, row0, gn_ref, modn_ref, lc)


def _outproj(ya_f, ya_b, ob_f, ob_b, oc_ctx, oc_lat, z, h, mod_l, onorm_l, w_out_r, layer, gain_next,
             mod_next, lc, last):
    n = h.shape[0]
    tm = ATTN_TILE
    assert lc % tm == 0
    n_ctx = lc // tm
    row_off = n_ctx if last else 0
    rows = lambda width, col: pl.BlockSpec((tm, width), lambda i: (i + row_off, col // width))
    const2 = lambda i: (0, 0)
    out_block = pl.BlockSpec((tm, D_MODEL), lambda i: (i, 0))
    if last:
        out_specs = [out_block]
        out_shape = [jax.ShapeDtypeStruct((n - lc, D_MODEL), F32)]
    else:
        out_specs = [out_block, out_block]
        out_shape = [jax.ShapeDtypeStruct((n, D_MODEL), F32), jax.ShapeDtypeStruct((n, D_MODEL), BF16)]
    return pl.pallas_call(
        functools.partial(_outproj_kernel, lc=lc, tm=tm, row_off=row_off, last=last),
        grid=((n - row_off * tm) // tm,),
        in_specs=[
            pl.BlockSpec((D_A // LANES, tm, LANES), lambda i: (0, i + row_off, 0)),
            pl.BlockSpec((D_A // LANES, tm, LANES), lambda i: (0, i + row_off, 0)),
            rows(D_B, 0), rows(D_B, 0),
            pl.BlockSpec((tm, D_Q_C), lambda i: (jnp.minimum(i + row_off, n_ctx - 1), 0)),
            pl.BlockSpec((tm, D_Q_C), lambda i: (jnp.maximum(i + row_off - n_ctx, 0), 0)),
            rows(D_A, COL_GA), rows(D_B, COL_GB), rows(D_Q_C, COL_GC), rows(D_MODEL, 0),
            pl.BlockSpec((SUBLANES, 3 * D_MODEL), const2), pl.BlockSpec((1, HEAD_DIM_B), const2),
            pl.BlockSpec((1, D_MIX, D_MODEL), lambda i: (layer, 0, 0)),
            pl.BlockSpec((1, D_MODEL), const2), pl.BlockSpec((SUBLANES, 3 * D_MODEL), const2),
        ],
        out_specs=out_specs,
        out_shape=out_shape,
        compiler_params=_cparams("parallel"),
        name="outproj_last" if last else "outproj",
    )(ya_f, ya_b, ob_f, ob_b, oc_ctx, oc_lat, z, z, z, h, mod_l, onorm_l, w_out_r, gain_next, mod_next)


def _rope_tables(t_len, lc):
    rows = t_len // GRID_W
    row = jnp.repeat(jnp.arange(rows, dtype=jnp.int32), GRID_W)
    col = jnp.tile(jnp.arange(GRID_W, dtype=jnp.int32), rows)
    n_freq = HEAD_DIM_C // 4
    inv_freq = ROPE_BASE ** (-jnp.arange(n_freq, dtype=F32) / n_freq)
    ang = jnp.concatenate([row.astype(F32)[:, None] * inv_freq, col.astype(F32)[:, None] * inv_freq], axis=-1)
    cos, sin = jnp.cos(ang), jnp.sin(ang)
    half = HEAD_DIM_C // 2
    cos2 = jnp.concatenate([jnp.ones((lc, 2 * half), F32), jnp.concatenate([cos, cos], axis=-1)], axis=0)
    sin2 = jnp.concatenate([jnp.zeros((lc, 2 * half), F32), jnp.concatenate([-sin, sin], axis=-1)], axis=0)
    return cos2, sin2


def _relayout_w_in_kernel(w_ref, o_ref):
    n_ba = 4 * N_HEADS_B
    src_ba = 2 * D_A + 4 * D_B
    src_qc = src_ba + n_ba
    copies = ((0, COL_XA, D_A), (D_A, COL_GA, D_A), (2 * D_A, COL_QKVB, 3 * D_B),
              (2 * D_A + 3 * D_B, COL_GB, D_B), (src_ba, COL_BA, n_ba),
              (src_qc, COL_QC, D_Q_C), (src_qc + D_Q_C, COL_KV, 2 * D_KV_C),
              (src_qc + D_Q_C + 2 * D_KV_C, COL_GC, D_Q_C))
    for src, dst, rows in copies:
        o_ref[0, dst:dst + rows, :] = w_ref[0, src:src + rows, :].astype(BF16)
    o_ref[0, COL_BA + n_ba:D_Z, :] = jnp.zeros((D_Z - COL_BA - n_ba, o_ref.shape[2]), BF16)


def _relayout_w_in(w_in):
    depth, d, d_in = w_in.shape
    tc = 512
    return pl.pallas_call(
        _relayout_w_in_kernel,
        grid=(depth, d // tc),
        in_specs=[pl.BlockSpec((1, d_in, tc), lambda l, i: (l, 0, i))],
        out_specs=pl.BlockSpec((1, D_Z, tc), lambda l, i: (l, 0, i)),
        out_shape=jax.ShapeDtypeStruct((depth, D_Z, d), BF16),
        compiler_params=_cparams("parallel", "parallel"),
        name="relayout_w_in",
    )(jnp.swapaxes(w_in, 1, 2))


def _block_diag(w):
    eye = jnp.eye(N_BLK_A, dtype=w.dtype)
    out = jnp.einsum('...nij,nm->...nimj', w, eye)
    return out.reshape(w.shape[:-3] + (D_A, D_A))


def kernel(x, c, ctx, c_ctx, norm_g, w_mod, b_mod, w_in, conv_a_w, conv_a_b, w_ra, b_ra, w_ia, b_ia, lam_a,
           conv_b_w, a_log_b, dt_bias_b, onorm_b, qn_c, kn_c, w_out, final_g):
    bsz, t_len, d = x.shape
    lc = ctx.shape[1]
    depth = w_in.shape[0]
    assert bsz == 1 and d == D_MODEL and t_len % GRID_W == 0

    h = jnp.concatenate([ctx[0], x[0]], axis=0)
    cc = jnp.concatenate([c, c_ctx[None, :], jnp.zeros((SUBLANES - 2, d), F32)], axis=0)
    mod = _modulation(cc, w_mod, b_mod)

    w_in_r = _relayout_w_in(w_in)
    w_out_r = w_out.astype(BF16)
    wg = jnp.concatenate([_block_diag(w_ra), _block_diag(w_ia)], axis=-1).astype(BF16)
    bgate = jnp.concatenate([b_ra, b_ia], axis=-1)[:, :, None, :]
    nh2 = 2 * N_HEADS_B
    lane_pad = lambda v: jnp.pad(v.reshape(depth, 1, nh2), ((0, 0), (0, 0), (nh2, LANES - 2 * nh2)))
    avec = lane_pad(a_log_b)
    dtvec = lane_pad(dt_bias_b)
    cos2, sin2 = _rope_tables(t_len, lc)
    expanders = _gdn_expanders()

    y = _prenorm(h, mod[0], norm_g[0][None, :], lc)
    for l in range(depth):
        last = l == depth - 1
        z = _inproj(y, w_in_r, l)
        qh, kh, vt = _attn_prep(z, cos2, sin2, qn_c[l][None, :], kn_c[l][None, :])
        oc_lat = _attention(qh, kh, vt, lc, latent=True)
        oc_ctx = oc_lat if last else _attention(qh, kh, vt, lc, latent=False)
        ya_f, ya_b = _rglru(z, conv_a_w[l], conv_a_b[l][None, :], wg[l], bgate[l], lam_a[l], lc)
        qn, kn, v, bg = _gdn_prep(z, conv_b_w[l], avec[l], dtvec[l], lc)
        wq, u, l2, egl = _gdn_chunk(qn, kn, v, bg, expanders)
        ob_f, ob_b = _gdn_rec(wq, u, l2, egl, lc)
        gain_next = final_g[None, :] if last else norm_g[l + 1][None, :]
        mod_next = mod[l] if last else mod[l + 1]
        outs = _outproj(ya_f, ya_b, ob_f, ob_b, oc_ctx, oc_lat, z, h, mod[l], onorm_b[l][None, :], w_out_r, l,
                        gain_next, mod_next, lc, last)
        if last:
            return outs[0][None]
        h, y = outs
```

```python
import functools
import math

import numpy as np
import jax
import jax.numpy as jnp
from jax import lax
from jax.experimental import pallas as pl
from jax.experimental.pallas import tpu as pltpu

F32 = jnp.float32
BF16 = jnp.bfloat16

D_MODEL = 2048
GRID_W = 64
EPS = 1e-6
ROPE_BASE = 10000.0

D_A = 512
N_BLK_A = 8
BLK_A = D_A // N_BLK_A
LRU_C = 8.0

N_HEADS_B = 4
HEAD_DIM_B = 128
D_B = N_HEADS_B * HEAD_DIM_B
CHUNK_B = 64

N_Q_HEADS_C = 8
N_KV_HEADS_C = 2
GROUP_C = N_Q_HEADS_C // N_KV_HEADS_C
HEAD_DIM_C = 128
D_Q_C = N_Q_HEADS_C * HEAD_DIM_C
D_KV_C = N_KV_HEADS_C * HEAD_DIM_C

D_MIX = D_A + D_B + D_Q_C

COL_QC = 0
COL_GC = 1024
COL_KV = 2048
COL_XA = 2560
COL_QKVB = 3072
COL_GA = 4608
COL_GB = 5120
COL_BA = 5632
D_Z = 5760
LANES = 128
SUBLANES = 8
LOG2_E = math.log2(math.e)
ATTN_TILE = 256
V_ROWS = HEAD_DIM_C + 16
NEG_BIG = -1e30
GDN_REC_CHUNKS_PER_STEP = 2
GDN_CHUNKS_PER_STEP = 4

VMEM_LIMIT = 56 * 1024 * 1024


def _cparams(*sem):
    return pltpu.CompilerParams(dimension_semantics=tuple(sem), vmem_limit_bytes=VMEM_LIMIT)


def _dot(a, b):
    return jnp.dot(a, b, preferred_element_type=F32)


def _dot_nt(a, b):
    return lax.dot_general(a, b, (((1,), (1,)), ((), ())), preferred_element_type=F32)


def _split2(a):
    hi = a.astype(BF16)
    lo = (a - hi.astype(F32)).astype(BF16)
    return hi, lo


def _split3(a):
    hi = a.astype(BF16)
    r = a - hi.astype(F32)
    mid = r.astype(BF16)
    lo = (r - mid.astype(F32)).astype(BF16)
    return hi, mid, lo


def _dot3(a, b):
    ah, al = _split2(a)
    bh, bl = _split2(b)
    return _dot(ah, bh) + (_dot(al, bh) + _dot(ah, bl))


def _silu(x):
    return x * jax.nn.sigmoid(x)


def _softplus(x):
    return jnp.maximum(x, 0.0) + jnp.log1p(jnp.exp(-jnp.abs(x)))


def _pick(n, cands):
    for c in cands:
        if n % c == 0:
            return c
    raise ValueError(f"no tile for {n} in {cands}")


def _mod_kernel(c_ref, w_ref, b_ref, o_ref):
    o_ref[0] = _dot3(_silu(c_ref[...]), w_ref[0]) + b_ref[0]


def _modulation(cc, w_mod, b_mod):
    depth, d, d3 = w_mod.shape
    tn = 768
    return pl.pallas_call(
        _mod_kernel,
        grid=(depth, d3 // tn),
        in_specs=[
            pl.BlockSpec((SUBLANES, d), lambda l, j: (0, 0)),
            pl.BlockSpec((1, d, tn), lambda l, j: (l, 0, j)),
            pl.BlockSpec((1, 1, tn), lambda l, j: (l, 0, j)),
        ],
        out_specs=pl.BlockSpec((1, SUBLANES, tn), lambda l, j: (l, 0, j)),
        out_shape=jax.ShapeDtypeStruct((depth, SUBLANES, d3), F32),
        compiler_params=_cparams("parallel", "parallel"),
        name="modulation",
    )(cc, w_mod, b_mod.reshape(depth, 1, d3))


def _adaln(h_new, row0, g_ref, mod_ref, lc):
    xn = h_new * lax.rsqrt(jnp.mean(h_new * h_new, axis=-1, keepdims=True) + EPS) * g_ref[...]
    row = row0 + lax.broadcasted_iota(jnp.int32, (h_new.shape[0], 1), 0)
    is_ctx = row < lc
    shift = jnp.where(is_ctx, mod_ref[1:2, 0:D_MODEL], mod_ref[0:1, 0:D_MODEL])
    scale = jnp.where(is_ctx, mod_ref[1:2, D_MODEL:2 * D_MODEL], mod_ref[0:1, D_MODEL:2 * D_MODEL])
    return (xn * (1.0 + scale) + shift).astype(BF16)


def _prenorm_kernel(ctx_ref, x_ref, mod_ref, g_ref, h_ref, y_ref, *, lc, tm):
    row0 = pl.program_id(0) * tm
    h = jnp.where(row0 < lc, ctx_ref[0], x_ref[0])
    h_ref[...] = h
    y_ref[...] = _adaln(h, row0, g_ref, mod_ref, lc)


def _prenorm(ctx, x, mod_l, norm_g_l):
    lc, t_len = ctx.shape[1], x.shape[1]
    tm = 256
    assert lc % tm == 0 and t_len % tm == 0
    n_ctx = lc // tm
    n = lc + t_len
    out_block = pl.BlockSpec((tm, D_MODEL), lambda i: (i, 0))
    return pl.pallas_call(
        functools.partial(_prenorm_kernel, lc=lc, tm=tm),
        grid=(n // tm,),
        in_specs=[
            pl.BlockSpec((1, tm, D_MODEL), lambda i: (0, jnp.minimum(i, n_ctx - 1), 0)),
            pl.BlockSpec((1, tm, D_MODEL), lambda i: (0, jnp.maximum(i - n_ctx, 0), 0)),
            pl.BlockSpec((SUBLANES, 3 * D_MODEL), lambda i: (0, 0)),
            pl.BlockSpec((1, D_MODEL), lambda i: (0, 0)),
        ],
        out_specs=[out_block, out_block],
        out_shape=[jax.ShapeDtypeStruct((n, D_MODEL), F32), jax.ShapeDtypeStruct((n, D_MODEL), BF16)],
        compiler_params=_cparams("parallel"),
        name="prenorm",
    )(ctx, x, mod_l, norm_g_l)


def _inproj_kernel(y_ref, wt_ref, z_ref):
    z_ref[...] = _dot_nt(y_ref[...], wt_ref[0])


def _inproj(y, w_in_t, layer):
    n = y.shape[0]
    tm = _pick(n, (1408, 768, 640, 512, 256))
    tn = 1920
    return pl.pallas_call(
        _inproj_kernel,
        grid=(n // tm, D_Z // tn),
        in_specs=[
            pl.BlockSpec((tm, D_MODEL), lambda i, j: (i, 0)),
            pl.BlockSpec((1, tn, D_MODEL), lambda i, j: (layer, j, 0)),
        ],
        out_specs=pl.BlockSpec((tm, tn), lambda i, j: (i, j)),
        out_shape=jax.ShapeDtypeStruct((n, D_Z), F32),
        compiler_params=_cparams("parallel", "parallel"),
        name="inproj",
    )(y, w_in_t)


def _attn_prep_kernel(q_ref, kv_ref, cos_ref, sin_ref, qn_ref, kn_ref, qo_ref, ko_ref, vo_ref):
    cos = cos_ref[...]
    sin = sin_ref[...]

    def prep(x, gain, scale):
        y = x * lax.rsqrt(jnp.mean(x * x, axis=-1, keepdims=True) + EPS) * gain
        rot = pltpu.roll(y, HEAD_DIM_C // 2, 1)
        return (y * cos + rot * sin) * scale

    for hd in range(N_Q_HEADS_C):
        sl = slice(hd * HEAD_DIM_C, (hd + 1) * HEAD_DIM_C)
        qo_ref[:, sl] = prep(q_ref[:, sl], qn_ref[...], HEAD_DIM_C ** -0.5 * LOG2_E).astype(BF16)
    for hd in range(N_KV_HEADS_C):
        sl = slice(hd * HEAD_DIM_C, (hd + 1) * HEAD_DIM_C)
        ko_ref[:, sl] = prep(kv_ref[:, sl], kn_ref[...], 1.0).astype(BF16)
    vt = kv_ref[:, D_KV_C:2 * D_KV_C].T
    ones = jnp.ones((V_ROWS - HEAD_DIM_C, vt.shape[1]), F32)
    for hd in range(N_KV_HEADS_C):
        vo_ref[0, hd] = jnp.concatenate([vt[hd * HEAD_DIM_C:(hd + 1) * HEAD_DIM_C], ones], axis=0).astype(BF16)


def _attn_prep(z, cos2, sin2, qn_l, kn_l):
    n = z.shape[0]
    tm = ATTN_TILE
    return pl.pallas_call(
        _attn_prep_kernel,
        grid=(n // tm,),
        in_specs=[
            pl.BlockSpec((tm, D_Q_C), lambda i: (i, COL_QC // D_Q_C)),
            pl.BlockSpec((tm, 2 * D_KV_C), lambda i: (i, COL_KV // (2 * D_KV_C))),
            pl.BlockSpec((tm, HEAD_DIM_C), lambda i: (i, 0)),
            pl.BlockSpec((tm, HEAD_DIM_C), lambda i: (i, 0)),
            pl.BlockSpec((1, HEAD_DIM_C), lambda i: (0, 0)),
            pl.BlockSpec((1, HEAD_DIM_C), lambda i: (0, 0)),
        ],
        out_specs=[
            pl.BlockSpec((tm, D_Q_C), lambda i: (i, 0)),
            pl.BlockSpec((tm, D_KV_C), lambda i: (i, 0)),
            pl.BlockSpec((1, N_KV_HEADS_C, V_ROWS, tm), lambda i: (i, 0, 0, 0)),
        ],
        out_shape=[
            jax.ShapeDtypeStruct((n, D_Q_C), BF16),
            jax.ShapeDtypeStruct((n, D_KV_C), BF16),
            jax.ShapeDtypeStruct((n // tm, N_KV_HEADS_C, V_ROWS, tm), BF16),
        ],
        compiler_params=_cparams("parallel"),
        name="attn_prep",
    )(z, z, cos2, sin2, qn_l, kn_l)


def _attn_kernel(q_ref, k_ref, vt_ref, o_ref, s0_scr, sa_scr, sb_scr, acc_scr, *, lc, tq, tk, n_super):
    nb = lc // tk

    def scores(row0, nrows, s_scr):
        kb = k_ref[pl.ds(row0, nrows), :]
        for g in range(GROUP_C):
            s_scr[g] = _dot_nt(kb, q_ref[:, g * HEAD_DIM_C:(g + 1) * HEAD_DIM_C])

    def softmax_pv(s_scr, blk0, n_blk, ms):
        vts = [vt_ref[blk0 + b, 0] for b in range(n_blk)]
        new_m = []
        for g in range(GROUP_C):
            st = s_scr[g]
            m_blk = jnp.max(st, axis=0, keepdims=True)
            m_new = m_blk if ms is None else jnp.maximum(ms[g], m_blk)
            p = jnp.exp2(st - m_new).astype(BF16)
            pv = _dot(vts[0], p[0:tk])
            for b in range(1, n_blk):
                pv = pv + _dot(vts[b], p[b * tk:(b + 1) * tk])
            acc_scr[g] = pv if ms is None else jnp.exp2(ms[g] - m_new) * acc_scr[g] + pv
            new_m.append(m_new)
        return tuple(new_m)

    def super_rows(j):
        return pl.multiple_of(lc + j * (2 * tk), tk)

    if n_super:
        def pair(j, ms):
            scores(super_rows(j + 1), 2 * tk, sb_scr)
            ms = softmax_pv(sa_scr, nb + 2 * j, 2, ms)
            scores(super_rows(j + 2), 2 * tk, sa_scr)
            return softmax_pv(sb_scr, nb + 2 * j + 2, 2, ms)

        scores(super_rows(0), 2 * tk, sa_scr)
        ms = None
        if n_super >= 4:
            ms = pair(0, None)
            ms = lax.fori_loop(1, n_super // 2 - 1, lambda i, ms: pair(2 * i, ms), ms)
        scores(super_rows(n_super - 1), 2 * tk, sb_scr)
        ms = softmax_pv(sa_scr, nb + 2 * (n_super - 2), 2, ms)
        scores(0, lc, s0_scr)
        ms = softmax_pv(sb_scr, nb + 2 * (n_super - 1), 2, ms)
        softmax_pv(s0_scr, 0, nb, ms)
    else:
        scores(0, lc, s0_scr)
        softmax_pv(s0_scr, 0, nb, None)

    for g in range(GROUP_C):
        acc = acc_scr[g]
        o_ref[:, g * HEAD_DIM_C:(g + 1) * HEAD_DIM_C] = (
            acc[0:HEAD_DIM_C] / acc[HEAD_DIM_C:HEAD_DIM_C + 1]).T


def _attention(qh, kh, vt, lc, latent):
    n = qh.shape[0]
    tq = tk = ATTN_TILE
    assert lc % tq == 0 and n % tq == 0
    if latent:
        n_keys, n_q, q_off = n, n - lc, lc // tq
        n_super = (n - lc) // (2 * tk)
        assert n_super * 2 * tk == n - lc and n_super % 2 == 0
        super_shape = (GROUP_C, 2 * tk, tq)
    else:
        n_keys, n_q, q_off, n_super = lc, lc, 0, 0
        super_shape = (1, SUBLANES, LANES)
    gw = GROUP_C * HEAD_DIM_C
    return pl.pallas_call(
        functools.partial(_attn_kernel, lc=lc, tq=tq, tk=tk, n_super=n_super),
        grid=(N_KV_HEADS_C, n_q // tq),
        in_specs=[
            pl.BlockSpec((tq, gw), lambda j, i: (i + q_off, j)),
            pl.BlockSpec((n_keys, HEAD_DIM_C), lambda j, i: (0, j)),
            pl.BlockSpec((n_keys // tk, 1, V_ROWS, tk), lambda j, i: (0, j, 0, 0)),
        ],
        out_specs=pl.BlockSpec((tq, gw), lambda j, i: (i, j)),
        out_shape=jax.ShapeDtypeStruct((n_q, D_Q_C), F32),
        scratch_shapes=[pltpu.VMEM((GROUP_C, lc, tq), F32),
                        pltpu.VMEM(super_shape, F32), pltpu.VMEM(super_shape, F32),
                        pltpu.VMEM((GROUP_C, V_ROWS, tq), F32)],
        compiler_params=_cparams("parallel", "parallel"),
        name="attention" if latent else "attention_ctx",
    )(qh, kh, vt)


def _conv4(x, prev8, next8, w_ref, prev_ok, next_ok):
    tm = x.shape[0]
    row = lax.broadcasted_iota(jnp.int32, (tm, 1), 0)
    p = jnp.where(prev_ok, prev8, 0.0)
    nx = jnp.where(next_ok, next8, 0.0)
    xm1 = jnp.where(row == 0, p[7:8], pltpu.roll(x, 1, 0))
    xm2 = jnp.where(row == 0, p[6:7], jnp.where(row == 1, p[7:8], pltpu.roll(x, 2, 0)))
    xp1 = jnp.where(row == tm - 1, nx[0:1], pltpu.roll(x, tm - 1, 0))
    return w_ref[0:1] * xm2 + w_ref[1:2] * xm1 + w_ref[2:3] * x + w_ref[3:4] * xp1


def _seg_flags(t, nbc, nt):
    prev_ok = jnp.logical_and(t != 0, t != nbc)
    next_ok = jnp.logical_and(t != nbc - 1, t != nt - 1)
    return prev_ok, next_ok


def _halo_maps(tile_of, tm, n, col_block):
    r = tm // SUBLANES
    last = n // SUBLANES - 1
    main = lambda i: (tile_of(i), col_block)
    prev = lambda i: (jnp.maximum(tile_of(i) * r - 1, 0), col_block)
    nxt = lambda i: (jnp.minimum((tile_of(i) + 1) * r, last), col_block)
    return main, prev, nxt


def _rglru_kernel(xf_ref, pf_ref, nf_ref, xb_ref, pb_ref, nb_ref, cw_ref, cb_ref, wg_ref, bg_ref,
                  lam_ref, of_ref, ob_ref, x_scr, hcar, *, tm, nbc, nt):
    i = pl.program_id(0)
    seg = tm // SUBLANES
    ncg = D_A // LANES
    sub = lax.broadcasted_iota(jnp.int32, (SUBLANES, D_A), 0)

    @pl.when(i == 0)
    def _():
        hcar[...] = jnp.zeros_like(hcar)

    tiles = (i, jnp.where(i < nbc, nbc - 1 - i, nt - 1 - (i - nbc)))
    refs = ((xf_ref, pf_ref, nf_ref, of_ref), (xb_ref, pb_ref, nb_ref, ob_ref))
    for d in range(2):
        x_ref, p_ref, n_ref, o_ref = refs[d]
        prev_ok, next_ok = _seg_flags(tiles[d], nbc, nt)
        p = jnp.where(prev_ok, p_ref[...], 0.0)
        nx = jnp.where(next_ok, n_ref[...], 0.0)
        for cg in range(ncg):
            x_scr[cg] = x_ref[:, cg * LANES:(cg + 1) * LANES]
        xg = [jnp.concatenate([x_scr[cg, pl.ds(j, SUBLANES, stride=seg), :] for cg in range(ncg)], axis=1)
              for j in range(seg)]
        before1 = jnp.where(sub == 0, p[7:8], pltpu.roll(xg[seg - 1], 1, 0))
        before2 = jnp.where(sub == 0, p[6:7], pltpu.roll(xg[seg - 2], 1, 0))
        after1 = jnp.where(sub == SUBLANES - 1, nx[0:1], pltpu.roll(xg[0], SUBLANES - 1, 0))
        xp = jnp.concatenate(xg, axis=0)
        xm1 = jnp.concatenate([before1] + xg[:seg - 1], axis=0)
        xm2 = jnp.concatenate([before2, before1] + xg[:seg - 2], axis=0)
        xp1 = jnp.concatenate(xg[1:] + [after1], axis=0)
        u = (cw_ref[0:1] * xm2 + cw_ref[1:2] * xm1 + cw_ref[2:3] * xp + cw_ref[3:4] * xp1) + cb_ref[...]
        gates = _dot(u.astype(BF16), wg_ref[d]) + bg_ref[d]
        gate_r = jax.nn.sigmoid(gates[:, 0:D_A])
        gate_i = jax.nn.sigmoid(gates[:, D_A:2 * D_A])
        log_a = (-LRU_C) * gate_r * _softplus(-lam_ref[d:d + 1, :])
        a = jnp.exp(log_a)
        b = jnp.sqrt(-jnp.tanh(log_a) * (a * a + 1.0)) * (gate_i * u)

        order = list(range(seg)) if d == 0 else list(range(seg - 1, -1, -1))
        hloc = jnp.zeros((SUBLANES, D_A), F32)
        pcum = jnp.ones((SUBLANES, D_A), F32)
        hs, ps = [None] * seg, [None] * seg
        for j in order:
            aj = a[j * SUBLANES:(j + 1) * SUBLANES]
            hloc = aj * hloc + b[j * SUBLANES:(j + 1) * SUBLANES]
            pcum = aj * pcum
            hs[j], ps[j] = hloc, pcum
        carry = hcar[d:d + 1, :]
        seg_in = [None] * SUBLANES
        for s in (range(SUBLANES) if d == 0 else range(SUBLANES - 1, -1, -1)):
            seg_in[s] = carry
            carry = pcum[s:s + 1] * carry + hloc[s:s + 1]
        hcar[d:d + 1, :] = carry
        cin = jnp.concatenate(seg_in, axis=0)
        for j in range(seg):
            hj = hs[j] + ps[j] * cin
            for cg in range(ncg):
                o_ref[cg, pl.ds(j, SUBLANES, stride=seg), :] = hj[:, cg * LANES:(cg + 1) * LANES]


def _rglru(z, cw, cb, wg, bg, lam, lc):
    n = z.shape[0]
    tm = 256
    assert lc % tm == 0 and n % tm == 0
    nbc, nt = lc // tm, n // tm
    fwd = lambda i: i
    bwd = lambda i: jnp.where(i < nbc, nbc - 1 - i, nt - 1 - (i - nbc))
    cblk = COL_XA // D_A
    fm, fp, fn = _halo_maps(fwd, tm, n, cblk)
    bm, bp, bn = _halo_maps(bwd, tm, n, cblk)
    const2 = lambda i: (0, 0)
    const3 = lambda i: (0, 0, 0)
    return pl.pallas_call(
        functools.partial(_rglru_kernel, tm=tm, nbc=nbc, nt=nt),
        grid=(nt,),
        in_specs=[
            pl.BlockSpec((tm, D_A), fm), pl.BlockSpec((SUBLANES, D_A), fp), pl.BlockSpec((SUBLANES, D_A), fn),
            pl.BlockSpec((tm, D_A), bm), pl.BlockSpec((SUBLANES, D_A), bp), pl.BlockSpec((SUBLANES, D_A), bn),
            pl.BlockSpec((4, D_A), const2), pl.BlockSpec((1, D_A), const2),
            pl.BlockSpec((2, D_A, 2 * D_A), const3), pl.BlockSpec((2, 1, 2 * D_A), const3),
            pl.BlockSpec((2, D_A), const2),
        ],
        out_specs=[pl.BlockSpec((D_A // LANES, tm, LANES), lambda i: (0, fwd(i), 0)),
                   pl.BlockSpec((D_A // LANES, tm, LANES), lambda i: (0, bwd(i), 0))],
        out_shape=[jax.ShapeDtypeStruct((D_A // LANES, n, LANES), F32)] * 2,
        scratch_shapes=[pltpu.VMEM((D_A // LANES, tm, LANES), F32), pltpu.VMEM((SUBLANES, D_A), F32)],
        compiler_params=_cparams("arbitrary"),
        name="rglru",
    )(z, z, z, z, z, z, cw, cb, wg, bg, lam)


def _gdn_prep_kernel(x_ref, p_ref, n_ref, ba_ref, cw_ref, av_ref, dt_ref, q_ref, k_ref, v_ref, bg_ref,
                     *, nbc, nt):
    prev_ok, next_ok = _seg_flags(pl.program_id(0), nbc, nt)
    u = _silu(_conv4(x_ref[...], p_ref[...], n_ref[...], cw_ref, prev_ok, next_ok))
    for hd in range(N_HEADS_B):
        sl = slice(hd * HEAD_DIM_B, (hd + 1) * HEAD_DIM_B)
        qv = u[:, hd * HEAD_DIM_B:(hd + 1) * HEAD_DIM_B]
        kv = u[:, D_B + hd * HEAD_DIM_B:D_B + (hd + 1) * HEAD_DIM_B]
        q_ref[:, sl] = qv * lax.rsqrt(jnp.sum(qv * qv, axis=-1, keepdims=True) + EPS)
        k_ref[:, sl] = kv * lax.rsqrt(jnp.sum(kv * kv, axis=-1, keepdims=True) + EPS)
    v_ref[...] = u[:, 2 * D_B:3 * D_B]
    ba = ba_ref[...]
    lane = lax.broadcasted_iota(jnp.int32, ba.shape, 1)
    beta = jax.nn.sigmoid(ba)
    g = -jnp.exp(av_ref[...]) * _softplus(ba + dt_ref[...])
    bg_ref[...] = jnp.where(lane < 2 * N_HEADS_B, beta, jnp.where(lane < 4 * N_HEADS_B, g, 0.0))


def _gdn_prep(z, cw, avec, dtvec, lc):
    n = z.shape[0]
    tm = 256
    nbc, nt = lc // tm, n // tm
    w3 = 3 * D_B
    assert COL_QKVB % w3 == 0
    m, p, nx = _halo_maps(lambda i: i, tm, n, COL_QKVB // w3)
    const2 = lambda i: (0, 0)
    return pl.pallas_call(
        functools.partial(_gdn_prep_kernel, nbc=nbc, nt=nt),
        grid=(nt,),
        in_specs=[
            pl.BlockSpec((tm, w3), m), pl.BlockSpec((SUBLANES, w3), p), pl.BlockSpec((SUBLANES, w3), nx),
            pl.BlockSpec((tm, LANES), lambda i: (i, COL_BA // LANES)),
            pl.BlockSpec((4, w3), const2), pl.BlockSpec((1, LANES), const2), pl.BlockSpec((1, LANES), const2),
        ],
        out_specs=[pl.BlockSpec((tm, D_B), lambda i: (i, 0))] * 3 + [pl.BlockSpec((tm, LANES), lambda i: (i, 0))],
        out_shape=[jax.ShapeDtypeStruct((n, D_B), F32)] * 3 + [jax.ShapeDtypeStruct((n, LANES), F32)],
        compiler_params=_cparams("parallel"),
        name="gdn_prep",
    )(z, z, z, z, cw, avec, dtvec)


def _unit_tri_inverse(mats, row, col, lane_lo):
    def block_diag(y):
        return jnp.concatenate([jnp.where(lane_lo, y, 0.0), jnp.where(lane_lo, 0.0, y)], axis=0).astype(BF16)

    def mm(xs, ys):
        return [_dot(x.astype(BF16), block_diag(y)) for x, y in zip(xs, ys)]

    def same(shift):
        return (row >> shift) == (col >> shift)

    eye = jnp.where(row == col, 1.0, 0.0)
    m8 = [jnp.where(same(3), m, 0.0) for m in mats]
    xs = [eye - m for m in m8]
    pw = mm(m8, m8)
    xs = [x + y for x, y in zip(xs, mm(xs, pw))]
    pw = mm(pw, pw)
    xs = [x + y for x, y in zip(xs, mm(xs, pw))]
    for shift in (4, 5, 6):
        off = [jnp.where(same(shift), jnp.where(same(shift - 1), 0.0, m), 0.0) for m in mats]
        xs = [x - y for x, y in zip(xs, mm(xs, mm(off, xs)))]
    return xs


def _gdn_chunk_kernel(q_ref, k_ref, v_ref, bg_ref, e64_ref, eb_ref, eg_ref, wq_ref, u_ref, l2_ref, egl_ref):
    cs = CHUNK_B
    nh = N_HEADS_B
    ri = lax.broadcasted_iota(jnp.int32, (cs, cs), 0)
    ci = lax.broadcasted_iota(jnp.int32, (cs, cs), 1)
    row = lax.broadcasted_iota(jnp.int32, (cs, nh * cs), 0)
    col = lax.broadcasted_iota(jnp.int32, (cs, nh * cs), 1) & (cs - 1)
    lane_lo = lax.broadcasted_iota(jnp.int32, (cs, LANES), 1) < cs
    lane_lo2 = lax.broadcasted_iota(jnp.int32, (2 * cs, LANES), 1) < cs
    zeros = jnp.zeros((cs, HEAD_DIM_B), F32)
    tri = (jnp.where(ci <= ri, 1.0, 0.0).astype(BF16), jnp.where(ci >= ri, 1.0, 0.0).astype(BF16))
    strict = (row > col, row < col)
    causal = (row >= col, row <= col)

    def dot_parts(lhs, parts):
        return _dot(lhs, parts[0]) + (_dot(lhs, parts[1]) + _dot(lhs, parts[2]))

    stage = {}
    mats = []
    for c in range(GDN_CHUNKS_PER_STEP):
        rows = slice(c * cs, (c + 1) * cs)
        qs = q_ref[rows, :] * (HEAD_DIM_B ** -0.5)
        kn = k_ref[rows, :]
        v = v_ref[rows, :]
        bg_parts = _split3(bg_ref[rows, :])
        for d in range(2):
            expand = lambda e: [_dot(part, e).astype(BF16) for part in bg_parts]
            g64 = [jnp.where(strict[d], part, 0.0).astype(BF16) for part in expand(e64_ref[d])]
            decay = jnp.where(causal[d], jnp.exp(dot_parts(tri[d], g64)), 0.0)
            beta_parts = expand(eb_ref[d])
            beta = beta_parts[0].astype(F32) + (beta_parts[1].astype(F32) + beta_parts[2].astype(F32))
            g_parts = expand(eg_ref[d])
            gc = dot_parts(tri[d], g_parts)
            gl = jnp.broadcast_to(gc[cs - 1:cs] if d == 0 else gc[0:1], gc.shape)
            egc = jnp.exp(gc)
            kb = kn * beta
            lhs = jnp.concatenate([kb, qs], axis=0).astype(BF16)
            prods = []
            for pr in range(2):
                ka = kn[:, (2 * pr) * HEAD_DIM_B:(2 * pr + 1) * HEAD_DIM_B]
                kb_ = kn[:, (2 * pr + 1) * HEAD_DIM_B:(2 * pr + 2) * HEAD_DIM_B]
                rhs_t = jnp.concatenate([jnp.concatenate([ka, zeros], axis=1),
                                         jnp.concatenate([zeros, kb_], axis=1)], axis=0).astype(BF16)
                prods.append(_dot_nt(lhs[:, 2 * pr * HEAD_DIM_B:(2 * pr + 2) * HEAD_DIM_B], rhs_t))
            prod = jnp.concatenate(prods, axis=1)
            mat = jnp.where(strict[d], prod[0:cs] * decay, 0.0)
            mats += [mat[:, 0:LANES], mat[:, LANES:2 * LANES]]
            stage[c, d] = dict(qk=prod[cs:2 * cs] * decay, vb=v * beta, kbe=kb * egc, qhead=qs * egc,
                               ktail=kn * jnp.exp(gl - gc), egl=jnp.exp(gl))

    row2 = lax.broadcasted_iota(jnp.int32, (cs, LANES), 0)
    col2 = lax.broadcasted_iota(jnp.int32, (cs, LANES), 1) & (cs - 1)
    tmats = _unit_tri_inverse(mats, row2, col2, lane_lo)

    for c in range(GDN_CHUNKS_PER_STEP):
        for d in range(2):
            st = stage[c, d]
            for pr in range(2):
                ha, hb = 2 * pr, 2 * pr + 1
                sa = slice(ha * HEAD_DIM_B, (ha + 1) * HEAD_DIM_B)
                sb = slice(hb * HEAD_DIM_B, (hb + 1) * HEAD_DIM_B)
                tp = tmats[(c * 2 + d) * 2 + pr]
                rhs = jnp.concatenate([jnp.concatenate([st["vb"][:, sa], st["kbe"][:, sa]], axis=1),
                                       jnp.concatenate([st["vb"][:, sb], st["kbe"][:, sb]], axis=1)],
                                      axis=0).astype(BF16)
                ra = _dot(jnp.where(lane_lo, tp, 0.0).astype(BF16), rhs)
                rb = _dot(jnp.where(lane_lo, 0.0, tp).astype(BF16), rhs)
                for hh, res, sl in ((ha, ra, sa), (hb, rb, sb)):
                    u_ref[c, d * nh + hh] = res[:, 0:HEAD_DIM_B]
                    wq_ref[c, d * nh + hh] = jnp.concatenate(
                        [res[:, HEAD_DIM_B:2 * HEAD_DIM_B], st["qhead"][:, sl]], axis=0).astype(BF16)
                kt_t = jnp.concatenate([st["ktail"][:, sa], st["ktail"][:, sb]], axis=0).T
                qkp = st["qk"][:, pr * LANES:(pr + 1) * LANES]
                l2_ref[c, d * 2 + pr] = jnp.concatenate(
                    [jnp.where(lane_lo, qkp, 0.0), jnp.where(lane_lo, 0.0, qkp),
                     jnp.where(lane_lo2, kt_t, 0.0), jnp.where(lane_lo2, 0.0, kt_t)], axis=0).astype(BF16)
            egl_ref[c, d] = jnp.concatenate(
                [st["egl"][0:1, hh * HEAD_DIM_B:(hh + 1) * HEAD_DIM_B] for hh in range(nh)]
                + [jnp.zeros((SUBLANES - nh, HEAD_DIM_B), F32)], axis=0)


def _gdn_expanders():
    nh, cs = N_HEADS_B, CHUNK_B
    e64 = np.zeros((2, LANES, nh * cs), np.float32)
    eb = np.zeros((2, LANES, nh * HEAD_DIM_B), np.float32)
    eg = np.zeros((2, LANES, nh * HEAD_DIM_B), np.float32)
    for d in range(2):
        for hh in range(nh):
            p = d * nh + hh
            e64[d, 2 * nh + p, hh * cs:(hh + 1) * cs] = 1.0
            eb[d, p, hh * HEAD_DIM_B:(hh + 1) * HEAD_DIM_B] = 1.0
            eg[d, 2 * nh + p, hh * HEAD_DIM_B:(hh + 1) * HEAD_DIM_B] = 1.0
    return jnp.asarray(e64, BF16), jnp.asarray(eb, BF16), jnp.asarray(eg, BF16)


def _gdn_chunk(qn, kn, v, bg, expanders):
    n = qn.shape[0]
    cs, nh = CHUNK_B, N_HEADS_B
    nc = n // cs
    g = GDN_CHUNKS_PER_STEP
    assert nc % g == 0
    e64, eb, eg = expanders
    const3 = lambda c: (0, 0, 0)
    return pl.pallas_call(
        _gdn_chunk_kernel,
        grid=(nc // g,),
        in_specs=[
            pl.BlockSpec((g * cs, D_B), lambda c: (c, 0)), pl.BlockSpec((g * cs, D_B), lambda c: (c, 0)),
            pl.BlockSpec((g * cs, D_B), lambda c: (c, 0)), pl.BlockSpec((g * cs, LANES), lambda c: (c, 0)),
            pl.BlockSpec(e64.shape, const3), pl.BlockSpec(eb.shape, const3), pl.BlockSpec(eg.shape, const3),
        ],
        out_specs=[
            pl.BlockSpec((g, 2 * nh, 2 * cs, HEAD_DIM_B), lambda c: (c, 0, 0, 0)),
            pl.BlockSpec((g, 2 * nh, cs, HEAD_DIM_B), lambda c: (c, 0, 0, 0)),
            pl.BlockSpec((g, 4, 6 * cs, LANES), lambda c: (c, 0, 0, 0)),
            pl.BlockSpec((g, 2, SUBLANES, HEAD_DIM_B), lambda c: (c, 0, 0, 0)),
        ],
        out_shape=[
            jax.ShapeDtypeStruct((nc, 2 * nh, 2 * cs, HEAD_DIM_B), BF16),
            jax.ShapeDtypeStruct((nc, 2 * nh, cs, HEAD_DIM_B), F32),
            jax.ShapeDtypeStruct((nc, 4, 6 * cs, LANES), BF16),
            jax.ShapeDtypeStruct((nc, 2, SUBLANES, HEAD_DIM_B), F32),
        ],
        compiler_params=_cparams("parallel"),
        name="gdn_chunk",
    )(qn, kn, v, bg, e64, eb, eg)


def _gdn_rec_kernel(wqf_ref, uf_ref, l2f_ref, egf_ref, wqb_ref, ub_ref, l2b_ref, egb_ref,
                    of_ref, ob_ref, s_scr):
    cs, nh = CHUNK_B, N_HEADS_B

    @pl.when(pl.program_id(0) == 0)
    def _():
        s_scr[...] = jnp.zeros_like(s_scr)

    dirs = ((wqf_ref, uf_ref, l2f_ref, egf_ref, of_ref), (wqb_ref, ub_ref, l2b_ref, egb_ref, ob_ref))
    for step in range(GDN_REC_CHUNKS_PER_STEP):
        for d in range(2):
            wq_ref, u_ref, l2_ref, eg_ref, o_ref = dirs[d]
            c = step if d == 0 else GDN_REC_CHUNKS_PER_STEP - 1 - step
            for pr in range(2):
                v_new, o_state = [], []
                for e in range(2):
                    hh = 2 * pr + e
                    r = _dot(wq_ref[c, hh], s_scr[d * nh + hh].astype(BF16))
                    v_new.append(u_ref[c, hh] - r[0:cs])
                    o_state.append(r[cs:2 * cs])
                r2 = _dot(l2_ref[c, pr], jnp.concatenate(v_new, axis=0).astype(BF16))
                for e in range(2):
                    hh = 2 * pr + e
                    o_ref[c * cs:(c + 1) * cs, hh * HEAD_DIM_B:(hh + 1) * HEAD_DIM_B] = (
                        o_state[e] + r2[e * cs:(e + 1) * cs])
                    s_scr[d * nh + hh] = (s_scr[d * nh + hh] * eg_ref[c, 0, hh:hh + 1, :]
                                          + r2[2 * cs + e * 2 * cs:2 * cs + (e + 1) * 2 * cs])


def _gdn_rec(wq, u, l2, egl, lc):
    cs, nh = CHUNK_B, N_HEADS_B
    g = GDN_REC_CHUNKS_PER_STEP
    assert wq.shape[0] % g == 0 and (lc // cs) % g == 0
    nc = wq.shape[0] // g
    ncc = lc // cs // g
    fwd = lambda i: i
    bwd = lambda i: jnp.where(i < ncc, ncc - 1 - i, nc - 1 - (i - ncc))
    specs = []
    for d, order in ((0, fwd), (1, bwd)):
        specs += [
            pl.BlockSpec((g, nh, 2 * cs, HEAD_DIM_B), lambda i, d=d, order=order: (order(i), d, 0, 0)),
            pl.BlockSpec((g, nh, cs, HEAD_DIM_B), lambda i, d=d, order=order: (order(i), d, 0, 0)),
            pl.BlockSpec((g, 2, 6 * cs, LANES), lambda i, d=d, order=order: (order(i), d, 0, 0)),
            pl.BlockSpec((g, 1, SUBLANES, HEAD_DIM_B), lambda i, d=d, order=order: (order(i), d, 0, 0)),
        ]
    return pl.pallas_call(
        _gdn_rec_kernel,
        grid=(nc,),
        in_specs=specs,
        out_specs=[pl.BlockSpec((g * cs, D_B), lambda i: (fwd(i), 0)),
                   pl.BlockSpec((g * cs, D_B), lambda i: (bwd(i), 0))],
        out_shape=[jax.ShapeDtypeStruct((nc * g * cs, D_B), F32)] * 2,
        scratch_shapes=[pltpu.VMEM((2 * nh, HEAD_DIM_B, HEAD_DIM_B), F32)],
        compiler_params=_cparams("arbitrary"),
        name="gdn_rec",
    )(wq, u, l2, egl, wq, u, l2, egl)


def _outproj_kernel(yaf_ref, yab_ref, obf_ref, obb_ref, octx_ref, olat_ref, ga_ref, gb_ref, gc_ref, h_ref,
                    mod_ref, on_ref, w_ref, gn_ref, modn_ref, *out_refs, lc, tm, row_off, last):
    half = tm // 2
    for part in range(2):
        r = slice(part * half, (part + 1) * half)
        row0 = (pl.program_id(0) + row_off) * tm + part * half
        ya = jnp.concatenate([yaf_ref[cg, r, :] + yab_ref[cg, r, :] for cg in range(D_A // LANES)], axis=1)
        mix_a = ya * _silu(ga_ref[r, :])
        ob = obf_ref[r, :] + obb_ref[r, :]
        gb = _silu(gb_ref[r, :])
        mix_b = []
        for hd in range(N_HEADS_B):
            sl = slice(hd * HEAD_DIM_B, (hd + 1) * HEAD_DIM_B)
            x = ob[:, sl]
            mix_b.append(x * lax.rsqrt(jnp.mean(x * x, axis=-1, keepdims=True) + EPS) * on_ref[...] * gb[:, sl])
        oc = olat_ref[r, :] if last else jnp.where(row0 < lc, octx_ref[r, :], olat_ref[r, :])
        mix_c = oc * _silu(gc_ref[r, :])
        mix = jnp.concatenate([mix_a] + mix_b + [mix_c], axis=1).astype(BF16)
        row = row0 + lax.broadcasted_iota(jnp.int32, (half, 1), 0)
        gate = jnp.where(row < lc, mod_ref[1:2, 2 * D_MODEL:3 * D_MODEL], mod_ref[0:1, 2 * D_MODEL:3 * D_MODEL])
        h_new = h_ref[r, :] + gate * _dot(mix, w_ref[0])
        if last:
            out_refs[0][r, :] = (h_new * lax.rsqrt(jnp.mean(h_new * h_new, axis=-1, keepdims=True) + EPS)
                                 * gn_ref[...])
        else:
            out_refs[0][r, :] = h_new
            out_refs[1][r, :] = _adaln(h_new, row0, gn_ref, modn_ref, lc)


def _outproj(ya_f, ya_b, ob_f, ob_b, oc_ctx, oc_lat, z, h, mod_l, onorm_l, w_out_r, layer, gain_next,
             mod_next, lc, last):
    n = h.shape[0]
    tm = ATTN_TILE
    assert lc % tm == 0
    n_ctx = lc // tm
    row_off = n_ctx if last else 0
    rows = lambda width, col: pl.BlockSpec((tm, width), lambda i: (i + row_off, col // width))
    const2 = lambda i: (0, 0)
    out_block = pl.BlockSpec((tm, D_MODEL), lambda i: (i, 0))
    if last:
        out_specs = [out_block]
        out_shape = [jax.ShapeDtypeStruct((n - lc, D_MODEL), F32)]
    else:
        out_specs = [out_block, out_block]
        out_shape = [jax.ShapeDtypeStruct((n, D_MODEL), F32), jax.ShapeDtypeStruct((n, D_MODEL), BF16)]
    return pl.pallas_call(
        functools.partial(_outproj_kernel, lc=lc, tm=tm, row_off=row_off, last=last),
        grid=((n - row_off * tm) // tm,),
        in_specs=[
            pl.BlockSpec((D_A // LANES, tm, LANES), lambda i: (0, i + row_off, 0)),
            pl.BlockSpec((D_A // LANES, tm, LANES), lambda i: (0, i + row_off, 0)),
            rows(D_B, 0), rows(D_B, 0),
            pl.BlockSpec((tm, D_Q_C), lambda i: (jnp.minimum(i + row_off, n_ctx - 1), 0)),
            pl.BlockSpec((tm, D_Q_C), lambda i: (jnp.maximum(i + row_off - n_ctx, 0), 0)),
            rows(D_A, COL_GA), rows(D_B, COL_GB), rows(D_Q_C, COL_GC), rows(D_MODEL, 0),
            pl.BlockSpec((SUBLANES, 3 * D_MODEL), const2), pl.BlockSpec((1, HEAD_DIM_B), const2),
            pl.BlockSpec((1, D_MIX, D_MODEL), lambda i: (layer, 0, 0)),
            pl.BlockSpec((1, D_MODEL), const2), pl.BlockSpec((SUBLANES, 3 * D_MODEL), const2),
        ],
        out_specs=out_specs,
        out_shape=out_shape,
        compiler_params=_cparams("parallel"),
        name="outproj_last" if last else "outproj",
    )(ya_f, ya_b, ob_f, ob_b, oc_ctx, oc_lat, z, z, z, h, mod_l, onorm_l, w_out_r, gain_next, mod_next)


def _rope_tables(t_len, lc):
    rows = t_len // GRID_W
    row = jnp.repeat(jnp.arange(rows, dtype=jnp.int32), GRID_W)
    col = jnp.tile(jnp.arange(GRID_W, dtype=jnp.int32), rows)
    n_freq = HEAD_DIM_C // 4
    inv_freq = ROPE_BASE ** (-jnp.arange(n_freq, dtype=F32) / n_freq)
    ang = jnp.concatenate([row.astype(F32)[:, None] * inv_freq, col.astype(F32)[:, None] * inv_freq], axis=-1)
    cos, sin = jnp.cos(ang), jnp.sin(ang)
    half = HEAD_DIM_C // 2
    cos2 = jnp.concatenate([jnp.ones((lc, 2 * half), F32), jnp.concatenate([cos, cos], axis=-1)], axis=0)
    sin2 = jnp.concatenate([jnp.zeros((lc, 2 * half), F32), jnp.concatenate([-sin, sin], axis=-1)], axis=0)
    return cos2, sin2


def _relayout_w_in_kernel(w_ref, o_ref):
    n_ba = 4 * N_HEADS_B
    src_ba = 2 * D_A + 4 * D_B
    src_qc = src_ba + n_ba
    copies = ((0, COL_XA, D_A), (D_A, COL_GA, D_A), (2 * D_A, COL_QKVB, 3 * D_B),
              (2 * D_A + 3 * D_B, COL_GB, D_B), (src_ba, COL_BA, n_ba),
              (src_qc, COL_QC, D_Q_C), (src_qc + D_Q_C, COL_KV, 2 * D_KV_C),
              (src_qc + D_Q_C + 2 * D_KV_C, COL_GC, D_Q_C))
    for src, dst, rows in copies:
        o_ref[0, dst:dst + rows, :] = w_ref[0, src:src + rows, :].astype(BF16)
    o_ref[0, COL_BA + n_ba:D_Z, :] = jnp.zeros((D_Z - COL_BA - n_ba, o_ref.shape[2]), BF16)


def _relayout_w_in(w_in):
    depth, d, d_in = w_in.shape
    tc = 512
    return pl.pallas_call(
        _relayout_w_in_kernel,
        grid=(depth, d // tc),
        in_specs=[pl.BlockSpec((1, d_in, tc), lambda l, i: (l, 0, i))],
        out_specs=pl.BlockSpec((1, D_Z, tc), lambda l, i: (l, 0, i)),
        out_shape=jax.ShapeDtypeStruct((depth, D_Z, d), BF16),
        compiler_params=_cparams("parallel", "parallel"),
        name="relayout_w_in",
    )(jnp.swapaxes(w_in, 1, 2))


def _block_diag(w):
    eye = jnp.eye(N_BLK_A, dtype=w.dtype)
    out = jnp.einsum('...nij,nm->...nimj', w, eye)
    return out.reshape(w.shape[:-3] + (D_A, D_A))


def kernel(x, c, ctx, c_ctx, norm_g, w_mod, b_mod, w_in, conv_a_w, conv_a_b, w_ra, b_ra, w_ia, b_ia, lam_a,
           conv_b_w, a_log_b, dt_bias_b, onorm_b, qn_c, kn_c, w_out, final_g):
    bsz, t_len, d = x.shape
    lc = ctx.shape[1]
    depth = w_in.shape[0]
    assert bsz == 1 and d == D_MODEL and t_len % GRID_W == 0

    cc = jnp.concatenate([c, c_ctx[None, :], jnp.zeros((SUBLANES - 2, d), F32)], axis=0)
    mod = _modulation(cc, w_mod, b_mod)

    w_in_r = _relayout_w_in(w_in)
    w_out_r = w_out.astype(BF16)
    wg = jnp.concatenate([_block_diag(w_ra), _block_diag(w_ia)], axis=-1).astype(BF16)
    bgate = jnp.concatenate([b_ra, b_ia], axis=-1)[:, :, None, :]
    nh2 = 2 * N_HEADS_B
    lane_pad = lambda v: jnp.pad(v.reshape(depth, 1, nh2), ((0, 0), (0, 0), (nh2, LANES - 2 * nh2)))
    avec = lane_pad(a_log_b)
    dtvec = lane_pad(dt_bias_b)
    cos2, sin2 = _rope_tables(t_len, lc)
    expanders = _gdn_expanders()

    h, y = _prenorm(ctx, x, mod[0], norm_g[0][None, :])
    for l in range(depth):
        last = l == depth - 1
        z = _inproj(y, w_in_r, l)
        qh, kh, vt = _attn_prep(z, cos2, sin2, qn_c[l][None, :], kn_c[l][None, :])
        oc_lat = _attention(qh, kh, vt, lc, latent=True)
        oc_ctx = oc_lat if last else _attention(qh, kh, vt, lc, latent=False)
        ya_f, ya_b = _rglru(z, conv_a_w[l], conv_a_b[l][None, :], wg[l], bgate[l], lam_a[l], lc)
        qn, kn, v, bg = _gdn_prep(z, conv_b_w[l], avec[l], dtvec[l], lc)
        wq, u, l2, egl = _gdn_chunk(qn, kn, v, bg, expanders)
        ob_f, ob_b = _gdn_rec(wq, u, l2, egl, lc)
        gain_next = final_g[None, :] if last else norm_g[l + 1][None, :]
        mod_next = mod[l] if last else mod[l + 1]
        outs = _outproj(ya_f, ya_b, ob_f, ob_b, oc_ctx, oc_lat, z, h, mod[l], onorm_b[l][None, :], w_out_r, l,
                        gain_next, mod_next, lc, last)
        if last:
            return outs[0][None]
        h, y = outs
```

```python
import functools
import math

import numpy as np
import jax
import jax.numpy as jnp
from jax import lax
from jax.experimental import pallas as pl
from jax.experimental.pallas import tpu as pltpu

F32 = jnp.float32
BF16 = jnp.bfloat16

D_MODEL = 2048
GRID_W = 64
EPS = 1e-6
ROPE_BASE = 10000.0

D_A = 512
N_BLK_A = 8
BLK_A = D_A // N_BLK_A
LRU_C = 8.0

N_HEADS_B = 4
HEAD_DIM_B = 128
D_B = N_HEADS_B * HEAD_DIM_B
CHUNK_B = 64

N_Q_HEADS_C = 8
N_KV_HEADS_C = 2
GROUP_C = N_Q_HEADS_C // N_KV_HEADS_C
HEAD_DIM_C = 128
D_Q_C = N_Q_HEADS_C * HEAD_DIM_C
D_KV_C = N_KV_HEADS_C * HEAD_DIM_C

D_MIX = D_A + D_B + D_Q_C

COL_QC = 0
COL_GC = 1024
COL_KV = 2048
COL_XA = 2560
COL_QKVB = 3072
COL_GA = 4608
COL_GB = 5120
COL_BA = 5632
D_Z = 5760
LANES = 128
SUBLANES = 8
LOG2_E = math.log2(math.e)
ATTN_TILE = 256
V_ROWS = HEAD_DIM_C + 16
NEG_BIG = -1e30
GDN_REC_CHUNKS_PER_STEP = 4
GDN_CHUNKS_PER_STEP = 4

VMEM_LIMIT = 56 * 1024 * 1024


def _cparams(*sem):
    return pltpu.CompilerParams(dimension_semantics=tuple(sem), vmem_limit_bytes=VMEM_LIMIT)


def _dot(a, b):
    return jnp.dot(a, b, preferred_element_type=F32)


def _dot_nt(a, b):
    return lax.dot_general(a, b, (((1,), (1,)), ((), ())), preferred_element_type=F32)


def _split2(a):
    hi = a.astype(BF16)
    lo = (a - hi.astype(F32)).astype(BF16)
    return hi, lo


def _split3(a):
    hi = a.astype(BF16)
    r = a - hi.astype(F32)
    mid = r.astype(BF16)
    lo = (r - mid.astype(F32)).astype(BF16)
    return hi, mid, lo


def _dot3(a, b):
    ah, al = _split2(a)
    bh, bl = _split2(b)
    return _dot(ah, bh) + (_dot(al, bh) + _dot(ah, bl))


def _silu(x):
    return x * jax.nn.sigmoid(x)


def _softplus(x):
    return jnp.maximum(x, 0.0) + jnp.log1p(jnp.exp(-jnp.abs(x)))


def _pick(n, cands):
    for c in cands:
        if n % c == 0:
            return c
    raise ValueError(f"no tile for {n} in {cands}")


def _mod_kernel(c_ref, w_ref, b_ref, o_ref):
    o_ref[0] = _dot3(_silu(c_ref[...]), w_ref[0]) + b_ref[0]


def _modulation(cc, w_mod, b_mod):
    depth, d, d3 = w_mod.shape
    tn = 768
    return pl.pallas_call(
        _mod_kernel,
        grid=(depth, d3 // tn),
        in_specs=[
            pl.BlockSpec((SUBLANES, d), lambda l, j: (0, 0)),
            pl.BlockSpec((1, d, tn), lambda l, j: (l, 0, j)),
            pl.BlockSpec((1, 1, tn), lambda l, j: (l, 0, j)),
        ],
        out_specs=pl.BlockSpec((1, SUBLANES, tn), lambda l, j: (l, 0, j)),
        out_shape=jax.ShapeDtypeStruct((depth, SUBLANES, d3), F32),
        compiler_params=_cparams("parallel", "parallel"),
        name="modulation",
    )(cc, w_mod, b_mod.reshape(depth, 1, d3))


def _adaln(h_new, row0, g_ref, mod_ref, lc):
    xn = h_new * lax.rsqrt(jnp.mean(h_new * h_new, axis=-1, keepdims=True) + EPS) * g_ref[...]
    row = row0 + lax.broadcasted_iota(jnp.int32, (h_new.shape[0], 1), 0)
    is_ctx = row < lc
    shift = jnp.where(is_ctx, mod_ref[1:2, 0:D_MODEL], mod_ref[0:1, 0:D_MODEL])
    scale = jnp.where(is_ctx, mod_ref[1:2, D_MODEL:2 * D_MODEL], mod_ref[0:1, D_MODEL:2 * D_MODEL])
    return (xn * (1.0 + scale) + shift).astype(BF16)


def _prenorm_kernel(ctx_ref, x_ref, mod_ref, g_ref, h_ref, y_ref, *, lc, tm):
    row0 = pl.program_id(0) * tm
    h = jnp.where(row0 < lc, ctx_ref[0], x_ref[0])
    h_ref[...] = h
    y_ref[...] = _adaln(h, row0, g_ref, mod_ref, lc)


def _prenorm(ctx, x, mod_l, norm_g_l):
    lc, t_len = ctx.shape[1], x.shape[1]
    tm = 256
    assert lc % tm == 0 and t_len % tm == 0
    n_ctx = lc // tm
    n = lc + t_len
    out_block = pl.BlockSpec((tm, D_MODEL), lambda i: (i, 0))
    return pl.pallas_call(
        functools.partial(_prenorm_kernel, lc=lc, tm=tm),
        grid=(n // tm,),
        in_specs=[
            pl.BlockSpec((1, tm, D_MODEL), lambda i: (0, jnp.minimum(i, n_ctx - 1), 0)),
            pl.BlockSpec((1, tm, D_MODEL), lambda i: (0, jnp.maximum(i - n_ctx, 0), 0)),
            pl.BlockSpec((SUBLANES, 3 * D_MODEL), lambda i: (0, 0)),
            pl.BlockSpec((1, D_MODEL), lambda i: (0, 0)),
        ],
        out_specs=[out_block, out_block],
        out_shape=[jax.ShapeDtypeStruct((n, D_MODEL), F32), jax.ShapeDtypeStruct((n, D_MODEL), BF16)],
        compiler_params=_cparams("parallel"),
        name="prenorm",
    )(ctx, x, mod_l, norm_g_l)


def _inproj_kernel(y_ref, wt_ref, z_ref):
    z_ref[...] = _dot_nt(y_ref[...], wt_ref[0])


def _inproj(y, w_in_t, layer):
    n = y.shape[0]
    tm = _pick(n, (1408, 768, 640, 512, 256))
    tn = 1920
    return pl.pallas_call(
        _inproj_kernel,
        grid=(n // tm, D_Z // tn),
        in_specs=[
            pl.BlockSpec((tm, D_MODEL), lambda i, j: (i, 0)),
            pl.BlockSpec((1, tn, D_MODEL), lambda i, j: (layer, j, 0)),
        ],
        out_specs=pl.BlockSpec((tm, tn), lambda i, j: (i, j)),
        out_shape=jax.ShapeDtypeStruct((n, D_Z), F32),
        compiler_params=_cparams("parallel", "parallel"),
        name="inproj",
    )(y, w_in_t)


def _attn_prep_kernel(q_ref, kv_ref, cos_ref, sin_ref, qn_ref, kn_ref, qo_ref, ko_ref, vo_ref):
    cos = cos_ref[...]
    sin = sin_ref[...]

    def prep(x, gain, scale):
        y = x * lax.rsqrt(jnp.mean(x * x, axis=-1, keepdims=True) + EPS) * gain
        rot = pltpu.roll(y, HEAD_DIM_C // 2, 1)
        return (y * cos + rot * sin) * scale

    for hd in range(N_Q_HEADS_C):
        sl = slice(hd * HEAD_DIM_C, (hd + 1) * HEAD_DIM_C)
        qo_ref[:, sl] = prep(q_ref[:, sl], qn_ref[...], HEAD_DIM_C ** -0.5 * LOG2_E).astype(BF16)
    for hd in range(N_KV_HEADS_C):
        sl = slice(hd * HEAD_DIM_C, (hd + 1) * HEAD_DIM_C)
        ko_ref[:, sl] = prep(kv_ref[:, sl], kn_ref[...], 1.0).astype(BF16)
    vt = kv_ref[:, D_KV_C:2 * D_KV_C].T
    ones = jnp.ones((V_ROWS - HEAD_DIM_C, vt.shape[1]), F32)
    for hd in range(N_KV_HEADS_C):
        vo_ref[0, hd] = jnp.concatenate([vt[hd * HEAD_DIM_C:(hd + 1) * HEAD_DIM_C], ones], axis=0).astype(BF16)


def _attn_prep(z, cos2, sin2, qn_l, kn_l):
    n = z.shape[0]
    tm = ATTN_TILE
    return pl.pallas_call(
        _attn_prep_kernel,
        grid=(n // tm,),
        in_specs=[
            pl.BlockSpec((tm, D_Q_C), lambda i: (i, COL_QC // D_Q_C)),
            pl.BlockSpec((tm, 2 * D_KV_C), lambda i: (i, COL_KV // (2 * D_KV_C))),
            pl.BlockSpec((tm, HEAD_DIM_C), lambda i: (i, 0)),
            pl.BlockSpec((tm, HEAD_DIM_C), lambda i: (i, 0)),
            pl.BlockSpec((1, HEAD_DIM_C), lambda i: (0, 0)),
            pl.BlockSpec((1, HEAD_DIM_C), lambda i: (0, 0)),
        ],
        out_specs=[
            pl.BlockSpec((tm, D_Q_C), lambda i: (i, 0)),
            pl.BlockSpec((tm, D_KV_C), lambda i: (i, 0)),
            pl.BlockSpec((1, N_KV_HEADS_C, V_ROWS, tm), lambda i: (i, 0, 0, 0)),
        ],
        out_shape=[
            jax.ShapeDtypeStruct((n, D_Q_C), BF16),
            jax.ShapeDtypeStruct((n, D_KV_C), BF16),
            jax.ShapeDtypeStruct((n // tm, N_KV_HEADS_C, V_ROWS, tm), BF16),
        ],
        compiler_params=_cparams("parallel"),
        name="attn_prep",
    )(z, z, cos2, sin2, qn_l, kn_l)


def _attn_kernel(q_ref, k_ref, vt_ref, o_ref, s0_scr, sa_scr, sb_scr, acc_scr, *, lc, tq, tk, n_super):
    nb = lc // tk

    def scores(row0, nrows, s_scr):
        kb = k_ref[pl.ds(row0, nrows), :]
        for g in range(GROUP_C):
            s_scr[g] = _dot_nt(kb, q_ref[:, g * HEAD_DIM_C:(g + 1) * HEAD_DIM_C])

    def softmax_pv(s_scr, blk0, n_blk, ms):
        vts = [vt_ref[blk0 + b, 0] for b in range(n_blk)]
        new_m = []
        for g in range(GROUP_C):
            st = s_scr[g]
            m_blk = jnp.max(st, axis=0, keepdims=True)
            m_new = m_blk if ms is None else jnp.maximum(ms[g], m_blk)
            p = jnp.exp2(st - m_new).astype(BF16)
            pv = _dot(vts[0], p[0:tk])
            for b in range(1, n_blk):
                pv = pv + _dot(vts[b], p[b * tk:(b + 1) * tk])
            acc_scr[g] = pv if ms is None else jnp.exp2(ms[g] - m_new) * acc_scr[g] + pv
            new_m.append(m_new)
        return tuple(new_m)

    def super_rows(j):
        return pl.multiple_of(lc + j * (2 * tk), tk)

    if n_super:
        def pair(j, ms):
            scores(super_rows(j + 1), 2 * tk, sb_scr)
            ms = softmax_pv(sa_scr, nb + 2 * j, 2, ms)
            scores(super_rows(j + 2), 2 * tk, sa_scr)
            return softmax_pv(sb_scr, nb + 2 * j + 2, 2, ms)

        scores(super_rows(0), 2 * tk, sa_scr)
        ms = None
        if n_super >= 4:
            ms = pair(0, None)
            ms = lax.fori_loop(1, n_super // 2 - 1, lambda i, ms: pair(2 * i, ms), ms)
        scores(super_rows(n_super - 1), 2 * tk, sb_scr)
        ms = softmax_pv(sa_scr, nb + 2 * (n_super - 2), 2, ms)
        scores(0, lc, s0_scr)
        ms = softmax_pv(sb_scr, nb + 2 * (n_super - 1), 2, ms)
        softmax_pv(s0_scr, 0, nb, ms)
    else:
        scores(0, lc, s0_scr)
        softmax_pv(s0_scr, 0, nb, None)

    for g in range(GROUP_C):
        acc = acc_scr[g]
        o_ref[:, g * HEAD_DIM_C:(g + 1) * HEAD_DIM_C] = (
            acc[0:HEAD_DIM_C] / acc[HEAD_DIM_C:HEAD_DIM_C + 1]).T


def _attention(qh, kh, vt, lc, latent):
    n = qh.shape[0]
    tq = tk = ATTN_TILE
    assert lc % tq == 0 and n % tq == 0
    if latent:
        n_keys, n_q, q_off = n, n - lc, lc // tq
        n_super = (n - lc) // (2 * tk)
        assert n_super * 2 * tk == n - lc and n_super % 2 == 0
        super_shape = (GROUP_C, 2 * tk, tq)
    else:
        n_keys, n_q, q_off, n_super = lc, lc, 0, 0
        super_shape = (1, SUBLANES, LANES)
    gw = GROUP_C * HEAD_DIM_C
    return pl.pallas_call(
        functools.partial(_attn_kernel, lc=lc, tq=tq, tk=tk, n_super=n_super),
        grid=(N_KV_HEADS_C, n_q // tq),
        in_specs=[
            pl.BlockSpec((tq, gw), lambda j, i: (i + q_off, j)),
            pl.BlockSpec((n_keys, HEAD_DIM_C), lambda j, i: (0, j)),
            pl.BlockSpec((n_keys // tk, 1, V_ROWS, tk), lambda j, i: (0, j, 0, 0)),
        ],
        out_specs=pl.BlockSpec((tq, gw), lambda j, i: (i, j)),
        out_shape=jax.ShapeDtypeStruct((n_q, D_Q_C), F32),
        scratch_shapes=[pltpu.VMEM((GROUP_C, lc, tq), F32),
                        pltpu.VMEM(super_shape, F32), pltpu.VMEM(super_shape, F32),
                        pltpu.VMEM((GROUP_C, V_ROWS, tq), F32)],
        compiler_params=_cparams("parallel", "parallel"),
        name="attention" if latent else "attention_ctx",
    )(qh, kh, vt)


def _conv4(x, prev8, next8, w_ref, prev_ok, next_ok):
    tm = x.shape[0]
    row = lax.broadcasted_iota(jnp.int32, (tm, 1), 0)
    p = jnp.where(prev_ok, prev8, 0.0)
    nx = jnp.where(next_ok, next8, 0.0)
    xm1 = jnp.where(row == 0, p[7:8], pltpu.roll(x, 1, 0))
    xm2 = jnp.where(row == 0, p[6:7], jnp.where(row == 1, p[7:8], pltpu.roll(x, 2, 0)))
    xp1 = jnp.where(row == tm - 1, nx[0:1], pltpu.roll(x, tm - 1, 0))
    return w_ref[0:1] * xm2 + w_ref[1:2] * xm1 + w_ref[2:3] * x + w_ref[3:4] * xp1


def _seg_flags(t, nbc, nt):
    prev_ok = jnp.logical_and(t != 0, t != nbc)
    next_ok = jnp.logical_and(t != nbc - 1, t != nt - 1)
    return prev_ok, next_ok


def _halo_maps(tile_of, tm, n, col_block):
    r = tm // SUBLANES
    last = n // SUBLANES - 1
    main = lambda i: (tile_of(i), col_block)
    prev = lambda i: (jnp.maximum(tile_of(i) * r - 1, 0), col_block)
    nxt = lambda i: (jnp.minimum((tile_of(i) + 1) * r, last), col_block)
    return main, prev, nxt


def _rglru_kernel(xf_ref, pf_ref, nf_ref, xb_ref, pb_ref, nb_ref, cw_ref, cb_ref, wg_ref, bg_ref,
                  lam_ref, of_ref, ob_ref, x_scr, hcar, *, tm, nbc, nt):
    i = pl.program_id(0)
    seg = tm // SUBLANES
    ncg = D_A // LANES
    sub = lax.broadcasted_iota(jnp.int32, (SUBLANES, D_A), 0)

    @pl.when(i == 0)
    def _():
        hcar[...] = jnp.zeros_like(hcar)

    tiles = (i, jnp.where(i < nbc, nbc - 1 - i, nt - 1 - (i - nbc)))
    refs = ((xf_ref, pf_ref, nf_ref, of_ref), (xb_ref, pb_ref, nb_ref, ob_ref))
    for d in range(2):
        x_ref, p_ref, n_ref, o_ref = refs[d]
        prev_ok, next_ok = _seg_flags(tiles[d], nbc, nt)
        p = jnp.where(prev_ok, p_ref[...], 0.0)
        nx = jnp.where(next_ok, n_ref[...], 0.0)
        for cg in range(ncg):
            x_scr[cg] = x_ref[:, cg * LANES:(cg + 1) * LANES]
        xg = [jnp.concatenate([x_scr[cg, pl.ds(j, SUBLANES, stride=seg), :] for cg in range(ncg)], axis=1)
              for j in range(seg)]
        before1 = jnp.where(sub == 0, p[7:8], pltpu.roll(xg[seg - 1], 1, 0))
        before2 = jnp.where(sub == 0, p[6:7], pltpu.roll(xg[seg - 2], 1, 0))
        after1 = jnp.where(sub == SUBLANES - 1, nx[0:1], pltpu.roll(xg[0], SUBLANES - 1, 0))
        xp = jnp.concatenate(xg, axis=0)
        xm1 = jnp.concatenate([before1] + xg[:seg - 1], axis=0)
        xm2 = jnp.concatenate([before2, before1] + xg[:seg - 2], axis=0)
        xp1 = jnp.concatenate(xg[1:] + [after1], axis=0)
        u = (cw_ref[0:1] * xm2 + cw_ref[1:2] * xm1 + cw_ref[2:3] * xp + cw_ref[3:4] * xp1) + cb_ref[...]
        gates = _dot(u.astype(BF16), wg_ref[d]) + bg_ref[d]
        gate_r = jax.nn.sigmoid(gates[:, 0:D_A])
        gate_i = jax.nn.sigmoid(gates[:, D_A:2 * D_A])
        log_a = (-LRU_C) * gate_r * _softplus(-lam_ref[d:d + 1, :])
        a = jnp.exp(log_a)
        b = jnp.sqrt(-jnp.tanh(log_a) * (a * a + 1.0)) * (gate_i * u)

        order = list(range(seg)) if d == 0 else list(range(seg - 1, -1, -1))
        hloc = jnp.zeros((SUBLANES, D_A), F32)
        pcum = jnp.ones((SUBLANES, D_A), F32)
        hs, ps = [None] * seg, [None] * seg
        for j in order:
            aj = a[j * SUBLANES:(j + 1) * SUBLANES]
            hloc = aj * hloc + b[j * SUBLANES:(j + 1) * SUBLANES]
            pcum = aj * pcum
            hs[j], ps[j] = hloc, pcum
        carry = hcar[d:d + 1, :]
        seg_in = [None] * SUBLANES
        for s in (range(SUBLANES) if d == 0 else range(SUBLANES - 1, -1, -1)):
            seg_in[s] = carry
            carry = pcum[s:s + 1] * carry + hloc[s:s + 1]
        hcar[d:d + 1, :] = carry
        cin = jnp.concatenate(seg_in, axis=0)
        for j in range(seg):
            hj = hs[j] + ps[j] * cin
            for cg in range(ncg):
                o_ref[cg, pl.ds(j, SUBLANES, stride=seg), :] = hj[:, cg * LANES:(cg + 1) * LANES]


def _rglru(z, cw, cb, wg, bg, lam, lc):
    n = z.shape[0]
    tm = 256
    assert lc % tm == 0 and n % tm == 0
    nbc, nt = lc // tm, n // tm
    fwd = lambda i: i
    bwd = lambda i: jnp.where(i < nbc, nbc - 1 - i, nt - 1 - (i - nbc))
    cblk = COL_XA // D_A
    fm, fp, fn = _halo_maps(fwd, tm, n, cblk)
    bm, bp, bn = _halo_maps(bwd, tm, n, cblk)
    const2 = lambda i: (0, 0)
    const3 = lambda i: (0, 0, 0)
    return pl.pallas_call(
        functools.partial(_rglru_kernel, tm=tm, nbc=nbc, nt=nt),
        grid=(nt,),
        in_specs=[
            pl.BlockSpec((tm, D_A), fm), pl.BlockSpec((SUBLANES, D_A), fp), pl.BlockSpec((SUBLANES, D_A), fn),
            pl.BlockSpec((tm, D_A), bm), pl.BlockSpec((SUBLANES, D_A), bp), pl.BlockSpec((SUBLANES, D_A), bn),
            pl.BlockSpec((4, D_A), const2), pl.BlockSpec((1, D_A), const2),
            pl.BlockSpec((2, D_A, 2 * D_A), const3), pl.BlockSpec((2, 1, 2 * D_A), const3),
            pl.BlockSpec((2, D_A), const2),
        ],
        out_specs=[pl.BlockSpec((D_A // LANES, tm, LANES), lambda i: (0, fwd(i), 0)),
                   pl.BlockSpec((D_A // LANES, tm, LANES), lambda i: (0, bwd(i), 0))],
        out_shape=[jax.ShapeDtypeStruct((D_A // LANES, n, LANES), F32)] * 2,
        scratch_shapes=[pltpu.VMEM((D_A // LANES, tm, LANES), F32), pltpu.VMEM((SUBLANES, D_A), F32)],
        compiler_params=_cparams("arbitrary"),
        name="rglru",
    )(z, z, z, z, z, z, cw, cb, wg, bg, lam)


def _gdn_prep_kernel(x_ref, p_ref, n_ref, ba_ref, cw_ref, av_ref, dt_ref, q_ref, k_ref, v_ref, bg_ref,
                     *, nbc, nt):
    prev_ok, next_ok = _seg_flags(pl.program_id(0), nbc, nt)
    u = _silu(_conv4(x_ref[...], p_ref[...], n_ref[...], cw_ref, prev_ok, next_ok))
    for hd in range(N_HEADS_B):
        sl = slice(hd * HEAD_DIM_B, (hd + 1) * HEAD_DIM_B)
        qv = u[:, hd * HEAD_DIM_B:(hd + 1) * HEAD_DIM_B]
        kv = u[:, D_B + hd * HEAD_DIM_B:D_B + (hd + 1) * HEAD_DIM_B]
        q_ref[:, sl] = qv * lax.rsqrt(jnp.sum(qv * qv, axis=-1, keepdims=True) + EPS)
        k_ref[:, sl] = kv * lax.rsqrt(jnp.sum(kv * kv, axis=-1, keepdims=True) + EPS)
    v_ref[...] = u[:, 2 * D_B:3 * D_B]
    ba = ba_ref[...]
    lane = lax.broadcasted_iota(jnp.int32, ba.shape, 1)
    beta = jax.nn.sigmoid(ba)
    g = -jnp.exp(av_ref[...]) * _softplus(ba + dt_ref[...])
    bg_ref[...] = jnp.where(lane < 2 * N_HEADS_B, beta, jnp.where(lane < 4 * N_HEADS_B, g, 0.0))


def _gdn_prep(z, cw, avec, dtvec, lc):
    n = z.shape[0]
    tm = 256
    nbc, nt = lc // tm, n // tm
    w3 = 3 * D_B
    assert COL_QKVB % w3 == 0
    m, p, nx = _halo_maps(lambda i: i, tm, n, COL_QKVB // w3)
    const2 = lambda i: (0, 0)
    return pl.pallas_call(
        functools.partial(_gdn_prep_kernel, nbc=nbc, nt=nt),
        grid=(nt,),
        in_specs=[
            pl.BlockSpec((tm, w3), m), pl.BlockSpec((SUBLANES, w3), p), pl.BlockSpec((SUBLANES, w3), nx),
            pl.BlockSpec((tm, LANES), lambda i: (i, COL_BA // LANES)),
            pl.BlockSpec((4, w3), const2), pl.BlockSpec((1, LANES), const2), pl.BlockSpec((1, LANES), const2),
        ],
        out_specs=[pl.BlockSpec((tm, D_B), lambda i: (i, 0))] * 3 + [pl.BlockSpec((tm, LANES), lambda i: (i, 0))],
        out_shape=[jax.ShapeDtypeStruct((n, D_B), F32)] * 3 + [jax.ShapeDtypeStruct((n, LANES), F32)],
        compiler_params=_cparams("parallel"),
        name="gdn_prep",
    )(z, z, z, z, cw, avec, dtvec)


def _unit_tri_inverse(mats, row, col, lane_lo):
    def block_diag(y):
        return jnp.concatenate([jnp.where(lane_lo, y, 0.0), jnp.where(lane_lo, 0.0, y)], axis=0).astype(BF16)

    def mm(xs, ys):
        return [_dot(x.astype(BF16), block_diag(y)) for x, y in zip(xs, ys)]

    def same(shift):
        return (row >> shift) == (col >> shift)

    eye = jnp.where(row == col, 1.0, 0.0)
    m8 = [jnp.where(same(3), m, 0.0) for m in mats]
    xs = [eye - m for m in m8]
    pw = mm(m8, m8)
    xs = [x + y for x, y in zip(xs, mm(xs, pw))]
    pw = mm(pw, pw)
    xs = [x + y for x, y in zip(xs, mm(xs, pw))]
    for shift in (4, 5, 6):
        off = [jnp.where(same(shift), jnp.where(same(shift - 1), 0.0, m), 0.0) for m in mats]
        xs = [x - y for x, y in zip(xs, mm(xs, mm(off, xs)))]
    return xs


def _gdn_chunk_kernel(q_ref, k_ref, v_ref, bg_ref, e64_ref, eb_ref, eg_ref, wq_ref, u_ref, l2_ref, egl_ref):
    cs = CHUNK_B
    nh = N_HEADS_B
    ri = lax.broadcasted_iota(jnp.int32, (cs, cs), 0)
    ci = lax.broadcasted_iota(jnp.int32, (cs, cs), 1)
    row = lax.broadcasted_iota(jnp.int32, (cs, nh * cs), 0)
    col = lax.broadcasted_iota(jnp.int32, (cs, nh * cs), 1) & (cs - 1)
    lane_lo = lax.broadcasted_iota(jnp.int32, (cs, LANES), 1) < cs
    lane_lo2 = lax.broadcasted_iota(jnp.int32, (2 * cs, LANES), 1) < cs
    zeros = jnp.zeros((cs, HEAD_DIM_B), F32)
    tri = (jnp.where(ci <= ri, 1.0, 0.0).astype(BF16), jnp.where(ci >= ri, 1.0, 0.0).astype(BF16))
    strict = (row > col, row < col)
    causal = (row >= col, row <= col)

    def dot_parts(lhs, parts):
        return _dot(lhs, parts[0]) + (_dot(lhs, parts[1]) + _dot(lhs, parts[2]))

    stage = {}
    mats = []
    for c in range(GDN_CHUNKS_PER_STEP):
        rows = slice(c * cs, (c + 1) * cs)
        qs = q_ref[rows, :] * (HEAD_DIM_B ** -0.5)
        kn = k_ref[rows, :]
        v = v_ref[rows, :]
        bg_parts = _split3(bg_ref[rows, :])
        for d in range(2):
            expand = lambda e: [_dot(part, e).astype(BF16) for part in bg_parts]
            g64 = [jnp.where(strict[d], part, 0.0).astype(BF16) for part in expand(e64_ref[d])]
            decay = jnp.where(causal[d], jnp.exp(dot_parts(tri[d], g64)), 0.0)
            beta_parts = expand(eb_ref[d])
            beta = beta_parts[0].astype(F32) + (beta_parts[1].astype(F32) + beta_parts[2].astype(F32))
            g_parts = expand(eg_ref[d])
            gc = dot_parts(tri[d], g_parts)
            gl = jnp.broadcast_to(gc[cs - 1:cs] if d == 0 else gc[0:1], gc.shape)
            egc = jnp.exp(gc)
            kb = kn * beta
            lhs = jnp.concatenate([kb, qs], axis=0).astype(BF16)
            prods = []
            for pr in range(2):
                ka = kn[:, (2 * pr) * HEAD_DIM_B:(2 * pr + 1) * HEAD_DIM_B]
                kb_ = kn[:, (2 * pr + 1) * HEAD_DIM_B:(2 * pr + 2) * HEAD_DIM_B]
                rhs_t = jnp.concatenate([jnp.concatenate([ka, zeros], axis=1),
                                         jnp.concatenate([zeros, kb_], axis=1)], axis=0).astype(BF16)
                prods.append(_dot_nt(lhs[:, 2 * pr * HEAD_DIM_B:(2 * pr + 2) * HEAD_DIM_B], rhs_t))
            prod = jnp.concatenate(prods, axis=1)
            mat = jnp.where(strict[d], prod[0:cs] * decay, 0.0)
            mats += [mat[:, 0:LANES], mat[:, LANES:2 * LANES]]
            stage[c, d] = dict(qk=prod[cs:2 * cs] * decay, vb=v * beta, kbe=kb * egc, qhead=qs * egc,
                               ktail=kn * jnp.exp(gl - gc), egl=jnp.exp(gl))

    row2 = lax.broadcasted_iota(jnp.int32, (cs, LANES), 0)
    col2 = lax.broadcasted_iota(jnp.int32, (cs, LANES), 1) & (cs - 1)
    tmats = _unit_tri_inverse(mats, row2, col2, lane_lo)

    for c in range(GDN_CHUNKS_PER_STEP):
        for d in range(2):
            st = stage[c, d]
            for pr in range(2):
                ha, hb = 2 * pr, 2 * pr + 1
                sa = slice(ha * HEAD_DIM_B, (ha + 1) * HEAD_DIM_B)
                sb = slice(hb * HEAD_DIM_B, (hb + 1) * HEAD_DIM_B)
                tp = tmats[(c * 2 + d) * 2 + pr]
                rhs = jnp.concatenate([jnp.concatenate([st["vb"][:, sa], st["kbe"][:, sa]], axis=1),
                                       jnp.concatenate([st["vb"][:, sb], st["kbe"][:, sb]], axis=1)],
                                      axis=0).astype(BF16)
                ra = _dot(jnp.where(lane_lo, tp, 0.0).astype(BF16), rhs)
                rb = _dot(jnp.where(lane_lo, 0.0, tp).astype(BF16), rhs)
                for hh, res, sl in ((ha, ra, sa), (hb, rb, sb)):
                    u_ref[c, d * nh + hh] = res[:, 0:HEAD_DIM_B]
                    wq_ref[c, d * nh + hh] = jnp.concatenate(
                        [res[:, HEAD_DIM_B:2 * HEAD_DIM_B], st["qhead"][:, sl]], axis=0).astype(BF16)
                kt_t = jnp.concatenate([st["ktail"][:, sa], st["ktail"][:, sb]], axis=0).T
                qkp = st["qk"][:, pr * LANES:(pr + 1) * LANES]
                l2_ref[c, d * 2 + pr] = jnp.concatenate(
                    [jnp.where(lane_lo, qkp, 0.0), jnp.where(lane_lo, 0.0, qkp),
                     jnp.where(lane_lo2, kt_t, 0.0), jnp.where(lane_lo2, 0.0, kt_t)], axis=0).astype(BF16)
            egl_ref[c, d] = jnp.concatenate(
                [st["egl"][0:1, hh * HEAD_DIM_B:(hh + 1) * HEAD_DIM_B] for hh in range(nh)]
                + [jnp.zeros((SUBLANES - nh, HEAD_DIM_B), F32)], axis=0)


def _gdn_expanders():
    nh, cs = N_HEADS_B, CHUNK_B
    e64 = np.zeros((2, LANES, nh * cs), np.float32)
    eb = np.zeros((2, LANES, nh * HEAD_DIM_B), np.float32)
    eg = np.zeros((2, LANES, nh * HEAD_DIM_B), np.float32)
    for d in range(2):
        for hh in range(nh):
            p = d * nh + hh
            e64[d, 2 * nh + p, hh * cs:(hh + 1) * cs] = 1.0
            eb[d, p, hh * HEAD_DIM_B:(hh + 1) * HEAD_DIM_B] = 1.0
            eg[d, 2 * nh + p, hh * HEAD_DIM_B:(hh + 1) * HEAD_DIM_B] = 1.0
    return jnp.asarray(e64, BF16), jnp.asarray(eb, BF16), jnp.asarray(eg, BF16)


def _gdn_chunk(qn, kn, v, bg, expanders):
    n = qn.shape[0]
    cs, nh = CHUNK_B, N_HEADS_B
    nc = n // cs
    g = GDN_CHUNKS_PER_STEP
    assert nc % g == 0
    e64, eb, eg = expanders
    const3 = lambda c: (0, 0, 0)
    return pl.pallas_call(
        _gdn_chunk_kernel,
        grid=(nc // g,),
        in_specs=[
            pl.BlockSpec((g * cs, D_B), lambda c: (c, 0)), pl.BlockSpec((g * cs, D_B), lambda c: (c, 0)),
            pl.BlockSpec((g * cs, D_B), lambda c: (c, 0)), pl.BlockSpec((g * cs, LANES), lambda c: (c, 0)),
            pl.BlockSpec(e64.shape, const3), pl.BlockSpec(eb.shape, const3), pl.BlockSpec(eg.shape, const3),
        ],
        out_specs=[
            pl.BlockSpec((g, 2 * nh, 2 * cs, HEAD_DIM_B), lambda c: (c, 0, 0, 0)),
            pl.BlockSpec((g, 2 * nh, cs, HEAD_DIM_B), lambda c: (c, 0, 0, 0)),
            pl.BlockSpec((g, 4, 6 * cs, LANES), lambda c: (c, 0, 0, 0)),
            pl.BlockSpec((g, 2, SUBLANES, HEAD_DIM_B), lambda c: (c, 0, 0, 0)),
        ],
        out_shape=[
            jax.ShapeDtypeStruct((nc, 2 * nh, 2 * cs, HEAD_DIM_B), BF16),
            jax.ShapeDtypeStruct((nc, 2 * nh, cs, HEAD_DIM_B), F32),
            jax.ShapeDtypeStruct((nc, 4, 6 * cs, LANES), BF16),
            jax.ShapeDtypeStruct((nc, 2, SUBLANES, HEAD_DIM_B), F32),
        ],
        compiler_params=_cparams("parallel"),
        name="gdn_chunk",
    )(qn, kn, v, bg, e64, eb, eg)


def _gdn_rec_kernel(wqf_ref, uf_ref, l2f_ref, egf_ref, wqb_ref, ub_ref, l2b_ref, egb_ref,
                    of_ref, ob_ref, s_scr):
    cs, nh = CHUNK_B, N_HEADS_B

    @pl.when(pl.program_id(0) == 0)
    def _():
        s_scr[...] = jnp.zeros_like(s_scr)

    dirs = ((wqf_ref, uf_ref, l2f_ref, egf_ref, of_ref), (wqb_ref, ub_ref, l2b_ref, egb_ref, ob_ref))
    for step in range(GDN_REC_CHUNKS_PER_STEP):
        for d in range(2):
            wq_ref, u_ref, l2_ref, eg_ref, o_ref = dirs[d]
            c = step if d == 0 else GDN_REC_CHUNKS_PER_STEP - 1 - step
            for pr in range(2):
                v_new, o_state = [], []
                for e in range(2):
                    hh = 2 * pr + e
                    r = _dot(wq_ref[c, hh], s_scr[d * nh + hh].astype(BF16))
                    v_new.append(u_ref[c, hh] - r[0:cs])
                    o_state.append(r[cs:2 * cs])
                r2 = _dot(l2_ref[c, pr], jnp.concatenate(v_new, axis=0).astype(BF16))
                for e in range(2):
                    hh = 2 * pr + e
                    o_ref[c * cs:(c + 1) * cs, hh * HEAD_DIM_B:(hh + 1) * HEAD_DIM_B] = (
                        o_state[e] + r2[e * cs:(e + 1) * cs])
                    s_scr[d * nh + hh] = (s_scr[d * nh + hh] * eg_ref[c, 0, hh:hh + 1, :]
                                          + r2[2 * cs + e * 2 * cs:2 * cs + (e + 1) * 2 * cs])


def _gdn_rec(wq, u, l2, egl, lc):
    cs, nh = CHUNK_B, N_HEADS_B
    g = GDN_REC_CHUNKS_PER_STEP
    assert wq.shape[0] % g == 0 and (lc // cs) % g == 0
    nc = wq.shape[0] // g
    ncc = lc // cs // g
    fwd = lambda i: i
    bwd = lambda i: jnp.where(i < ncc, ncc - 1 - i, nc - 1 - (i - ncc))
    specs = []
    for d, order in ((0, fwd), (1, bwd)):
        specs += [
            pl.BlockSpec((g, nh, 2 * cs, HEAD_DIM_B), lambda i, d=d, order=order: (order(i), d, 0, 0)),
            pl.BlockSpec((g, nh, cs, HEAD_DIM_B), lambda i, d=d, order=order: (order(i), d, 0, 0)),
            pl.BlockSpec((g, 2, 6 * cs, LANES), lambda i, d=d, order=order: (order(i), d, 0, 0)),
            pl.BlockSpec((g, 1, SUBLANES, HEAD_DIM_B), lambda i, d=d, order=order: (order(i), d, 0, 0)),
        ]
    return pl.pallas_call(
        _gdn_rec_kernel,
        grid=(nc,),
        in_specs=specs,
        out_specs=[pl.BlockSpec((g * cs, D_B), lambda i: (fwd(i), 0)),
                   pl.BlockSpec((g * cs, D_B), lambda i: (bwd(i), 0))],
        out_shape=[jax.ShapeDtypeStruct((nc * g * cs, D_B), F32)] * 2,
        scratch_shapes=[pltpu.VMEM((2 * nh, HEAD_DIM_B, HEAD_DIM_B), F32)],
        compiler_params=_cparams("arbitrary"),
        name="gdn_rec",
    )(wq, u, l2, egl, wq, u, l2, egl)


def _outproj_kernel(yaf_ref, yab_ref, obf_ref, obb_ref, octx_ref, olat_ref, ga_ref, gb_ref, gc_ref, h_ref,
                    mod_ref, on_ref, w_ref, gn_ref, modn_ref, *out_refs, lc, tm, row_off, last):
    row0 = (pl.program_id(0) + row_off) * tm
    ya = jnp.concatenate([yaf_ref[cg] + yab_ref[cg] for cg in range(D_A // LANES)], axis=1)
    mix_a = (ya * _silu(ga_ref[...])).astype(BF16)
    proj = _dot(mix_a, w_ref[0, 0:D_A, :])
    ob = obf_ref[...] + obb_ref[...]
    gb = _silu(gb_ref[...])
    mix_b = []
    for hd in range(N_HEADS_B):
        sl = slice(hd * HEAD_DIM_B, (hd + 1) * HEAD_DIM_B)
        x = ob[:, sl]
        mix_b.append(x * lax.rsqrt(jnp.mean(x * x, axis=-1, keepdims=True) + EPS) * on_ref[...] * gb[:, sl])
    proj = proj + _dot(jnp.concatenate(mix_b, axis=1).astype(BF16), w_ref[0, D_A:D_A + D_B, :])
    oc = olat_ref[...] if last else jnp.where(row0 < lc, octx_ref[...], olat_ref[...])
    mix_c = (oc * _silu(gc_ref[...])).astype(BF16)
    proj = proj + _dot(mix_c, w_ref[0, D_A + D_B:D_MIX, :])
    row = row0 + lax.broadcasted_iota(jnp.int32, (tm, 1), 0)
    gate = jnp.where(row < lc, mod_ref[1:2, 2 * D_MODEL:3 * D_MODEL], mod_ref[0:1, 2 * D_MODEL:3 * D_MODEL])
    h_new = h_ref[...] + gate * proj
    if last:
        out_refs[0][...] = (h_new * lax.rsqrt(jnp.mean(h_new * h_new, axis=-1, keepdims=True) + EPS)
                            * gn_ref[...])
    else:
        out_refs[0][...] = h_new
        out_refs[1][...] = _adaln(h_new, row0, gn_ref, modn_ref, lc)


def _outproj(ya_f, ya_b, ob_f, ob_b, oc_ctx, oc_lat, z, h, mod_l, onorm_l, w_out_r, layer, gain_next,
             mod_next, lc, last):
    n = h.shape[0]
    tm = ATTN_TILE
    assert lc % tm == 0
    n_ctx = lc // tm
    row_off = n_ctx if last else 0
    rows = lambda width, col: pl.BlockSpec((tm, width), lambda i: (i + row_off, col // width))
    const2 = lambda i: (0, 0)
    out_block = pl.BlockSpec((tm, D_MODEL), lambda i: (i, 0))
    if last:
        out_specs = [out_block]
        out_shape = [jax.ShapeDtypeStruct((n - lc, D_MODEL), F32)]
    else:
        out_specs = [out_block, out_block]
        out_shape = [jax.ShapeDtypeStruct((n, D_MODEL), F32), jax.ShapeDtypeStruct((n, D_MODEL), BF16)]
    return pl.pallas_call(
        functools.partial(_outproj_kernel, lc=lc, tm=tm, row_off=row_off, last=last),
        grid=((n - row_off * tm) // tm,),
        in_specs=[
            pl.BlockSpec((D_A // LANES, tm, LANES), lambda i: (0, i + row_off, 0)),
            pl.BlockSpec((D_A // LANES, tm, LANES), lambda i: (0, i + row_off, 0)),
            rows(D_B, 0), rows(D_B, 0),
            pl.BlockSpec((tm, D_Q_C), lambda i: (jnp.minimum(i + row_off, n_ctx - 1), 0)),
            pl.BlockSpec((tm, D_Q_C), lambda i: (jnp.maximum(i + row_off - n_ctx, 0), 0)),
            rows(D_A, COL_GA), rows(D_B, COL_GB), rows(D_Q_C, COL_GC), rows(D_MODEL, 0),
            pl.BlockSpec((SUBLANES, 3 * D_MODEL), const2), pl.BlockSpec((1, HEAD_DIM_B), const2),
            pl.BlockSpec((1, D_MIX, D_MODEL), lambda i: (layer, 0, 0)),
            pl.BlockSpec((1, D_MODEL), const2), pl.BlockSpec((SUBLANES, 3 * D_MODEL), const2),
        ],
        out_specs=out_specs,
        out_shape=out_shape,
        compiler_params=_cparams("parallel"),
        name="outproj_last" if last else "outproj",
    )(ya_f, ya_b, ob_f, ob_b, oc_ctx, oc_lat, z, z, z, h, mod_l, onorm_l, w_out_r, gain_next, mod_next)


def _rope_tables(t_len, lc):
    rows = t_len // GRID_W
    row = jnp.repeat(jnp.arange(rows, dtype=jnp.int32), GRID_W)
    col = jnp.tile(jnp.arange(GRID_W, dtype=jnp.int32), rows)
    n_freq = HEAD_DIM_C // 4
    inv_freq = ROPE_BASE ** (-jnp.arange(n_freq, dtype=F32) / n_freq)
    ang = jnp.concatenate([row.astype(F32)[:, None] * inv_freq, col.astype(F32)[:, None] * inv_freq], axis=-1)
    cos, sin = jnp.cos(ang), jnp.sin(ang)
    half = HEAD_DIM_C // 2
    cos2 = jnp.concatenate([jnp.ones((lc, 2 * half), F32), jnp.concatenate([cos, cos], axis=-1)], axis=0)
    sin2 = jnp.concatenate([jnp.zeros((lc, 2 * half), F32), jnp.concatenate([-sin, sin], axis=-1)], axis=0)
    return cos2, sin2


def _relayout_w_in_kernel(w_ref, o_ref):
    n_ba = 4 * N_HEADS_B
    src_ba = 2 * D_A + 4 * D_B
    src_qc = src_ba + n_ba
    copies = ((0, COL_XA, D_A), (D_A, COL_GA, D_A), (2 * D_A, COL_QKVB, 3 * D_B),
              (2 * D_A + 3 * D_B, COL_GB, D_B), (src_ba, COL_BA, n_ba),
              (src_qc, COL_QC, D_Q_C), (src_qc + D_Q_C, COL_KV, 2 * D_KV_C),
              (src_qc + D_Q_C + 2 * D_KV_C, COL_GC, D_Q_C))
    for src, dst, rows in copies:
        o_ref[0, dst:dst + rows, :] = w_ref[0, src:src + rows, :].astype(BF16)
    o_ref[0, COL_BA + n_ba:D_Z, :] = jnp.zeros((D_Z - COL_BA - n_ba, o_ref.shape[2]), BF16)


def _relayout_w_in(w_in):
    depth, d, d_in = w_in.shape
    tc = 512
    return pl.pallas_call(
        _relayout_w_in_kernel,
        grid=(depth, d // tc),
        in_specs=[pl.BlockSpec((1, d_in, tc), lambda l, i: (l, 0, i))],
        out_specs=pl.BlockSpec((1, D_Z, tc), lambda l, i: (l, 0, i)),
        out_shape=jax.ShapeDtypeStruct((depth, D_Z, d), BF16),
        compiler_params=_cparams("parallel", "parallel"),
        name="relayout_w_in",
    )(jnp.swapaxes(w_in, 1, 2))


def _block_diag(w):
    eye = jnp.eye(N_BLK_A, dtype=w.dtype)
    out = jnp.einsum('...nij,nm->...nimj', w, eye)
    return out.reshape(w.shape[:-3] + (D_A, D_A))


def kernel(x, c, ctx, c_ctx, norm_g, w_mod, b_mod, w_in, conv_a_w, conv_a_b, w_ra, b_ra, w_ia, b_ia, lam_a,
           conv_b_w, a_log_b, dt_bias_b, onorm_b, qn_c, kn_c, w_out, final_g):
    bsz, t_len, d = x.shape
    lc = ctx.shape[1]
    depth = w_in.shape[0]
    assert bsz == 1 and d == D_MODEL and t_len % GRID_W == 0

    cc = jnp.concatenate([c, c_ctx[None, :], jnp.zeros((SUBLANES - 2, d), F32)], axis=0)
    mod = _modulation(cc, w_mod, b_mod)

    w_in_r = _relayout_w_in(w_in)
    w_out_r = w_out.astype(BF16)
    wg = jnp.concatenate([_block_diag(w_ra), _block_diag(w_ia)], axis=-1).astype(BF16)
    bgate = jnp.concatenate([b_ra, b_ia], axis=-1)[:, :, None, :]
    nh2 = 2 * N_HEADS_B
    lane_pad = lambda v: jnp.pad(v.reshape(depth, 1, nh2), ((0, 0), (0, 0), (nh2, LANES - 2 * nh2)))
    avec = lane_pad(a_log_b)
    dtvec = lane_pad(dt_bias_b)
    cos2, sin2 = _rope_tables(t_len, lc)
    expanders = _gdn_expanders()

    h, y = _prenorm(ctx, x, mod[0], norm_g[0][None, :])
    for l in range(depth):
        last = l == depth - 1
        z = _inproj(y, w_in_r, l)
        qh, kh, vt = _attn_prep(z, cos2, sin2, qn_c[l][None, :], kn_c[l][None, :])
        oc_lat = _attention(qh, kh, vt, lc, latent=True)
        oc_ctx = oc_lat if last else _attention(qh, kh, vt, lc, latent=False)
        ya_f, ya_b = _rglru(z, conv_a_w[l], conv_a_b[l][None, :], wg[l], bgate[l], lam_a[l], lc)
        qn, kn, v, bg = _gdn_prep(z, conv_b_w[l], avec[l], dtvec[l], lc)
        wq, u, l2, egl = _gdn_chunk(qn, kn, v, bg, expanders)
        ob_f, ob_b = _gdn_rec(wq, u, l2, egl, lc)
        gain_next = final_g[None, :] if last else norm_g[l + 1][None, :]
        mod_next = mod[l] if last else mod[l + 1]
        outs = _outproj(ya_f, ya_b, ob_f, ob_b, oc_ctx, oc_lat, z, h, mod[l], onorm_b[l][None, :], w_out_r, l,
                        gain_next, mod_next, lc, last)
        if last:
            return outs[0][None]
        h, y = outs
```

```python
import functools
import math

import numpy as np
import jax
import jax.numpy as jnp
from jax import lax
from jax.experimental import pallas as pl
from jax.experimental.pallas import tpu as pltpu

F32 = jnp.float32
BF16 = jnp.bfloat16

D_MODEL = 2048
GRID_W = 64
EPS = 1e-6
ROPE_BASE = 10000.0

D_A = 512
N_BLK_A = 8
BLK_A = D_A // N_BLK_A
LRU_C = 8.0

N_HEADS_B = 4
HEAD_DIM_B = 128
D_B = N_HEADS_B * HEAD_DIM_B
CHUNK_B = 64

N_Q_HEADS_C = 8
N_KV_HEADS_C = 2
GROUP_C = N_Q_HEADS_C // N_KV_HEADS_C
HEAD_DIM_C = 128
D_Q_C = N_Q_HEADS_C * HEAD_DIM_C
D_KV_C = N_KV_HEADS_C * HEAD_DIM_C

D_MIX = D_A + D_B + D_Q_C

COL_QC = 0
COL_GC = 1024
COL_KV = 2048
COL_XA = 2560
COL_QKVB = 3072
COL_GA = 4608
COL_GB = 5120
COL_BA = 5632
D_Z = 5760
LANES = 128
SUBLANES = 8
LOG2_E = math.log2(math.e)
ATTN_TILE = 256
V_ROWS = HEAD_DIM_C + 16
NEG_BIG = -1e30
GDN_REC_CHUNKS_PER_STEP = 4
GDN_CHUNKS_PER_STEP = 4

VMEM_LIMIT = 56 * 1024 * 1024


def _cparams(*sem):
    return pltpu.CompilerParams(dimension_semantics=tuple(sem), vmem_limit_bytes=VMEM_LIMIT)


def _dot(a, b):
    return jnp.dot(a, b, preferred_element_type=F32)


def _dot_nt(a, b):
    return lax.dot_general(a, b, (((1,), (1,)), ((), ())), preferred_element_type=F32)


def _split2(a):
    hi = a.astype(BF16)
    lo = (a - hi.astype(F32)).astype(BF16)
    return hi, lo


def _split3(a):
    hi = a.astype(BF16)
    r = a - hi.astype(F32)
    mid = r.astype(BF16)
    lo = (r - mid.astype(F32)).astype(BF16)
    return hi, mid, lo


def _dot3(a, b):
    ah, al = _split2(a)
    bh, bl = _split2(b)
    return _dot(ah, bh) + (_dot(al, bh) + _dot(ah, bl))


def _silu(x):
    return x * jax.nn.sigmoid(x)


def _softplus(x):
    return jnp.maximum(x, 0.0) + jnp.log1p(jnp.exp(-jnp.abs(x)))


def _pick(n, cands):
    for c in cands:
        if n % c == 0:
            return c
    raise ValueError(f"no tile for {n} in {cands}")


def _mod_kernel(c_ref, w_ref, b_ref, o_ref):
    o_ref[0] = _dot3(_silu(c_ref[...]), w_ref[0]) + b_ref[0]


def _modulation(cc, w_mod, b_mod):
    depth, d, d3 = w_mod.shape
    tn = 768
    return pl.pallas_call(
        _mod_kernel,
        grid=(depth, d3 // tn),
        in_specs=[
            pl.BlockSpec((SUBLANES, d), lambda l, j: (0, 0)),
            pl.BlockSpec((1, d, tn), lambda l, j: (l, 0, j)),
            pl.BlockSpec((1, 1, tn), lambda l, j: (l, 0, j)),
        ],
        out_specs=pl.BlockSpec((1, SUBLANES, tn), lambda l, j: (l, 0, j)),
        out_shape=jax.ShapeDtypeStruct((depth, SUBLANES, d3), F32),
        compiler_params=_cparams("parallel", "parallel"),
        name="modulation",
    )(cc, w_mod, b_mod.reshape(depth, 1, d3))


def _adaln(h_new, row0, g_ref, mod_ref, lc):
    xn = h_new * lax.rsqrt(jnp.mean(h_new * h_new, axis=-1, keepdims=True) + EPS) * g_ref[...]
    row = row0 + lax.broadcasted_iota(jnp.int32, (h_new.shape[0], 1), 0)
    is_ctx = row < lc
    shift = jnp.where(is_ctx, mod_ref[1:2, 0:D_MODEL], mod_ref[0:1, 0:D_MODEL])
    scale = jnp.where(is_ctx, mod_ref[1:2, D_MODEL:2 * D_MODEL], mod_ref[0:1, D_MODEL:2 * D_MODEL])
    return (xn * (1.0 + scale) + shift).astype(BF16)


def _prenorm_kernel(ctx_ref, x_ref, mod_ref, g_ref, h_ref, y_ref, *, lc, tm):
    row0 = pl.program_id(0) * tm
    h = jnp.where(row0 < lc, ctx_ref[0], x_ref[0])
    h_ref[...] = h
    y_ref[...] = _adaln(h, row0, g_ref, mod_ref, lc)


def _prenorm(ctx, x, mod_l, norm_g_l):
    lc, t_len = ctx.shape[1], x.shape[1]
    tm = 256
    assert lc % tm == 0 and t_len % tm == 0
    n_ctx = lc // tm
    n = lc + t_len
    out_block = pl.BlockSpec((tm, D_MODEL), lambda i: (i, 0))
    return pl.pallas_call(
        functools.partial(_prenorm_kernel, lc=lc, tm=tm),
        grid=(n // tm,),
        in_specs=[
            pl.BlockSpec((1, tm, D_MODEL), lambda i: (0, jnp.minimum(i, n_ctx - 1), 0)),
            pl.BlockSpec((1, tm, D_MODEL), lambda i: (0, jnp.maximum(i - n_ctx, 0), 0)),
            pl.BlockSpec((SUBLANES, 3 * D_MODEL), lambda i: (0, 0)),
            pl.BlockSpec((1, D_MODEL), lambda i: (0, 0)),
        ],
        out_specs=[out_block, out_block],
        out_shape=[jax.ShapeDtypeStruct((n, D_MODEL), F32), jax.ShapeDtypeStruct((n, D_MODEL), BF16)],
        compiler_params=_cparams("parallel"),
        name="prenorm",
    )(ctx, x, mod_l, norm_g_l)


def _inproj_kernel(y_ref, wt_ref, z_ref):
    z_ref[...] = _dot_nt(y_ref[...], wt_ref[0])


def _inproj(y, w_in_t, layer):
    n = y.shape[0]
    tm = _pick(n, (1408, 768, 640, 512, 256))
    tn = 1920
    return pl.pallas_call(
        _inproj_kernel,
        grid=(n // tm, D_Z // tn),
        in_specs=[
            pl.BlockSpec((tm, D_MODEL), lambda i, j: (i, 0)),
            pl.BlockSpec((1, tn, D_MODEL), lambda i, j: (layer, j, 0)),
        ],
        out_specs=pl.BlockSpec((tm, tn), lambda i, j: (i, j)),
        out_shape=jax.ShapeDtypeStruct((n, D_Z), F32),
        compiler_params=_cparams("parallel", "parallel"),
        name="inproj",
    )(y, w_in_t)


def _attn_prep_kernel(q_ref, kv_ref, cos_ref, sin_ref, qn_ref, kn_ref, qo_ref, ko_ref, vo_ref):
    cos = cos_ref[...]
    sin = sin_ref[...]

    def prep(x, gain, scale):
        y = x * lax.rsqrt(jnp.mean(x * x, axis=-1, keepdims=True) + EPS) * gain
        rot = pltpu.roll(y, HEAD_DIM_C // 2, 1)
        return (y * cos + rot * sin) * scale

    for hd in range(N_Q_HEADS_C):
        sl = slice(hd * HEAD_DIM_C, (hd + 1) * HEAD_DIM_C)
        qo_ref[:, sl] = prep(q_ref[:, sl], qn_ref[...], HEAD_DIM_C ** -0.5 * LOG2_E).astype(BF16)
    for hd in range(N_KV_HEADS_C):
        sl = slice(hd * HEAD_DIM_C, (hd + 1) * HEAD_DIM_C)
        ko_ref[:, sl] = prep(kv_ref[:, sl], kn_ref[...], 1.0).astype(BF16)
    vt = kv_ref[:, D_KV_C:2 * D_KV_C].T
    ones = jnp.ones((V_ROWS - HEAD_DIM_C, vt.shape[1]), F32)
    for hd in range(N_KV_HEADS_C):
        vo_ref[0, hd] = jnp.concatenate([vt[hd * HEAD_DIM_C:(hd + 1) * HEAD_DIM_C], ones], axis=0).astype(BF16)


def _attn_prep(z, cos2, sin2, qn_l, kn_l):
    n = z.shape[0]
    tm = ATTN_TILE
    return pl.pallas_call(
        _attn_prep_kernel,
        grid=(n // tm,),
        in_specs=[
            pl.BlockSpec((tm, D_Q_C), lambda i: (i, COL_QC // D_Q_C)),
            pl.BlockSpec((tm, 2 * D_KV_C), lambda i: (i, COL_KV // (2 * D_KV_C))),
            pl.BlockSpec((tm, HEAD_DIM_C), lambda i: (i, 0)),
            pl.BlockSpec((tm, HEAD_DIM_C), lambda i: (i, 0)),
            pl.BlockSpec((1, HEAD_DIM_C), lambda i: (0, 0)),
            pl.BlockSpec((1, HEAD_DIM_C), lambda i: (0, 0)),
        ],
        out_specs=[
            pl.BlockSpec((tm, D_Q_C), lambda i: (i, 0)),
            pl.BlockSpec((tm, D_KV_C), lambda i: (i, 0)),
            pl.BlockSpec((1, N_KV_HEADS_C, V_ROWS, tm), lambda i: (i, 0, 0, 0)),
        ],
        out_shape=[
            jax.ShapeDtypeStruct((n, D_Q_C), BF16),
            jax.ShapeDtypeStruct((n, D_KV_C), BF16),
            jax.ShapeDtypeStruct((n // tm, N_KV_HEADS_C, V_ROWS, tm), BF16),
        ],
        compiler_params=_cparams("parallel"),
        name="attn_prep",
    )(z, z, cos2, sin2, qn_l, kn_l)


def _attn_kernel(q_ref, k_ref, vt_ref, o_ref, s0_scr, sa_scr, sb_scr, acc_scr, *, lc, tq, tk, n_super):
    nb = lc // tk

    def scores(row0, nrows, s_scr):
        kb = k_ref[pl.ds(row0, nrows), :]
        for g in range(GROUP_C):
            s_scr[g] = _dot_nt(kb, q_ref[:, g * HEAD_DIM_C:(g + 1) * HEAD_DIM_C])

    def softmax_pv(s_scr, blk0, n_blk, ms):
        vts = [vt_ref[blk0 + b, 0] for b in range(n_blk)]
        new_m = []
        for g in range(GROUP_C):
            st = s_scr[g]
            m_blk = jnp.max(st, axis=0, keepdims=True)
            m_new = m_blk if ms is None else jnp.maximum(ms[g], m_blk)
            p = jnp.exp2(st - m_new).astype(BF16)
            pv = _dot(vts[0], p[0:tk])
            for b in range(1, n_blk):
                pv = pv + _dot(vts[b], p[b * tk:(b + 1) * tk])
            acc_scr[g] = pv if ms is None else jnp.exp2(ms[g] - m_new) * acc_scr[g] + pv
            new_m.append(m_new)
        return tuple(new_m)

    def super_rows(j):
        return pl.multiple_of(lc + j * (2 * tk), tk)

    if n_super:
        def pair(j, ms):
            scores(super_rows(j + 1), 2 * tk, sb_scr)
            ms = softmax_pv(sa_scr, nb + 2 * j, 2, ms)
            scores(super_rows(j + 2), 2 * tk, sa_scr)
            return softmax_pv(sb_scr, nb + 2 * j + 2, 2, ms)

        scores(super_rows(0), 2 * tk, sa_scr)
        ms = None
        if n_super >= 4:
            ms = pair(0, None)
            middle = n_super // 2 - 2
            ms = lax.fori_loop(0, middle // 2, lambda i, ms: pair(4 * i + 4, pair(4 * i + 2, ms)), ms)
            if middle % 2:
                ms = pair(n_super - 4, ms)
        scores(super_rows(n_super - 1), 2 * tk, sb_scr)
        ms = softmax_pv(sa_scr, nb + 2 * (n_super - 2), 2, ms)
        scores(0, lc, s0_scr)
        ms = softmax_pv(sb_scr, nb + 2 * (n_super - 1), 2, ms)
        softmax_pv(s0_scr, 0, nb, ms)
    else:
        scores(0, lc, s0_scr)
        softmax_pv(s0_scr, 0, nb, None)

    for g in range(GROUP_C):
        acc = acc_scr[g]
        o_ref[:, g * HEAD_DIM_C:(g + 1) * HEAD_DIM_C] = (
            acc[0:HEAD_DIM_C] / acc[HEAD_DIM_C:HEAD_DIM_C + 1]).T


def _attention(qh, kh, vt, lc, latent):
    n = qh.shape[0]
    tq = tk = ATTN_TILE
    assert lc % tq == 0 and n % tq == 0
    if latent:
        n_keys, n_q, q_off = n, n - lc, lc // tq
        n_super = (n - lc) // (2 * tk)
        assert n_super * 2 * tk == n - lc and n_super % 2 == 0
        super_shape = (GROUP_C, 2 * tk, tq)
    else:
        n_keys, n_q, q_off, n_super = lc, lc, 0, 0
        super_shape = (1, SUBLANES, LANES)
    gw = GROUP_C * HEAD_DIM_C
    return pl.pallas_call(
        functools.partial(_attn_kernel, lc=lc, tq=tq, tk=tk, n_super=n_super),
        grid=(N_KV_HEADS_C, n_q // tq),
        in_specs=[
            pl.BlockSpec((tq, gw), lambda j, i: (i + q_off, j)),
            pl.BlockSpec((n_keys, HEAD_DIM_C), lambda j, i: (0, j)),
            pl.BlockSpec((n_keys // tk, 1, V_ROWS, tk), lambda j, i: (0, j, 0, 0)),
        ],
        out_specs=pl.BlockSpec((tq, gw), lambda j, i: (i, j)),
        out_shape=jax.ShapeDtypeStruct((n_q, D_Q_C), F32),
        scratch_shapes=[pltpu.VMEM((GROUP_C, lc, tq), F32),
                        pltpu.VMEM(super_shape, F32), pltpu.VMEM(super_shape, F32),
                        pltpu.VMEM((GROUP_C, V_ROWS, tq), F32)],
        compiler_params=_cparams("parallel", "parallel"),
        name="attention" if latent else "attention_ctx",
    )(qh, kh, vt)


def _conv4(x, prev8, next8, w_ref, prev_ok, next_ok):
    tm = x.shape[0]
    row = lax.broadcasted_iota(jnp.int32, (tm, 1), 0)
    p = jnp.where(prev_ok, prev8, 0.0)
    nx = jnp.where(next_ok, next8, 0.0)
    xm1 = jnp.where(row == 0, p[7:8], pltpu.roll(x, 1, 0))
    xm2 = jnp.where(row == 0, p[6:7], jnp.where(row == 1, p[7:8], pltpu.roll(x, 2, 0)))
    xp1 = jnp.where(row == tm - 1, nx[0:1], pltpu.roll(x, tm - 1, 0))
    return w_ref[0:1] * xm2 + w_ref[1:2] * xm1 + w_ref[2:3] * x + w_ref[3:4] * xp1


def _seg_flags(t, nbc, nt):
    prev_ok = jnp.logical_and(t != 0, t != nbc)
    next_ok = jnp.logical_and(t != nbc - 1, t != nt - 1)
    return prev_ok, next_ok


def _halo_maps(tile_of, tm, n, col_block):
    r = tm // SUBLANES
    last = n // SUBLANES - 1
    main = lambda i: (tile_of(i), col_block)
    prev = lambda i: (jnp.maximum(tile_of(i) * r - 1, 0), col_block)
    nxt = lambda i: (jnp.minimum((tile_of(i) + 1) * r, last), col_block)
    return main, prev, nxt


def _rglru_kernel(xf_ref, pf_ref, nf_ref, xb_ref, pb_ref, nb_ref, cw_ref, cb_ref, wg_ref, bg_ref,
                  lam_ref, of_ref, ob_ref, x_scr, hcar, *, tm, nbc, nt):
    i = pl.program_id(0)
    seg = tm // SUBLANES
    ncg = D_A // LANES
    sub = lax.broadcasted_iota(jnp.int32, (SUBLANES, D_A), 0)

    @pl.when(i == 0)
    def _():
        hcar[...] = jnp.zeros_like(hcar)

    tiles = (i, jnp.where(i < nbc, nbc - 1 - i, nt - 1 - (i - nbc)))
    refs = ((xf_ref, pf_ref, nf_ref, of_ref), (xb_ref, pb_ref, nb_ref, ob_ref))
    for d in range(2):
        x_ref, p_ref, n_ref, o_ref = refs[d]
        prev_ok, next_ok = _seg_flags(tiles[d], nbc, nt)
        p = jnp.where(prev_ok, p_ref[...], 0.0)
        nx = jnp.where(next_ok, n_ref[...], 0.0)
        for cg in range(ncg):
            x_scr[cg] = x_ref[:, cg * LANES:(cg + 1) * LANES]
        xg = [jnp.concatenate([x_scr[cg, pl.ds(j, SUBLANES, stride=seg), :] for cg in range(ncg)], axis=1)
              for j in range(seg)]
        before1 = jnp.where(sub == 0, p[7:8], pltpu.roll(xg[seg - 1], 1, 0))
        before2 = jnp.where(sub == 0, p[6:7], pltpu.roll(xg[seg - 2], 1, 0))
        after1 = jnp.where(sub == SUBLANES - 1, nx[0:1], pltpu.roll(xg[0], SUBLANES - 1, 0))
        xp = jnp.concatenate(xg, axis=0)
        xm1 = jnp.concatenate([before1] + xg[:seg - 1], axis=0)
        xm2 = jnp.concatenate([before2, before1] + xg[:seg - 2], axis=0)
        xp1 = jnp.concatenate(xg[1:] + [after1], axis=0)
        u = (cw_ref[0:1] * xm2 + cw_ref[1:2] * xm1 + cw_ref[2:3] * xp + cw_ref[3:4] * xp1) + cb_ref[...]
        gates = _dot(u.astype(BF16), wg_ref[d]) + bg_ref[d]
        gate_r = jax.nn.sigmoid(gates[:, 0:D_A])
        gate_i = jax.nn.sigmoid(gates[:, D_A:2 * D_A])
        log_a = (-LRU_C) * gate_r * _softplus(-lam_ref[d:d + 1, :])
        a = jnp.exp(log_a)
        b = jnp.sqrt(-jnp.tanh(log_a) * (a * a + 1.0)) * (gate_i * u)

        order = list(range(seg)) if d == 0 else list(range(seg - 1, -1, -1))
        hloc = jnp.zeros((SUBLANES, D_A), F32)
        pcum = jnp.ones((SUBLANES, D_A), F32)
        hs, ps = [None] * seg, [None] * seg
        for j in order:
            aj = a[j * SUBLANES:(j + 1) * SUBLANES]
            hloc = aj * hloc + b[j * SUBLANES:(j + 1) * SUBLANES]
            pcum = aj * pcum
            hs[j], ps[j] = hloc, pcum
        carry = hcar[d:d + 1, :]
        seg_in = [None] * SUBLANES
        for s in (range(SUBLANES) if d == 0 else range(SUBLANES - 1, -1, -1)):
            seg_in[s] = carry
            carry = pcum[s:s + 1] * carry + hloc[s:s + 1]
        hcar[d:d + 1, :] = carry
        cin = jnp.concatenate(seg_in, axis=0)
        for j in range(seg):
            hj = hs[j] + ps[j] * cin
            for cg in range(ncg):
                o_ref[cg, pl.ds(j, SUBLANES, stride=seg), :] = hj[:, cg * LANES:(cg + 1) * LANES]


def _rglru(z, cw, cb, wg, bg, lam, lc):
    n = z.shape[0]
    tm = 256
    assert lc % tm == 0 and n % tm == 0
    nbc, nt = lc // tm, n // tm
    fwd = lambda i: i
    bwd = lambda i: jnp.where(i < nbc, nbc - 1 - i, nt - 1 - (i - nbc))
    cblk = COL_XA // D_A
    fm, fp, fn = _halo_maps(fwd, tm, n, cblk)
    bm, bp, bn = _halo_maps(bwd, tm, n, cblk)
    const2 = lambda i: (0, 0)
    const3 = lambda i: (0, 0, 0)
    return pl.pallas_call(
        functools.partial(_rglru_kernel, tm=tm, nbc=nbc, nt=nt),
        grid=(nt,),
        in_specs=[
            pl.BlockSpec((tm, D_A), fm), pl.BlockSpec((SUBLANES, D_A), fp), pl.BlockSpec((SUBLANES, D_A), fn),
            pl.BlockSpec((tm, D_A), bm), pl.BlockSpec((SUBLANES, D_A), bp), pl.BlockSpec((SUBLANES, D_A), bn),
            pl.BlockSpec((4, D_A), const2), pl.BlockSpec((1, D_A), const2),
            pl.BlockSpec((2, D_A, 2 * D_A), const3), pl.BlockSpec((2, 1, 2 * D_A), const3),
            pl.BlockSpec((2, D_A), const2),
        ],
        out_specs=[pl.BlockSpec((D_A // LANES, tm, LANES), lambda i: (0, fwd(i), 0)),
                   pl.BlockSpec((D_A // LANES, tm, LANES), lambda i: (0, bwd(i), 0))],
        out_shape=[jax.ShapeDtypeStruct((D_A // LANES, n, LANES), F32)] * 2,
        scratch_shapes=[pltpu.VMEM((D_A // LANES, tm, LANES), F32), pltpu.VMEM((SUBLANES, D_A), F32)],
        compiler_params=_cparams("arbitrary"),
        name="rglru",
    )(z, z, z, z, z, z, cw, cb, wg, bg, lam)


def _gdn_prep_kernel(x_ref, p_ref, n_ref, ba_ref, cw_ref, av_ref, dt_ref, q_ref, k_ref, v_ref, bg_ref,
                     *, nbc, nt):
    prev_ok, next_ok = _seg_flags(pl.program_id(0), nbc, nt)
    u = _silu(_conv4(x_ref[...], p_ref[...], n_ref[...], cw_ref, prev_ok, next_ok))
    for hd in range(N_HEADS_B):
        sl = slice(hd * HEAD_DIM_B, (hd + 1) * HEAD_DIM_B)
        qv = u[:, hd * HEAD_DIM_B:(hd + 1) * HEAD_DIM_B]
        kv = u[:, D_B + hd * HEAD_DIM_B:D_B + (hd + 1) * HEAD_DIM_B]
        q_ref[:, sl] = qv * lax.rsqrt(jnp.sum(qv * qv, axis=-1, keepdims=True) + EPS)
        k_ref[:, sl] = kv * lax.rsqrt(jnp.sum(kv * kv, axis=-1, keepdims=True) + EPS)
    v_ref[...] = u[:, 2 * D_B:3 * D_B]
    ba = ba_ref[...]
    lane = lax.broadcasted_iota(jnp.int32, ba.shape, 1)
    beta = jax.nn.sigmoid(ba)
    g = -jnp.exp(av_ref[...]) * _softplus(ba + dt_ref[...])
    bg_ref[...] = jnp.where(lane < 2 * N_HEADS_B, beta, jnp.where(lane < 4 * N_HEADS_B, g, 0.0))


def _gdn_prep(z, cw, avec, dtvec, lc):
    n = z.shape[0]
    tm = 256
    nbc, nt = lc // tm, n // tm
    w3 = 3 * D_B
    assert COL_QKVB % w3 == 0
    m, p, nx = _halo_maps(lambda i: i, tm, n, COL_QKVB // w3)
    const2 = lambda i: (0, 0)
    return pl.pallas_call(
        functools.partial(_gdn_prep_kernel, nbc=nbc, nt=nt),
        grid=(nt,),
        in_specs=[
            pl.BlockSpec((tm, w3), m), pl.BlockSpec((SUBLANES, w3), p), pl.BlockSpec((SUBLANES, w3), nx),
            pl.BlockSpec((tm, LANES), lambda i: (i, COL_BA // LANES)),
            pl.BlockSpec((4, w3), const2), pl.BlockSpec((1, LANES), const2), pl.BlockSpec((1, LANES), const2),
        ],
        out_specs=[pl.BlockSpec((tm, D_B), lambda i: (i, 0))] * 3 + [pl.BlockSpec((tm, LANES), lambda i: (i, 0))],
        out_shape=[jax.ShapeDtypeStruct((n, D_B), F32)] * 3 + [jax.ShapeDtypeStruct((n, LANES), F32)],
        compiler_params=_cparams("parallel"),
        name="gdn_prep",
    )(z, z, z, z, cw, avec, dtvec)


def _unit_tri_inverse(mats, row, col, lane_lo):
    def block_diag(y):
        return jnp.concatenate([jnp.where(lane_lo, y, 0.0), jnp.where(lane_lo, 0.0, y)], axis=0).astype(BF16)

    def mm(xs, ys):
        return [_dot(x.astype(BF16), block_diag(y)) for x, y in zip(xs, ys)]

    def same(shift):
        return (row >> shift) == (col >> shift)

    eye = jnp.where(row == col, 1.0, 0.0)
    rows = row.shape[0]
    m8 = [jnp.where(same(3), m, 0.0) for m in mats]
    xs = [eye - m for m in m8]
    pw = mm(m8, m8)
    both = mm([jnp.concatenate([x, p], axis=0) for x, p in zip(xs, pw)], pw)
    xs = [x + b[0:rows] for x, b in zip(xs, both)]
    pw = [b[rows:2 * rows] for b in both]
    xs = [x + y for x, y in zip(xs, mm(xs, pw))]
    for shift in (4, 5, 6):
        off = [jnp.where(same(shift), jnp.where(same(shift - 1), 0.0, m), 0.0) for m in mats]
        xs = [x - y for x, y in zip(xs, mm(xs, mm(off, xs)))]
    return xs


def _gdn_chunk_kernel(q_ref, k_ref, v_ref, bg_ref, e64_ref, eb_ref, eg_ref, wq_ref, u_ref, l2_ref, egl_ref):
    cs = CHUNK_B
    nh = N_HEADS_B
    ri = lax.broadcasted_iota(jnp.int32, (cs, cs), 0)
    ci = lax.broadcasted_iota(jnp.int32, (cs, cs), 1)
    row = lax.broadcasted_iota(jnp.int32, (cs, nh * cs), 0)
    col = lax.broadcasted_iota(jnp.int32, (cs, nh * cs), 1) & (cs - 1)
    lane_lo = lax.broadcasted_iota(jnp.int32, (cs, LANES), 1) < cs
    lane_lo2 = lax.broadcasted_iota(jnp.int32, (2 * cs, LANES), 1) < cs
    zeros = jnp.zeros((cs, HEAD_DIM_B), F32)
    tri = (jnp.where(ci <= ri, 1.0, 0.0).astype(BF16), jnp.where(ci >= ri, 1.0, 0.0).astype(BF16))
    strict = (row > col, row < col)
    causal = (row >= col, row <= col)

    def dot_parts(lhs, parts):
        return _dot(lhs, parts[0]) + (_dot(lhs, parts[1]) + _dot(lhs, parts[2]))

    stage = {}
    mats = []
    for c in range(GDN_CHUNKS_PER_STEP):
        rows = slice(c * cs, (c + 1) * cs)
        qs = q_ref[rows, :] * (HEAD_DIM_B ** -0.5)
        kn = k_ref[rows, :]
        v = v_ref[rows, :]
        bg_parts = _split3(bg_ref[rows, :])
        for d in range(2):
            expand = lambda e: [_dot(part, e).astype(BF16) for part in bg_parts]
            g64 = [jnp.where(strict[d], part, 0.0).astype(BF16) for part in expand(e64_ref[d])]
            decay = jnp.where(causal[d], jnp.exp(dot_parts(tri[d], g64)), 0.0)
            beta_parts = expand(eb_ref[d])
            beta = beta_parts[0].astype(F32) + (beta_parts[1].astype(F32) + beta_parts[2].astype(F32))
            g_parts = expand(eg_ref[d])
            gc = dot_parts(tri[d], g_parts)
            gl = jnp.broadcast_to(gc[cs - 1:cs] if d == 0 else gc[0:1], gc.shape)
            egc = jnp.exp(gc)
            kb = kn * beta
            lhs = jnp.concatenate([kb, qs], axis=0).astype(BF16)
            prods = []
            for pr in range(2):
                ka = kn[:, (2 * pr) * HEAD_DIM_B:(2 * pr + 1) * HEAD_DIM_B]
                kb_ = kn[:, (2 * pr + 1) * HEAD_DIM_B:(2 * pr + 2) * HEAD_DIM_B]
                rhs_t = jnp.concatenate([jnp.concatenate([ka, zeros], axis=1),
                                         jnp.concatenate([zeros, kb_], axis=1)], axis=0).astype(BF16)
                prods.append(_dot_nt(lhs[:, 2 * pr * HEAD_DIM_B:(2 * pr + 2) * HEAD_DIM_B], rhs_t))
            prod = jnp.concatenate(prods, axis=1)
            mat = jnp.where(strict[d], prod[0:cs] * decay, 0.0)
            mats += [mat[:, 0:LANES], mat[:, LANES:2 * LANES]]
            stage[c, d] = dict(qk=prod[cs:2 * cs] * decay, vb=v * beta, kbe=kb * egc, qhead=qs * egc,
                               ktail=kn * jnp.exp(gl - gc), egl=jnp.exp(gl))

    row2 = lax.broadcasted_iota(jnp.int32, (cs, LANES), 0)
    col2 = lax.broadcasted_iota(jnp.int32, (cs, LANES), 1) & (cs - 1)
    tmats = _unit_tri_inverse(mats, row2, col2, lane_lo)

    for c in range(GDN_CHUNKS_PER_STEP):
        for d in range(2):
            st = stage[c, d]
            for pr in range(2):
                ha, hb = 2 * pr, 2 * pr + 1
                sa = slice(ha * HEAD_DIM_B, (ha + 1) * HEAD_DIM_B)
                sb = slice(hb * HEAD_DIM_B, (hb + 1) * HEAD_DIM_B)
                tp = tmats[(c * 2 + d) * 2 + pr]
                rhs = jnp.concatenate([jnp.concatenate([st["vb"][:, sa], st["kbe"][:, sa]], axis=1),
                                       jnp.concatenate([st["vb"][:, sb], st["kbe"][:, sb]], axis=1)],
                                      axis=0).astype(BF16)
                ra = _dot(jnp.where(lane_lo, tp, 0.0).astype(BF16), rhs)
                rb = _dot(jnp.where(lane_lo, 0.0, tp).astype(BF16), rhs)
                for hh, res, sl in ((ha, ra, sa), (hb, rb, sb)):
                    u_ref[c, d * nh + hh] = res[:, 0:HEAD_DIM_B]
                    wq_ref[c, d * nh + hh] = jnp.concatenate(
                        [res[:, HEAD_DIM_B:2 * HEAD_DIM_B], st["qhead"][:, sl]], axis=0).astype(BF16)
                kt_t = jnp.concatenate([st["ktail"][:, sa], st["ktail"][:, sb]], axis=0).T
                qkp = st["qk"][:, pr * LANES:(pr + 1) * LANES]
                l2_ref[c, d * 2 + pr] = jnp.concatenate(
                    [jnp.where(lane_lo, qkp, 0.0), jnp.where(lane_lo, 0.0, qkp),
                     jnp.where(lane_lo2, kt_t, 0.0), jnp.where(lane_lo2, 0.0, kt_t)], axis=0).astype(BF16)
            egl_ref[c, d] = jnp.concatenate(
                [st["egl"][0:1, hh * HEAD_DIM_B:(hh + 1) * HEAD_DIM_B] for hh in range(nh)]
                + [jnp.zeros((SUBLANES - nh, HEAD_DIM_B), F32)], axis=0)


def _gdn_expanders():
    nh, cs = N_HEADS_B, CHUNK_B
    e64 = np.zeros((2, LANES, nh * cs), np.float32)
    eb = np.zeros((2, LANES, nh * HEAD_DIM_B), np.float32)
    eg = np.zeros((2, LANES, nh * HEAD_DIM_B), np.float32)
    for d in range(2):
        for hh in range(nh):
            p = d * nh + hh
            e64[d, 2 * nh + p, hh * cs:(hh + 1) * cs] = 1.0
            eb[d, p, hh * HEAD_DIM_B:(hh + 1) * HEAD_DIM_B] = 1.0
            eg[d, 2 * nh + p, hh * HEAD_DIM_B:(hh + 1) * HEAD_DIM_B] = 1.0
    return jnp.asarray(e64, BF16), jnp.asarray(eb, BF16), jnp.asarray(eg, BF16)


def _gdn_chunk(qn, kn, v, bg, expanders):
    n = qn.shape[0]
    cs, nh = CHUNK_B, N_HEADS_B
    nc = n // cs
    g = GDN_CHUNKS_PER_STEP
    assert nc % g == 0
    e64, eb, eg = expanders
    const3 = lambda c: (0, 0, 0)
    return pl.pallas_call(
        _gdn_chunk_kernel,
        grid=(nc // g,),
        in_specs=[
            pl.BlockSpec((g * cs, D_B), lambda c: (c, 0)), pl.BlockSpec((g * cs, D_B), lambda c: (c, 0)),
            pl.BlockSpec((g * cs, D_B), lambda c: (c, 0)), pl.BlockSpec((g * cs, LANES), lambda c: (c, 0)),
            pl.BlockSpec(e64.shape, const3), pl.BlockSpec(eb.shape, const3), pl.BlockSpec(eg.shape, const3),
        ],
        out_specs=[
            pl.BlockSpec((g, 2 * nh, 2 * cs, HEAD_DIM_B), lambda c: (c, 0, 0, 0)),
            pl.BlockSpec((g, 2 * nh, cs, HEAD_DIM_B), lambda c: (c, 0, 0, 0)),
            pl.BlockSpec((g, 4, 6 * cs, LANES), lambda c: (c, 0, 0, 0)),
            pl.BlockSpec((g, 2, SUBLANES, HEAD_DIM_B), lambda c: (c, 0, 0, 0)),
        ],
        out_shape=[
            jax.ShapeDtypeStruct((nc, 2 * nh, 2 * cs, HEAD_DIM_B), BF16),
            jax.ShapeDtypeStruct((nc, 2 * nh, cs, HEAD_DIM_B), F32),
            jax.ShapeDtypeStruct((nc, 4, 6 * cs, LANES), BF16),
            jax.ShapeDtypeStruct((nc, 2, SUBLANES, HEAD_DIM_B), F32),
        ],
        compiler_params=_cparams("parallel"),
        name="gdn_chunk",
    )(qn, kn, v, bg, e64, eb, eg)


def _gdn_rec_kernel(wqf_ref, uf_ref, l2f_ref, egf_ref, wqb_ref, ub_ref, l2b_ref, egb_ref,
                    of_ref, ob_ref, s_scr):
    cs, nh = CHUNK_B, N_HEADS_B

    @pl.when(pl.program_id(0) == 0)
    def _():
        s_scr[...] = jnp.zeros_like(s_scr)

    dirs = ((wqf_ref, uf_ref, l2f_ref, egf_ref, of_ref), (wqb_ref, ub_ref, l2b_ref, egb_ref, ob_ref))
    for step in range(GDN_REC_CHUNKS_PER_STEP):
        for d in range(2):
            wq_ref, u_ref, l2_ref, eg_ref, o_ref = dirs[d]
            c = step if d == 0 else GDN_REC_CHUNKS_PER_STEP - 1 - step
            for pr in range(2):
                v_new, o_state = [], []
                for e in range(2):
                    hh = 2 * pr + e
                    r = _dot(wq_ref[c, hh], s_scr[d * nh + hh].astype(BF16))
                    v_new.append(u_ref[c, hh] - r[0:cs])
                    o_state.append(r[cs:2 * cs])
                r2 = _dot(l2_ref[c, pr], jnp.concatenate(v_new, axis=0).astype(BF16))
                for e in range(2):
                    hh = 2 * pr + e
                    o_ref[c * cs:(c + 1) * cs, hh * HEAD_DIM_B:(hh + 1) * HEAD_DIM_B] = (
                        o_state[e] + r2[e * cs:(e + 1) * cs])
                    s_scr[d * nh + hh] = (s_scr[d * nh + hh] * eg_ref[c, 0, hh:hh + 1, :]
                                          + r2[2 * cs + e * 2 * cs:2 * cs + (e + 1) * 2 * cs])


def _gdn_rec(wq, u, l2, egl, lc):
    cs, nh = CHUNK_B, N_HEADS_B
    g = GDN_REC_CHUNKS_PER_STEP
    assert wq.shape[0] % g == 0 and (lc // cs) % g == 0
    nc = wq.shape[0] // g
    ncc = lc // cs // g
    fwd = lambda i: i
    bwd = lambda i: jnp.where(i < ncc, ncc - 1 - i, nc - 1 - (i - ncc))
    specs = []
    for d, order in ((0, fwd), (1, bwd)):
        specs += [
            pl.BlockSpec((g, nh, 2 * cs, HEAD_DIM_B), lambda i, d=d, order=order: (order(i), d, 0, 0)),
            pl.BlockSpec((g, nh, cs, HEAD_DIM_B), lambda i, d=d, order=order: (order(i), d, 0, 0)),
            pl.BlockSpec((g, 2, 6 * cs, LANES), lambda i, d=d, order=order: (order(i), d, 0, 0)),
            pl.BlockSpec((g, 1, SUBLANES, HEAD_DIM_B), lambda i, d=d, order=order: (order(i), d, 0, 0)),
        ]
    return pl.pallas_call(
        _gdn_rec_kernel,
        grid=(nc,),
        in_specs=specs,
        out_specs=[pl.BlockSpec((g * cs, D_B), lambda i: (fwd(i), 0)),
                   pl.BlockSpec((g * cs, D_B), lambda i: (bwd(i), 0))],
        out_shape=[jax.ShapeDtypeStruct((nc * g * cs, D_B), F32)] * 2,
        scratch_shapes=[pltpu.VMEM((2 * nh, HEAD_DIM_B, HEAD_DIM_B), F32)],
        compiler_params=_cparams("arbitrary"),
        name="gdn_rec",
    )(wq, u, l2, egl, wq, u, l2, egl)


def _outproj_kernel(yaf_ref, yab_ref, obf_ref, obb_ref, octx_ref, olat_ref, ga_ref, gb_ref, gc_ref, h_ref,
                    mod_ref, on_ref, w_ref, gn_ref, modn_ref, *out_refs, lc, tm, row_off, last):
    row0 = (pl.program_id(0) + row_off) * tm
    ya = jnp.concatenate([yaf_ref[cg] + yab_ref[cg] for cg in range(D_A // LANES)], axis=1)
    mix_a = (ya * _silu(ga_ref[...])).astype(BF16)
    proj = _dot(mix_a, w_ref[0, 0:D_A, :])
    ob = obf_ref[...] + obb_ref[...]
    gb = _silu(gb_ref[...])
    mix_b = []
    for hd in range(N_HEADS_B):
        sl = slice(hd * HEAD_DIM_B, (hd + 1) * HEAD_DIM_B)
        x = ob[:, sl]
        mix_b.append(x * lax.rsqrt(jnp.mean(x * x, axis=-1, keepdims=True) + EPS) * on_ref[...] * gb[:, sl])
    proj = proj + _dot(jnp.concatenate(mix_b, axis=1).astype(BF16), w_ref[0, D_A:D_A + D_B, :])
    oc = olat_ref[...] if last else jnp.where(row0 < lc, octx_ref[...], olat_ref[...])
    mix_c = (oc * _silu(gc_ref[...])).astype(BF16)
    proj = proj + _dot(mix_c, w_ref[0, D_A + D_B:D_MIX, :])
    row = row0 + lax.broadcasted_iota(jnp.int32, (tm, 1), 0)
    gate = jnp.where(row < lc, mod_ref[1:2, 2 * D_MODEL:3 * D_MODEL], mod_ref[0:1, 2 * D_MODEL:3 * D_MODEL])
    h_new = h_ref[...] + gate * proj
    if last:
        out_refs[0][...] = (h_new * lax.rsqrt(jnp.mean(h_new * h_new, axis=-1, keepdims=True) + EPS)
                            * gn_ref[...])
    else:
        out_refs[0][...] = h_new
        out_refs[1][...] = _adaln(h_new, row0, gn_ref, modn_ref, lc)


def _outproj(ya_f, ya_b, ob_f, ob_b, oc_ctx, oc_lat, z, h, mod_l, onorm_l, w_out_r, layer, gain_next,
             mod_next, lc, last):
    n = h.shape[0]
    tm = ATTN_TILE
    assert lc % tm == 0
    n_ctx = lc // tm
    row_off = n_ctx if last else 0
    rows = lambda width, col: pl.BlockSpec((tm, width), lambda i: (i + row_off, col // width))
    const2 = lambda i: (0, 0)
    out_block = pl.BlockSpec((tm, D_MODEL), lambda i: (i, 0))
    if last:
        out_specs = [out_block]
        out_shape = [jax.ShapeDtypeStruct((n - lc, D_MODEL), F32)]
    else:
        out_specs = [out_block, out_block]
        out_shape = [jax.ShapeDtypeStruct((n, D_MODEL), F32), jax.ShapeDtypeStruct((n, D_MODEL), BF16)]
    return pl.pallas_call(
        functools.partial(_outproj_kernel, lc=lc, tm=tm, row_off=row_off, last=last),
        grid=((n - row_off * tm) // tm,),
        in_specs=[
            pl.BlockSpec((D_A // LANES, tm, LANES), lambda i: (0, i + row_off, 0)),
            pl.BlockSpec((D_A // LANES, tm, LANES), lambda i: (0, i + row_off, 0)),
            rows(D_B, 0), rows(D_B, 0),
            pl.BlockSpec((tm, D_Q_C), lambda i: (jnp.minimum(i + row_off, n_ctx - 1), 0)),
            pl.BlockSpec((tm, D_Q_C), lambda i: (jnp.maximum(i + row_off - n_ctx, 0), 0)),
            rows(D_A, COL_GA), rows(D_B, COL_GB), rows(D_Q_C, COL_GC), rows(D_MODEL, 0),
            pl.BlockSpec((SUBLANES, 3 * D_MODEL), const2), pl.BlockSpec((1, HEAD_DIM_B), const2),
            pl.BlockSpec((1, D_MIX, D_MODEL), lambda i: (layer, 0, 0)),
            pl.BlockSpec((1, D_MODEL), const2), pl.BlockSpec((SUBLANES, 3 * D_MODEL), const2),
        ],
        out_specs=out_specs,
        out_shape=out_shape,
        compiler_params=_cparams("parallel"),
        name="outproj_last" if last else "outproj",
    )(ya_f, ya_b, ob_f, ob_b, oc_ctx, oc_lat, z, z, z, h, mod_l, onorm_l, w_out_r, gain_next, mod_next)


def _rope_tables(t_len, lc):
    rows = t_len // GRID_W
    row = jnp.repeat(jnp.arange(rows, dtype=jnp.int32), GRID_W)
    col = jnp.tile(jnp.arange(GRID_W, dtype=jnp.int32), rows)
    n_freq = HEAD_DIM_C // 4
    inv_freq = ROPE_BASE ** (-jnp.arange(n_freq, dtype=F32) / n_freq)
    ang = jnp.concatenate([row.astype(F32)[:, None] * inv_freq, col.astype(F32)[:, None] * inv_freq], axis=-1)
    cos, sin = jnp.cos(ang), jnp.sin(ang)
    half = HEAD_DIM_C // 2
    cos2 = jnp.concatenate([jnp.ones((lc, 2 * half), F32), jnp.concatenate([cos, cos], axis=-1)], axis=0)
    sin2 = jnp.concatenate([jnp.zeros((lc, 2 * half), F32), jnp.concatenate([-sin, sin], axis=-1)], axis=0)
    return cos2, sin2


def _relayout_w_in_kernel(w_ref, o_ref):
    n_ba = 4 * N_HEADS_B
    src_ba = 2 * D_A + 4 * D_B
    src_qc = src_ba + n_ba
    copies = ((0, COL_XA, D_A), (D_A, COL_GA, D_A), (2 * D_A, COL_QKVB, 3 * D_B),
              (2 * D_A + 3 * D_B, COL_GB, D_B), (src_ba, COL_BA, n_ba),
              (src_qc, COL_QC, D_Q_C), (src_qc + D_Q_C, COL_KV, 2 * D_KV_C),
              (src_qc + D_Q_C + 2 * D_KV_C, COL_GC, D_Q_C))
    for src, dst, rows in copies:
        o_ref[0, dst:dst + rows, :] = w_ref[0, src:src + rows, :].astype(BF16)
    o_ref[0, COL_BA + n_ba:D_Z, :] = jnp.zeros((D_Z - COL_BA - n_ba, o_ref.shape[2]), BF16)


def _relayout_w_in(w_in):
    depth, d, d_in = w_in.shape
    tc = 512
    return pl.pallas_call(
        _relayout_w_in_kernel,
        grid=(depth, d // tc),
        in_specs=[pl.BlockSpec((1, d_in, tc), lambda l, i: (l, 0, i))],
        out_specs=pl.BlockSpec((1, D_Z, tc), lambda l, i: (l, 0, i)),
        out_shape=jax.ShapeDtypeStruct((depth, D_Z, d), BF16),
        compiler_params=_cparams("parallel", "parallel"),
        name="relayout_w_in",
    )(jnp.swapaxes(w_in, 1, 2))


def _block_diag(w):
    eye = jnp.eye(N_BLK_A, dtype=w.dtype)
    out = jnp.einsum('...nij,nm->...nimj', w, eye)
    return out.reshape(w.shape[:-3] + (D_A, D_A))


def kernel(x, c, ctx, c_ctx, norm_g, w_mod, b_mod, w_in, conv_a_w, conv_a_b, w_ra, b_ra, w_ia, b_ia, lam_a,
           conv_b_w, a_log_b, dt_bias_b, onorm_b, qn_c, kn_c, w_out, final_g):
    bsz, t_len, d = x.shape
    lc = ctx.shape[1]
    depth = w_in.shape[0]
    assert bsz == 1 and d == D_MODEL and t_len % GRID_W == 0

    cc = jnp.concatenate([c, c_ctx[None, :], jnp.zeros((SUBLANES - 2, d), F32)], axis=0)
    mod = _modulation(cc, w_mod, b_mod)

    w_in_r = _relayout_w_in(w_in)
    w_out_r = w_out.astype(BF16)
    wg = jnp.concatenate([_block_diag(w_ra), _block_diag(w_ia)], axis=-1).astype(BF16)
    bgate = jnp.concatenate([b_ra, b_ia], axis=-1)[:, :, None, :]
    nh2 = 2 * N_HEADS_B
    lane_pad = lambda v: jnp.pad(v.reshape(depth, 1, nh2), ((0, 0), (0, 0), (nh2, LANES - 2 * nh2)))
    avec = lane_pad(a_log_b)
    dtvec = lane_pad(dt_bias_b)
    cos2, sin2 = _rope_tables(t_len, lc)
    expanders = _gdn_expanders()

    h, y = _prenorm(ctx, x, mod[0], norm_g[0][None, :])
    for l in range(depth):
        last = l == depth - 1
        z = _inproj(y, w_in_r, l)
        qh, kh, vt = _attn_prep(z, cos2, sin2, qn_c[l][None, :], kn_c[l][None, :])
        oc_lat = _attention(qh, kh, vt, lc, latent=True)
        oc_ctx = oc_lat if last else _attention(qh, kh, vt, lc, latent=False)
        ya_f, ya_b = _rglru(z, conv_a_w[l], conv_a_b[l][None, :], wg[l], bgate[l], lam_a[l], lc)
        qn, kn, v, bg = _gdn_prep(z, conv_b_w[l], avec[l], dtvec[l], lc)
        wq, u, l2, egl = _gdn_chunk(qn, kn, v, bg, expanders)
        ob_f, ob_b = _gdn_rec(wq, u, l2, egl, lc)
        gain_next = final_g[None, :] if last else norm_g[l + 1][None, :]
        mod_next = mod[l] if last else mod[l + 1]
        outs = _outproj(ya_f, ya_b, ob_f, ob_b, oc_ctx, oc_lat, z, h, mod[l], onorm_b[l][None, :], w_out_r, l,
                        gain_next, mod_next, lc, last)
        if last:
            return outs[0][None]
        h, y = outs
```

```python
import functools
import math

import numpy as np
import jax
import jax.numpy as jnp
from jax import lax
from jax.experimental import pallas as pl
from jax.experimental.pallas import tpu as pltpu

F32 = jnp.float32
BF16 = jnp.bfloat16

D_MODEL = 2048
GRID_W = 64
EPS = 1e-6
ROPE_BASE = 10000.0

D_A = 512
N_BLK_A = 8
BLK_A = D_A // N_BLK_A
LRU_C = 8.0

N_HEADS_B = 4
HEAD_DIM_B = 128
D_B = N_HEADS_B * HEAD_DIM_B
CHUNK_B = 64

N_Q_HEADS_C = 8
N_KV_HEADS_C = 2
GROUP_C = N_Q_HEADS_C // N_KV_HEADS_C
HEAD_DIM_C = 128
D_Q_C = N_Q_HEADS_C * HEAD_DIM_C
D_KV_C = N_KV_HEADS_C * HEAD_DIM_C

D_MIX = D_A + D_B + D_Q_C

COL_QC = 0
COL_GC = 1024
COL_KV = 2048
COL_XA = 2560
COL_QKVB = 3072
COL_GA = 4608
COL_GB = 5120
COL_BA = 5632
D_Z = 5760
LANES = 128
SUBLANES = 8
LOG2_E = math.log2(math.e)
ATTN_TILE = 256
V_ROWS = HEAD_DIM_C + 16
NEG_BIG = -1e30
GDN_REC_CHUNKS_PER_STEP = 4
GDN_CHUNKS_PER_STEP = 4

VMEM_LIMIT = 56 * 1024 * 1024


def _cparams(*sem):
    return pltpu.CompilerParams(dimension_semantics=tuple(sem), vmem_limit_bytes=VMEM_LIMIT)


def _dot(a, b):
    return jnp.dot(a, b, preferred_element_type=F32)


def _dot_nt(a, b):
    return lax.dot_general(a, b, (((1,), (1,)), ((), ())), preferred_element_type=F32)


def _split2(a):
    hi = a.astype(BF16)
    lo = (a - hi.astype(F32)).astype(BF16)
    return hi, lo


def _split3(a):
    hi = a.astype(BF16)
    r = a - hi.astype(F32)
    mid = r.astype(BF16)
    lo = (r - mid.astype(F32)).astype(BF16)
    return hi, mid, lo


def _dot3(a, b):
    ah, al = _split2(a)
    bh, bl = _split2(b)
    return _dot(ah, bh) + (_dot(al, bh) + _dot(ah, bl))


def _silu(x):
    return x * jax.nn.sigmoid(x)


def _softplus(x):
    return jnp.maximum(x, 0.0) + jnp.log1p(jnp.exp(-jnp.abs(x)))


def _pick(n, cands):
    for c in cands:
        if n % c == 0:
            return c
    raise ValueError(f"no tile for {n} in {cands}")


def _mod_kernel(c_ref, w_ref, b_ref, o_ref):
    o_ref[0] = _dot3(_silu(c_ref[...]), w_ref[0]) + b_ref[0]


def _modulation(cc, w_mod, b_mod):
    depth, d, d3 = w_mod.shape
    tn = 768
    return pl.pallas_call(
        _mod_kernel,
        grid=(depth, d3 // tn),
        in_specs=[
            pl.BlockSpec((SUBLANES, d), lambda l, j: (0, 0)),
            pl.BlockSpec((1, d, tn), lambda l, j: (l, 0, j)),
            pl.BlockSpec((1, 1, tn), lambda l, j: (l, 0, j)),
        ],
        out_specs=pl.BlockSpec((1, SUBLANES, tn), lambda l, j: (l, 0, j)),
        out_shape=jax.ShapeDtypeStruct((depth, SUBLANES, d3), F32),
        compiler_params=_cparams("parallel", "parallel"),
        name="modulation",
    )(cc, w_mod, b_mod.reshape(depth, 1, d3))


def _adaln(h_new, row0, g_ref, mod_ref, lc):
    xn = h_new * lax.rsqrt(jnp.mean(h_new * h_new, axis=-1, keepdims=True) + EPS) * g_ref[...]
    row = row0 + lax.broadcasted_iota(jnp.int32, (h_new.shape[0], 1), 0)
    is_ctx = row < lc
    shift = jnp.where(is_ctx, mod_ref[1:2, 0:D_MODEL], mod_ref[0:1, 0:D_MODEL])
    scale = jnp.where(is_ctx, mod_ref[1:2, D_MODEL:2 * D_MODEL], mod_ref[0:1, D_MODEL:2 * D_MODEL])
    return (xn * (1.0 + scale) + shift).astype(BF16)


def _prenorm_kernel(ctx_ref, x_ref, mod_ref, g_ref, h_ref, y_ref, *, lc, tm):
    row0 = pl.program_id(0) * tm
    h = jnp.where(row0 < lc, ctx_ref[0], x_ref[0])
    h_ref[...] = h
    y_ref[...] = _adaln(h, row0, g_ref, mod_ref, lc)


def _prenorm(ctx, x, mod_l, norm_g_l):
    lc, t_len = ctx.shape[1], x.shape[1]
    tm = 256
    assert lc % tm == 0 and t_len % tm == 0
    n_ctx = lc // tm
    n = lc + t_len
    out_block = pl.BlockSpec((tm, D_MODEL), lambda i: (i, 0))
    return pl.pallas_call(
        functools.partial(_prenorm_kernel, lc=lc, tm=tm),
        grid=(n // tm,),
        in_specs=[
            pl.BlockSpec((1, tm, D_MODEL), lambda i: (0, jnp.minimum(i, n_ctx - 1), 0)),
            pl.BlockSpec((1, tm, D_MODEL), lambda i: (0, jnp.maximum(i - n_ctx, 0), 0)),
            pl.BlockSpec((SUBLANES, 3 * D_MODEL), lambda i: (0, 0)),
            pl.BlockSpec((1, D_MODEL), lambda i: (0, 0)),
        ],
        out_specs=[out_block, out_block],
        out_shape=[jax.ShapeDtypeStruct((n, D_MODEL), F32), jax.ShapeDtypeStruct((n, D_MODEL), BF16)],
        compiler_params=_cparams("parallel"),
        name="prenorm",
    )(ctx, x, mod_l, norm_g_l)


def _inproj_kernel(y_ref, wt_ref, z_ref):
    z_ref[...] = _dot_nt(y_ref[...], wt_ref[0])


def _inproj(y, w_in_t, layer):
    n = y.shape[0]
    tm = _pick(n, (1408, 768, 640, 512, 256))
    tn = 1920
    return pl.pallas_call(
        _inproj_kernel,
        grid=(n // tm, D_Z // tn),
        in_specs=[
            pl.BlockSpec((tm, D_MODEL), lambda i, j: (i, 0)),
            pl.BlockSpec((1, tn, D_MODEL), lambda i, j: (layer, j, 0)),
        ],
        out_specs=pl.BlockSpec((tm, tn), lambda i, j: (i, j)),
        out_shape=jax.ShapeDtypeStruct((n, D_Z), F32),
        compiler_params=_cparams("parallel", "parallel"),
        name="inproj",
    )(y, w_in_t)


def _attn_prep_kernel(q_ref, kv_ref, cos_ref, sin_ref, qn_ref, kn_ref, qo_ref, ko_ref, vo_ref):
    cos = cos_ref[...]
    sin = sin_ref[...]

    ones = jnp.ones((HEAD_DIM_C, HEAD_DIM_C), BF16)

    def prep(x, gain, scale):
        sq_hi, sq_lo = _split2(x * x)
        mean_sq = (_dot(sq_hi, ones) + _dot(sq_lo, ones)) * (1.0 / HEAD_DIM_C)
        y = x * lax.rsqrt(mean_sq + EPS) * gain
        rot = pltpu.roll(y, HEAD_DIM_C // 2, 1)
        return (y * cos + rot * sin) * scale

    for hd in range(N_Q_HEADS_C):
        sl = slice(hd * HEAD_DIM_C, (hd + 1) * HEAD_DIM_C)
        qo_ref[:, sl] = prep(q_ref[:, sl], qn_ref[...], HEAD_DIM_C ** -0.5 * LOG2_E).astype(BF16)
    for hd in range(N_KV_HEADS_C):
        sl = slice(hd * HEAD_DIM_C, (hd + 1) * HEAD_DIM_C)
        ko_ref[:, sl] = prep(kv_ref[:, sl], kn_ref[...], 1.0).astype(BF16)
    vt = kv_ref[:, D_KV_C:2 * D_KV_C].T
    ones = jnp.ones((V_ROWS - HEAD_DIM_C, vt.shape[1]), F32)
    for hd in range(N_KV_HEADS_C):
        vo_ref[0, hd] = jnp.concatenate([vt[hd * HEAD_DIM_C:(hd + 1) * HEAD_DIM_C], ones], axis=0).astype(BF16)


def _attn_prep(z, cos2, sin2, qn_l, kn_l):
    n = z.shape[0]
    tm = ATTN_TILE
    return pl.pallas_call(
        _attn_prep_kernel,
        grid=(n // tm,),
        in_specs=[
            pl.BlockSpec((tm, D_Q_C), lambda i: (i, COL_QC // D_Q_C)),
            pl.BlockSpec((tm, 2 * D_KV_C), lambda i: (i, COL_KV // (2 * D_KV_C))),
            pl.BlockSpec((tm, HEAD_DIM_C), lambda i: (i, 0)),
            pl.BlockSpec((tm, HEAD_DIM_C), lambda i: (i, 0)),
            pl.BlockSpec((1, HEAD_DIM_C), lambda i: (0, 0)),
            pl.BlockSpec((1, HEAD_DIM_C), lambda i: (0, 0)),
        ],
        out_specs=[
            pl.BlockSpec((tm, D_Q_C), lambda i: (i, 0)),
            pl.BlockSpec((tm, D_KV_C), lambda i: (i, 0)),
            pl.BlockSpec((1, N_KV_HEADS_C, V_ROWS, tm), lambda i: (i, 0, 0, 0)),
        ],
        out_shape=[
            jax.ShapeDtypeStruct((n, D_Q_C), BF16),
            jax.ShapeDtypeStruct((n, D_KV_C), BF16),
            jax.ShapeDtypeStruct((n // tm, N_KV_HEADS_C, V_ROWS, tm), BF16),
        ],
        compiler_params=_cparams("parallel"),
        name="attn_prep",
    )(z, z, cos2, sin2, qn_l, kn_l)


def _attn_kernel(q_ref, k_ref, vt_ref, o_ref, s0_scr, sa_scr, sb_scr, acc_scr, *, lc, tq, tk, n_super):
    nb = lc // tk

    def scores(row0, nrows, s_scr):
        kb = k_ref[pl.ds(row0, nrows), :]
        for g in range(GROUP_C):
            s_scr[g] = _dot_nt(kb, q_ref[:, g * HEAD_DIM_C:(g + 1) * HEAD_DIM_C])

    def softmax_pv(s_scr, blk0, n_blk, ms):
        vts = [vt_ref[blk0 + b, 0] for b in range(n_blk)]
        new_m = []
        for g in range(GROUP_C):
            st = s_scr[g]
            m_blk = jnp.max(st, axis=0, keepdims=True)
            m_new = m_blk if ms is None else jnp.maximum(ms[g], m_blk)
            p = jnp.exp2(st - m_new).astype(BF16)
            pv = _dot(vts[0], p[0:tk])
            for b in range(1, n_blk):
                pv = pv + _dot(vts[b], p[b * tk:(b + 1) * tk])
            acc_scr[g] = pv if ms is None else jnp.exp2(ms[g] - m_new) * acc_scr[g] + pv
            new_m.append(m_new)
        return tuple(new_m)

    def super_rows(j):
        return pl.multiple_of(lc + j * (2 * tk), tk)

    if n_super:
        def pair(j, ms):
            scores(super_rows(j + 1), 2 * tk, sb_scr)
            ms = softmax_pv(sa_scr, nb + 2 * j, 2, ms)
            scores(super_rows(j + 2), 2 * tk, sa_scr)
            return softmax_pv(sb_scr, nb + 2 * j + 2, 2, ms)

        scores(super_rows(0), 2 * tk, sa_scr)
        ms = None
        if n_super >= 4:
            ms = pair(0, None)
            middle = n_super // 2 - 2
            per_trip = 3 if middle % 3 == 0 else 2

            def trip(i, ms):
                for t in range(per_trip):
                    ms = pair(2 * (per_trip * i + t) + 2, ms)
                return ms

            ms = lax.fori_loop(0, middle // per_trip, trip, ms)
            for j in range(2 * (middle // per_trip * per_trip) + 2, n_super - 2, 2):
                ms = pair(j, ms)
        scores(super_rows(n_super - 1), 2 * tk, sb_scr)
        ms = softmax_pv(sa_scr, nb + 2 * (n_super - 2), 2, ms)
        scores(0, lc, s0_scr)
        ms = softmax_pv(sb_scr, nb + 2 * (n_super - 1), 2, ms)
        softmax_pv(s0_scr, 0, nb, ms)
    else:
        scores(0, lc, s0_scr)
        softmax_pv(s0_scr, 0, nb, None)

    for g in range(GROUP_C):
        acc = acc_scr[g]
        o_ref[:, g * HEAD_DIM_C:(g + 1) * HEAD_DIM_C] = (
            acc[0:HEAD_DIM_C] / acc[HEAD_DIM_C:HEAD_DIM_C + 1]).T


def _attention(qh, kh, vt, lc, latent):
    n = qh.shape[0]
    tq = tk = ATTN_TILE
    assert lc % tq == 0 and n % tq == 0
    if latent:
        n_keys, n_q, q_off = n, n - lc, lc // tq
        n_super = (n - lc) // (2 * tk)
        assert n_super * 2 * tk == n - lc and n_super % 2 == 0
        super_shape = (GROUP_C, 2 * tk, tq)
    else:
        n_keys, n_q, q_off, n_super = lc, lc, 0, 0
        super_shape = (1, SUBLANES, LANES)
    gw = GROUP_C * HEAD_DIM_C
    return pl.pallas_call(
        functools.partial(_attn_kernel, lc=lc, tq=tq, tk=tk, n_super=n_super),
        grid=(N_KV_HEADS_C, n_q // tq),
        in_specs=[
            pl.BlockSpec((tq, gw), lambda j, i: (i + q_off, j)),
            pl.BlockSpec((n_keys, HEAD_DIM_C), lambda j, i: (0, j)),
            pl.BlockSpec((n_keys // tk, 1, V_ROWS, tk), lambda j, i: (0, j, 0, 0)),
        ],
        out_specs=pl.BlockSpec((tq, gw), lambda j, i: (i, j)),
        out_shape=jax.ShapeDtypeStruct((n_q, D_Q_C), F32),
        scratch_shapes=[pltpu.VMEM((GROUP_C, lc, tq), F32),
                        pltpu.VMEM(super_shape, F32), pltpu.VMEM(super_shape, F32),
                        pltpu.VMEM((GROUP_C, V_ROWS, tq), F32)],
        compiler_params=_cparams("parallel", "parallel"),
        name="attention" if latent else "attention_ctx",
    )(qh, kh, vt)


def _conv4(x, prev8, next8, w_ref, prev_ok, next_ok):
    tm = x.shape[0]
    row = lax.broadcasted_iota(jnp.int32, (tm, 1), 0)
    p = jnp.where(prev_ok, prev8, 0.0)
    nx = jnp.where(next_ok, next8, 0.0)
    xm1 = jnp.where(row == 0, p[7:8], pltpu.roll(x, 1, 0))
    xm2 = jnp.where(row == 0, p[6:7], jnp.where(row == 1, p[7:8], pltpu.roll(x, 2, 0)))
    xp1 = jnp.where(row == tm - 1, nx[0:1], pltpu.roll(x, tm - 1, 0))
    return w_ref[0:1] * xm2 + w_ref[1:2] * xm1 + w_ref[2:3] * x + w_ref[3:4] * xp1


def _seg_flags(t, nbc, nt):
    prev_ok = jnp.logical_and(t != 0, t != nbc)
    next_ok = jnp.logical_and(t != nbc - 1, t != nt - 1)
    return prev_ok, next_ok


def _halo_maps(tile_of, tm, n, col_block):
    r = tm // SUBLANES
    last = n // SUBLANES - 1
    main = lambda i: (tile_of(i), col_block)
    prev = lambda i: (jnp.maximum(tile_of(i) * r - 1, 0), col_block)
    nxt = lambda i: (jnp.minimum((tile_of(i) + 1) * r, last), col_block)
    return main, prev, nxt


def _rglru_kernel(xf_ref, pf_ref, nf_ref, xb_ref, pb_ref, nb_ref, cw_ref, cb_ref, wg_ref, bg_ref,
                  lam_ref, of_ref, ob_ref, x_scr, hcar, *, tm, nbc, nt):
    i = pl.program_id(0)
    seg = tm // SUBLANES
    ncg = D_A // LANES
    sub = lax.broadcasted_iota(jnp.int32, (SUBLANES, D_A), 0)

    @pl.when(i == 0)
    def _():
        hcar[...] = jnp.zeros_like(hcar)

    tiles = (i, jnp.where(i < nbc, nbc - 1 - i, nt - 1 - (i - nbc)))
    refs = ((xf_ref, pf_ref, nf_ref, of_ref), (xb_ref, pb_ref, nb_ref, ob_ref))
    for d in range(2):
        x_ref, p_ref, n_ref, o_ref = refs[d]
        prev_ok, next_ok = _seg_flags(tiles[d], nbc, nt)
        p = jnp.where(prev_ok, p_ref[...], 0.0)
        nx = jnp.where(next_ok, n_ref[...], 0.0)
        for cg in range(ncg):
            x_scr[cg] = x_ref[:, cg * LANES:(cg + 1) * LANES]
        xg = [jnp.concatenate([x_scr[cg, pl.ds(j, SUBLANES, stride=seg), :] for cg in range(ncg)], axis=1)
              for j in range(seg)]
        before1 = jnp.where(sub == 0, p[7:8], pltpu.roll(xg[seg - 1], 1, 0))
        before2 = jnp.where(sub == 0, p[6:7], pltpu.roll(xg[seg - 2], 1, 0))
        after1 = jnp.where(sub == SUBLANES - 1, nx[0:1], pltpu.roll(xg[0], SUBLANES - 1, 0))
        xp = jnp.concatenate(xg, axis=0)
        xm1 = jnp.concatenate([before1] + xg[:seg - 1], axis=0)
        xm2 = jnp.concatenate([before2, before1] + xg[:seg - 2], axis=0)
        xp1 = jnp.concatenate(xg[1:] + [after1], axis=0)
        u = (cw_ref[0:1] * xm2 + cw_ref[1:2] * xm1 + cw_ref[2:3] * xp + cw_ref[3:4] * xp1) + cb_ref[...]
        gates = _dot(u.astype(BF16), wg_ref[d]) + bg_ref[d]
        gate_r = jax.nn.sigmoid(gates[:, 0:D_A])
        gate_i = jax.nn.sigmoid(gates[:, D_A:2 * D_A])
        log_a = (-LRU_C) * gate_r * _softplus(-lam_ref[d:d + 1, :])
        a = jnp.exp(log_a)
        b = jnp.sqrt(-jnp.tanh(log_a) * (a * a + 1.0)) * (gate_i * u)

        order = list(range(seg)) if d == 0 else list(range(seg - 1, -1, -1))
        hloc = jnp.zeros((SUBLANES, D_A), F32)
        pcum = jnp.ones((SUBLANES, D_A), F32)
        hs, ps = [None] * seg, [None] * seg
        for j in order:
            aj = a[j * SUBLANES:(j + 1) * SUBLANES]
            hloc = aj * hloc + b[j * SUBLANES:(j + 1) * SUBLANES]
            pcum = aj * pcum
            hs[j], ps[j] = hloc, pcum
        carry = hcar[d:d + 1, :]
        seg_in = [None] * SUBLANES
        for s in (range(SUBLANES) if d == 0 else range(SUBLANES - 1, -1, -1)):
            seg_in[s] = carry
            carry = pcum[s:s + 1] * carry + hloc[s:s + 1]
        hcar[d:d + 1, :] = carry
        cin = jnp.concatenate(seg_in, axis=0)
        for j in range(seg):
            hj = hs[j] + ps[j] * cin
            for cg in range(ncg):
                o_ref[cg, pl.ds(j, SUBLANES, stride=seg), :] = hj[:, cg * LANES:(cg + 1) * LANES]


def _rglru(z, cw, cb, wg, bg, lam, lc):
    n = z.shape[0]
    tm = 256
    assert lc % tm == 0 and n % tm == 0
    nbc, nt = lc // tm, n // tm
    fwd = lambda i: i
    bwd = lambda i: jnp.where(i < nbc, nbc - 1 - i, nt - 1 - (i - nbc))
    cblk = COL_XA // D_A
    fm, fp, fn = _halo_maps(fwd, tm, n, cblk)
    bm, bp, bn = _halo_maps(bwd, tm, n, cblk)
    const2 = lambda i: (0, 0)
    const3 = lambda i: (0, 0, 0)
    return pl.pallas_call(
        functools.partial(_rglru_kernel, tm=tm, nbc=nbc, nt=nt),
        grid=(nt,),
        in_specs=[
            pl.BlockSpec((tm, D_A), fm), pl.BlockSpec((SUBLANES, D_A), fp), pl.BlockSpec((SUBLANES, D_A), fn),
            pl.BlockSpec((tm, D_A), bm), pl.BlockSpec((SUBLANES, D_A), bp), pl.BlockSpec((SUBLANES, D_A), bn),
            pl.BlockSpec((4, D_A), const2), pl.BlockSpec((1, D_A), const2),
            pl.BlockSpec((2, D_A, 2 * D_A), const3), pl.BlockSpec((2, 1, 2 * D_A), const3),
            pl.BlockSpec((2, D_A), const2),
        ],
        out_specs=[pl.BlockSpec((D_A // LANES, tm, LANES), lambda i: (0, fwd(i), 0)),
                   pl.BlockSpec((D_A // LANES, tm, LANES), lambda i: (0, bwd(i), 0))],
        out_shape=[jax.ShapeDtypeStruct((D_A // LANES, n, LANES), F32)] * 2,
        scratch_shapes=[pltpu.VMEM((D_A // LANES, tm, LANES), F32), pltpu.VMEM((SUBLANES, D_A), F32)],
        compiler_params=_cparams("arbitrary"),
        name="rglru",
    )(z, z, z, z, z, z, cw, cb, wg, bg, lam)


def _gdn_prep_kernel(x_ref, p_ref, n_ref, ba_ref, cw_ref, av_ref, dt_ref, q_ref, k_ref, v_ref, bg_ref,
                     *, nbc, nt):
    prev_ok, next_ok = _seg_flags(pl.program_id(0), nbc, nt)
    u = _silu(_conv4(x_ref[...], p_ref[...], n_ref[...], cw_ref, prev_ok, next_ok))
    for hd in range(N_HEADS_B):
        sl = slice(hd * HEAD_DIM_B, (hd + 1) * HEAD_DIM_B)
        qv = u[:, hd * HEAD_DIM_B:(hd + 1) * HEAD_DIM_B]
        kv = u[:, D_B + hd * HEAD_DIM_B:D_B + (hd + 1) * HEAD_DIM_B]
        q_ref[:, sl] = qv * lax.rsqrt(jnp.sum(qv * qv, axis=-1, keepdims=True) + EPS)
        k_ref[:, sl] = kv * lax.rsqrt(jnp.sum(kv * kv, axis=-1, keepdims=True) + EPS)
    v_ref[...] = u[:, 2 * D_B:3 * D_B]
    ba = ba_ref[...]
    lane = lax.broadcasted_iota(jnp.int32, ba.shape, 1)
    beta = jax.nn.sigmoid(ba)
    g = -jnp.exp(av_ref[...]) * _softplus(ba + dt_ref[...])
    bg_ref[...] = jnp.where(lane < 2 * N_HEADS_B, beta, jnp.where(lane < 4 * N_HEADS_B, g, 0.0))


def _gdn_prep(z, cw, avec, dtvec, lc):
    n = z.shape[0]
    tm = 256
    nbc, nt = lc // tm, n // tm
    w3 = 3 * D_B
    assert COL_QKVB % w3 == 0
    m, p, nx = _halo_maps(lambda i: i, tm, n, COL_QKVB // w3)
    const2 = lambda i: (0, 0)
    return pl.pallas_call(
        functools.partial(_gdn_prep_kernel, nbc=nbc, nt=nt),
        grid=(nt,),
        in_specs=[
            pl.BlockSpec((tm, w3), m), pl.BlockSpec((SUBLANES, w3), p), pl.BlockSpec((SUBLANES, w3), nx),
            pl.BlockSpec((tm, LANES), lambda i: (i, COL_BA // LANES)),
            pl.BlockSpec((4, w3), const2), pl.BlockSpec((1, LANES), const2), pl.BlockSpec((1, LANES), const2),
        ],
        out_specs=[pl.BlockSpec((tm, D_B), lambda i: (i, 0))] * 3 + [pl.BlockSpec((tm, LANES), lambda i: (i, 0))],
        out_shape=[jax.ShapeDtypeStruct((n, D_B), F32)] * 3 + [jax.ShapeDtypeStruct((n, LANES), F32)],
        compiler_params=_cparams("parallel"),
        name="gdn_prep",
    )(z, z, z, z, cw, avec, dtvec)


def _unit_tri_inverse(mats, row, col, lane_lo):
    def block_diag(y):
        return jnp.concatenate([jnp.where(lane_lo, y, 0.0), jnp.where(lane_lo, 0.0, y)], axis=0).astype(BF16)

    def mm(xs, ys):
        return [_dot(x.astype(BF16), block_diag(y)) for x, y in zip(xs, ys)]

    def same(shift):
        return (row >> shift) == (col >> shift)

    eye = jnp.where(row == col, 1.0, 0.0)
    rows = row.shape[0]
    m8 = [jnp.where(same(3), m, 0.0) for m in mats]
    xs = [eye - m for m in m8]
    pw = mm(m8, m8)
    both = mm([jnp.concatenate([x, p], axis=0) for x, p in zip(xs, pw)], pw)
    xs = [x + b[0:rows] for x, b in zip(xs, both)]
    pw = [b[rows:2 * rows] for b in both]
    xs = [x + y for x, y in zip(xs, mm(xs, pw))]
    for shift in (4, 5, 6):
        off = [jnp.where(same(shift), jnp.where(same(shift - 1), 0.0, m), 0.0) for m in mats]
        xs = [x - y for x, y in zip(xs, mm(xs, mm(off, xs)))]
    return xs


def _gdn_chunk_kernel(q_ref, k_ref, v_ref, bg_ref, e64_ref, eb_ref, eg_ref, wq_ref, u_ref, l2_ref, egl_ref):
    cs = CHUNK_B
    nh = N_HEADS_B
    ri = lax.broadcasted_iota(jnp.int32, (cs, cs), 0)
    ci = lax.broadcasted_iota(jnp.int32, (cs, cs), 1)
    row = lax.broadcasted_iota(jnp.int32, (cs, nh * cs), 0)
    col = lax.broadcasted_iota(jnp.int32, (cs, nh * cs), 1) & (cs - 1)
    lane_lo = lax.broadcasted_iota(jnp.int32, (cs, LANES), 1) < cs
    lane_lo2 = lax.broadcasted_iota(jnp.int32, (2 * cs, LANES), 1) < cs
    zeros = jnp.zeros((cs, HEAD_DIM_B), F32)
    tri = (jnp.where(ci <= ri, 1.0, 0.0).astype(BF16), jnp.where(ci >= ri, 1.0, 0.0).astype(BF16))
    strict = (row > col, row < col)
    causal = (row >= col, row <= col)

    def dot_parts(lhs, parts):
        return _dot(lhs, parts[0]) + (_dot(lhs, parts[1]) + _dot(lhs, parts[2]))

    stage = {}
    mats = []
    for c in range(GDN_CHUNKS_PER_STEP):
        rows = slice(c * cs, (c + 1) * cs)
        qs = q_ref[rows, :] * (HEAD_DIM_B ** -0.5)
        kn = k_ref[rows, :]
        v = v_ref[rows, :]
        bg_parts = _split3(bg_ref[rows, :])
        for d in range(2):
            expand = lambda e: [_dot(part, e).astype(BF16) for part in bg_parts]
            g64 = [jnp.where(strict[d], part, 0.0).astype(BF16) for part in expand(e64_ref[d])]
            decay = jnp.where(causal[d], jnp.exp(dot_parts(tri[d], g64)), 0.0)
            beta_parts = expand(eb_ref[d])
            beta = beta_parts[0].astype(F32) + (beta_parts[1].astype(F32) + beta_parts[2].astype(F32))
            g_parts = expand(eg_ref[d])
            gc = dot_parts(tri[d], g_parts)
            gl = jnp.broadcast_to(gc[cs - 1:cs] if d == 0 else gc[0:1], gc.shape)
            egc = jnp.exp(gc)
            kb = kn * beta
            lhs = jnp.concatenate([kb, qs], axis=0).astype(BF16)
            prods = []
            for pr in range(2):
                ka = kn[:, (2 * pr) * HEAD_DIM_B:(2 * pr + 1) * HEAD_DIM_B]
                kb_ = kn[:, (2 * pr + 1) * HEAD_DIM_B:(2 * pr + 2) * HEAD_DIM_B]
                rhs_t = jnp.concatenate([jnp.concatenate([ka, zeros], axis=1),
                                         jnp.concatenate([zeros, kb_], axis=1)], axis=0).astype(BF16)
                prods.append(_dot_nt(lhs[:, 2 * pr * HEAD_DIM_B:(2 * pr + 2) * HEAD_DIM_B], rhs_t))
            prod = jnp.concatenate(prods, axis=1)
            mat = jnp.where(strict[d], prod[0:cs] * decay, 0.0)
            mats += [mat[:, 0:LANES], mat[:, LANES:2 * LANES]]
            stage[c, d] = dict(qk=prod[cs:2 * cs] * decay, vb=v * beta, kbe=kb * egc, qhead=qs * egc,
                               ktail=kn * jnp.exp(gl - gc), egl=jnp.exp(gl))

    row2 = lax.broadcasted_iota(jnp.int32, (cs, LANES), 0)
    col2 = lax.broadcasted_iota(jnp.int32, (cs, LANES), 1) & (cs - 1)
    tmats = _unit_tri_inverse(mats, row2, col2, lane_lo)

    for c in range(GDN_CHUNKS_PER_STEP):
        for d in range(2):
            st = stage[c, d]
            for pr in range(2):
                ha, hb = 2 * pr, 2 * pr + 1
                sa = slice(ha * HEAD_DIM_B, (ha + 1) * HEAD_DIM_B)
                sb = slice(hb * HEAD_DIM_B, (hb + 1) * HEAD_DIM_B)
                tp = tmats[(c * 2 + d) * 2 + pr]
                rhs = jnp.concatenate([jnp.concatenate([st["vb"][:, sa], st["kbe"][:, sa]], axis=1),
                                       jnp.concatenate([st["vb"][:, sb], st["kbe"][:, sb]], axis=1)],
                                      axis=0).astype(BF16)
                ra = _dot(jnp.where(lane_lo, tp, 0.0).astype(BF16), rhs)
                rb = _dot(jnp.where(lane_lo, 0.0, tp).astype(BF16), rhs)
                for hh, res, sl in ((ha, ra, sa), (hb, rb, sb)):
                    u_ref[c, d * nh + hh] = res[:, 0:HEAD_DIM_B]
                    wq_ref[c, d * nh + hh] = jnp.concatenate(
                        [res[:, HEAD_DIM_B:2 * HEAD_DIM_B], st["qhead"][:, sl]], axis=0).astype(BF16)
                kt_t = jnp.concatenate([st["ktail"][:, sa], st["ktail"][:, sb]], axis=0).T
                qkp = st["qk"][:, pr * LANES:(pr + 1) * LANES]
                l2_ref[c, d * 2 + pr] = jnp.concatenate(
                    [jnp.where(lane_lo, qkp, 0.0), jnp.where(lane_lo, 0.0, qkp),
                     jnp.where(lane_lo2, kt_t, 0.0), jnp.where(lane_lo2, 0.0, kt_t)], axis=0).astype(BF16)
            egl_ref[c, d] = jnp.concatenate(
                [st["egl"][0:1, hh * HEAD_DIM_B:(hh + 1) * HEAD_DIM_B] for hh in range(nh)]
                + [jnp.zeros((SUBLANES - nh, HEAD_DIM_B), F32)], axis=0)


def _gdn_expanders():
    nh, cs = N_HEADS_B, CHUNK_B
    e64 = np.zeros((2, LANES, nh * cs), np.float32)
    eb = np.zeros((2, LANES, nh * HEAD_DIM_B), np.float32)
    eg = np.zeros((2, LANES, nh * HEAD_DIM_B), np.float32)
    for d in range(2):
        for hh in range(nh):
            p = d * nh + hh
            e64[d, 2 * nh + p, hh * cs:(hh + 1) * cs] = 1.0
            eb[d, p, hh * HEAD_DIM_B:(hh + 1) * HEAD_DIM_B] = 1.0
            eg[d, 2 * nh + p, hh * HEAD_DIM_B:(hh + 1) * HEAD_DIM_B] = 1.0
    return jnp.asarray(e64, BF16), jnp.asarray(eb, BF16), jnp.asarray(eg, BF16)


def _gdn_chunk(qn, kn, v, bg, expanders):
    n = qn.shape[0]
    cs, nh = CHUNK_B, N_HEADS_B
    nc = n // cs
    g = GDN_CHUNKS_PER_STEP
    assert nc % g == 0
    e64, eb, eg = expanders
    const3 = lambda c: (0, 0, 0)
    return pl.pallas_call(
        _gdn_chunk_kernel,
        grid=(nc // g,),
        in_specs=[
            pl.BlockSpec((g * cs, D_B), lambda c: (c, 0)), pl.BlockSpec((g * cs, D_B), lambda c: (c, 0)),
            pl.BlockSpec((g * cs, D_B), lambda c: (c, 0)), pl.BlockSpec((g * cs, LANES), lambda c: (c, 0)),
            pl.BlockSpec(e64.shape, const3), pl.BlockSpec(eb.shape, const3), pl.BlockSpec(eg.shape, const3),
        ],
        out_specs=[
            pl.BlockSpec((g, 2 * nh, 2 * cs, HEAD_DIM_B), lambda c: (c, 0, 0, 0)),
            pl.BlockSpec((g, 2 * nh, cs, HEAD_DIM_B), lambda c: (c, 0, 0, 0)),
            pl.BlockSpec((g, 4, 6 * cs, LANES), lambda c: (c, 0, 0, 0)),
            pl.BlockSpec((g, 2, SUBLANES, HEAD_DIM_B), lambda c: (c, 0, 0, 0)),
        ],
        out_shape=[
            jax.ShapeDtypeStruct((nc, 2 * nh, 2 * cs, HEAD_DIM_B), BF16),
            jax.ShapeDtypeStruct((nc, 2 * nh, cs, HEAD_DIM_B), F32),
            jax.ShapeDtypeStruct((nc, 4, 6 * cs, LANES), BF16),
            jax.ShapeDtypeStruct((nc, 2, SUBLANES, HEAD_DIM_B), F32),
        ],
        compiler_params=_cparams("parallel"),
        name="gdn_chunk",
    )(qn, kn, v, bg, e64, eb, eg)


def _gdn_rec_kernel(wqf_ref, uf_ref, l2f_ref, egf_ref, wqb_ref, ub_ref, l2b_ref, egb_ref,
                    of_ref, ob_ref, s_scr):
    cs, nh = CHUNK_B, N_HEADS_B

    @pl.when(pl.program_id(0) == 0)
    def _():
        s_scr[...] = jnp.zeros_like(s_scr)

    dirs = ((wqf_ref, uf_ref, l2f_ref, egf_ref, of_ref), (wqb_ref, ub_ref, l2b_ref, egb_ref, ob_ref))
    for step in range(GDN_REC_CHUNKS_PER_STEP):
        for d in range(2):
            wq_ref, u_ref, l2_ref, eg_ref, o_ref = dirs[d]
            c = step if d == 0 else GDN_REC_CHUNKS_PER_STEP - 1 - step
            for pr in range(2):
                v_new, o_state = [], []
                for e in range(2):
                    hh = 2 * pr + e
                    r = _dot(wq_ref[c, hh], s_scr[d * nh + hh].astype(BF16))
                    v_new.append(u_ref[c, hh] - r[0:cs])
                    o_state.append(r[cs:2 * cs])
                r2 = _dot(l2_ref[c, pr], jnp.concatenate(v_new, axis=0).astype(BF16))
                for e in range(2):
                    hh = 2 * pr + e
                    o_ref[c * cs:(c + 1) * cs, hh * HEAD_DIM_B:(hh + 1) * HEAD_DIM_B] = (
                        o_state[e] + r2[e * cs:(e + 1) * cs])
                    s_scr[d * nh + hh] = (s_scr[d * nh + hh] * eg_ref[c, 0, hh:hh + 1, :]
                                          + r2[2 * cs + e * 2 * cs:2 * cs + (e + 1) * 2 * cs])


def _gdn_rec(wq, u, l2, egl, lc):
    cs, nh = CHUNK_B, N_HEADS_B
    g = GDN_REC_CHUNKS_PER_STEP
    assert wq.shape[0] % g == 0 and (lc // cs) % g == 0
    nc = wq.shape[0] // g
    ncc = lc // cs // g
    fwd = lambda i: i
    bwd = lambda i: jnp.where(i < ncc, ncc - 1 - i, nc - 1 - (i - ncc))
    specs = []
    for d, order in ((0, fwd), (1, bwd)):
        specs += [
            pl.BlockSpec((g, nh, 2 * cs, HEAD_DIM_B), lambda i, d=d, order=order: (order(i), d, 0, 0)),
            pl.BlockSpec((g, nh, cs, HEAD_DIM_B), lambda i, d=d, order=order: (order(i), d, 0, 0)),
            pl.BlockSpec((g, 2, 6 * cs, LANES), lambda i, d=d, order=order: (order(i), d, 0, 0)),
            pl.BlockSpec((g, 1, SUBLANES, HEAD_DIM_B), lambda i, d=d, order=order: (order(i), d, 0, 0)),
        ]
    return pl.pallas_call(
        _gdn_rec_kernel,
        grid=(nc,),
        in_specs=specs,
        out_specs=[pl.BlockSpec((g * cs, D_B), lambda i: (fwd(i), 0)),
                   pl.BlockSpec((g * cs, D_B), lambda i: (bwd(i), 0))],
        out_shape=[jax.ShapeDtypeStruct((nc * g * cs, D_B), F32)] * 2,
        scratch_shapes=[pltpu.VMEM((2 * nh, HEAD_DIM_B, HEAD_DIM_B), F32)],
        compiler_params=_cparams("arbitrary"),
        name="gdn_rec",
    )(wq, u, l2, egl, wq, u, l2, egl)


def _outproj_kernel(yaf_ref, yab_ref, obf_ref, obb_ref, octx_ref, olat_ref, ga_ref, gb_ref, gc_ref, h_ref,
                    mod_ref, on_ref, w_ref, gn_ref, modn_ref, *out_refs, lc, tm, row_off, last):
    row0 = (pl.program_id(0) + row_off) * tm
    ya = jnp.concatenate([yaf_ref[cg] + yab_ref[cg] for cg in range(D_A // LANES)], axis=1)
    mix_a = (ya * _silu(ga_ref[...])).astype(BF16)
    proj = _dot(mix_a, w_ref[0, 0:D_A, :])
    ob = obf_ref[...] + obb_ref[...]
    gb = _silu(gb_ref[...])
    mix_b = []
    for hd in range(N_HEADS_B):
        sl = slice(hd * HEAD_DIM_B, (hd + 1) * HEAD_DIM_B)
        x = ob[:, sl]
        mix_b.append(x * lax.rsqrt(jnp.mean(x * x, axis=-1, keepdims=True) + EPS) * on_ref[...] * gb[:, sl])
    proj = proj + _dot(jnp.concatenate(mix_b, axis=1).astype(BF16), w_ref[0, D_A:D_A + D_B, :])
    oc = olat_ref[...] if last else jnp.where(row0 < lc, octx_ref[...], olat_ref[...])
    mix_c = (oc * _silu(gc_ref[...])).astype(BF16)
    proj = proj + _dot(mix_c, w_ref[0, D_A + D_B:D_MIX, :])
    row = row0 + lax.broadcasted_iota(jnp.int32, (tm, 1), 0)
    gate = jnp.where(row < lc, mod_ref[1:2, 2 * D_MODEL:3 * D_MODEL], mod_ref[0:1, 2 * D_MODEL:3 * D_MODEL])
    h_new = h_ref[...] + gate * proj
    if last:
        out_refs[0][...] = (h_new * lax.rsqrt(jnp.mean(h_new * h_new, axis=-1, keepdims=True) + EPS)
                            * gn_ref[...])
    else:
        out_refs[0][...] = h_new
        out_refs[1][...] = _adaln(h_new, row0, gn_ref, modn_ref, lc)


def _outproj(ya_f, ya_b, ob_f, ob_b, oc_ctx, oc_lat, z, h, mod_l, onorm_l, w_out_r, layer, gain_next,
             mod_next, lc, last):
    n = h.shape[0]
    tm = ATTN_TILE
    assert lc % tm == 0
    n_ctx = lc // tm
    row_off = n_ctx if last else 0
    rows = lambda width, col: pl.BlockSpec((tm, width), lambda i: (i + row_off, col // width))
    const2 = lambda i: (0, 0)
    out_block = pl.BlockSpec((tm, D_MODEL), lambda i: (i, 0))
    if last:
        out_specs = [out_block]
        out_shape = [jax.ShapeDtypeStruct((n - lc, D_MODEL), F32)]
    else:
        out_specs = [out_block, out_block]
        out_shape = [jax.ShapeDtypeStruct((n, D_MODEL), F32), jax.ShapeDtypeStruct((n, D_MODEL), BF16)]
    return pl.pallas_call(
        functools.partial(_outproj_kernel, lc=lc, tm=tm, row_off=row_off, last=last),
        grid=((n - row_off * tm) // tm,),
        in_specs=[
            pl.BlockSpec((D_A // LANES, tm, LANES), lambda i: (0, i + row_off, 0)),
            pl.BlockSpec((D_A // LANES, tm, LANES), lambda i: (0, i + row_off, 0)),
            rows(D_B, 0), rows(D_B, 0),
            pl.BlockSpec((tm, D_Q_C), lambda i: (jnp.minimum(i + row_off, n_ctx - 1), 0)),
            pl.BlockSpec((tm, D_Q_C), lambda i: (jnp.maximum(i + row_off - n_ctx, 0), 0)),
            rows(D_A, COL_GA), rows(D_B, COL_GB), rows(D_Q_C, COL_GC), rows(D_MODEL, 0),
            pl.BlockSpec((SUBLANES, 3 * D_MODEL), const2), pl.BlockSpec((1, HEAD_DIM_B), const2),
            pl.BlockSpec((1, D_MIX, D_MODEL), lambda i: (layer, 0, 0)),
            pl.BlockSpec((1, D_MODEL), const2), pl.BlockSpec((SUBLANES, 3 * D_MODEL), const2),
        ],
        out_specs=out_specs,
        out_shape=out_shape,
        compiler_params=_cparams("parallel"),
        name="outproj_last" if last else "outproj",
    )(ya_f, ya_b, ob_f, ob_b, oc_ctx, oc_lat, z, z, z, h, mod_l, onorm_l, w_out_r, gain_next, mod_next)


def _rope_tables(t_len, lc):
    rows = t_len // GRID_W
    row = jnp.repeat(jnp.arange(rows, dtype=jnp.int32), GRID_W)
    col = jnp.tile(jnp.arange(GRID_W, dtype=jnp.int32), rows)
    n_freq = HEAD_DIM_C // 4
    inv_freq = ROPE_BASE ** (-jnp.arange(n_freq, dtype=F32) / n_freq)
    ang = jnp.concatenate([row.astype(F32)[:, None] * inv_freq, col.astype(F32)[:, None] * inv_freq], axis=-1)
    cos, sin = jnp.cos(ang), jnp.sin(ang)
    half = HEAD_DIM_C // 2
    cos2 = jnp.concatenate([jnp.ones((lc, 2 * half), F32), jnp.concatenate([cos, cos], axis=-1)], axis=0)
    sin2 = jnp.concatenate([jnp.zeros((lc, 2 * half), F32), jnp.concatenate([-sin, sin], axis=-1)], axis=0)
    return cos2, sin2


def _relayout_w_in_kernel(w_ref, o_ref):
    n_ba = 4 * N_HEADS_B
    src_ba = 2 * D_A + 4 * D_B
    src_qc = src_ba + n_ba
    copies = ((0, COL_XA, D_A), (D_A, COL_GA, D_A), (2 * D_A, COL_QKVB, 3 * D_B),
              (2 * D_A + 3 * D_B, COL_GB, D_B), (src_ba, COL_BA, n_ba),
              (src_qc, COL_QC, D_Q_C), (src_qc + D_Q_C, COL_KV, 2 * D_KV_C),
              (src_qc + D_Q_C + 2 * D_KV_C, COL_GC, D_Q_C))
    for src, dst, rows in copies:
        o_ref[0, dst:dst + rows, :] = w_ref[0, src:src + rows, :].astype(BF16)
    o_ref[0, COL_BA + n_ba:D_Z, :] = jnp.zeros((D_Z - COL_BA - n_ba, o_ref.shape[2]), BF16)


def _relayout_w_in(w_in):
    depth, d, d_in = w_in.shape
    tc = 512
    return pl.pallas_call(
        _relayout_w_in_kernel,
        grid=(depth, d // tc),
        in_specs=[pl.BlockSpec((1, d_in, tc), lambda l, i: (l, 0, i))],
        out_specs=pl.BlockSpec((1, D_Z, tc), lambda l, i: (l, 0, i)),
        out_shape=jax.ShapeDtypeStruct((depth, D_Z, d), BF16),
        compiler_params=_cparams("parallel", "parallel"),
        name="relayout_w_in",
    )(jnp.swapaxes(w_in, 1, 2))


def _block_diag(w):
    eye = jnp.eye(N_BLK_A, dtype=w.dtype)
    out = jnp.einsum('...nij,nm->...nimj', w, eye)
    return out.reshape(w.shape[:-3] + (D_A, D_A))


def kernel(x, c, ctx, c_ctx, norm_g, w_mod, b_mod, w_in, conv_a_w, conv_a_b, w_ra, b_ra, w_ia, b_ia, lam_a,
           conv_b_w, a_log_b, dt_bias_b, onorm_b, qn_c, kn_c, w_out, final_g):
    bsz, t_len, d = x.shape
    lc = ctx.shape[1]
    depth = w_in.shape[0]
    assert bsz == 1 and d == D_MODEL and t_len % GRID_W == 0

    cc = jnp.concatenate([c, c_ctx[None, :], jnp.zeros((SUBLANES - 2, d), F32)], axis=0)
    mod = _modulation(cc, w_mod, b_mod)

    w_in_r = _relayout_w_in(w_in)
    w_out_r = w_out.astype(BF16)
    wg = jnp.concatenate([_block_diag(w_ra), _block_diag(w_ia)], axis=-1).astype(BF16)
    bgate = jnp.concatenate([b_ra, b_ia], axis=-1)[:, :, None, :]
    nh2 = 2 * N_HEADS_B
    lane_pad = lambda v: jnp.pad(v.reshape(depth, 1, nh2), ((0, 0), (0, 0), (nh2, LANES - 2 * nh2)))
    avec = lane_pad(a_log_b)
    dtvec = lane_pad(dt_bias_b)
    cos2, sin2 = _rope_tables(t_len, lc)
    expanders = _gdn_expanders()

    h, y = _prenorm(ctx, x, mod[0], norm_g[0][None, :])
    for l in range(depth):
        last = l == depth - 1
        z = _inproj(y, w_in_r, l)
        qh, kh, vt = _attn_prep(z, cos2, sin2, qn_c[l][None, :], kn_c[l][None, :])
        oc_lat = _attention(qh, kh, vt, lc, latent=True)
        oc_ctx = oc_lat if last else _attention(qh, kh, vt, lc, latent=False)
        ya_f, ya_b = _rglru(z, conv_a_w[l], conv_a_b[l][None, :], wg[l], bgate[l], lam_a[l], lc)
        qn, kn, v, bg = _gdn_prep(z, conv_b_w[l], avec[l], dtvec[l], lc)
        wq, u, l2, egl = _gdn_chunk(qn, kn, v, bg, expanders)
        ob_f, ob_b = _gdn_rec(wq, u, l2, egl, lc)
        gain_next = final_g[None, :] if last else norm_g[l + 1][None, :]
        mod_next = mod[l] if last else mod[l + 1]
        outs = _outproj(ya_f, ya_b, ob_f, ob_b, oc_ctx, oc_lat, z, h, mod[l], onorm_b[l][None, :], w_out_r, l,
                        gain_next, mod_next, lc, last)
        if last:
            return outs[0][None]
        h, y = outs
```

```python
import functools
import math

import numpy as np
import jax
import jax.numpy as jnp
from jax import lax
from jax.experimental import pallas as pl
from jax.experimental.pallas import tpu as pltpu

F32 = jnp.float32
BF16 = jnp.bfloat16

D_MODEL = 2048
GRID_W = 64
EPS = 1e-6
ROPE_BASE = 10000.0

D_A = 512
N_BLK_A = 8
BLK_A = D_A // N_BLK_A
LRU_C = 8.0

N_HEADS_B = 4
HEAD_DIM_B = 128
D_B = N_HEADS_B * HEAD_DIM_B
CHUNK_B = 64

N_Q_HEADS_C = 8
N_KV_HEADS_C = 2
GROUP_C = N_Q_HEADS_C // N_KV_HEADS_C
HEAD_DIM_C = 128
D_Q_C = N_Q_HEADS_C * HEAD_DIM_C
D_KV_C = N_KV_HEADS_C * HEAD_DIM_C

D_MIX = D_A + D_B + D_Q_C

COL_QC = 0
COL_GC = 1024
COL_KV = 2048
COL_XA = 2560
COL_QKVB = 3072
COL_GA = 4608
COL_GB = 5120
COL_BA = 5632
D_Z = 5760
LANES = 128
SUBLANES = 8
LOG2_E = math.log2(math.e)
ATTN_TILE = 256
V_ROWS = HEAD_DIM_C + 16
NEG_BIG = -1e30
GDN_REC_CHUNKS_PER_STEP = 4
GDN_CHUNKS_PER_STEP = 4

VMEM_LIMIT = 56 * 1024 * 1024


def _cparams(*sem):
    return pltpu.CompilerParams(dimension_semantics=tuple(sem), vmem_limit_bytes=VMEM_LIMIT)


def _dot(a, b):
    return jnp.dot(a, b, preferred_element_type=F32)


def _dot_nt(a, b):
    return lax.dot_general(a, b, (((1,), (1,)), ((), ())), preferred_element_type=F32)


def _split2(a):
    hi = a.astype(BF16)
    lo = (a - hi.astype(F32)).astype(BF16)
    return hi, lo


def _split3(a):
    hi = a.astype(BF16)
    r = a - hi.astype(F32)
    mid = r.astype(BF16)
    lo = (r - mid.astype(F32)).astype(BF16)
    return hi, mid, lo


def _dot3(a, b):
    ah, al = _split2(a)
    bh, bl = _split2(b)
    return _dot(ah, bh) + (_dot(al, bh) + _dot(ah, bl))


def _silu(x):
    return x * jax.nn.sigmoid(x)


def _softplus(x):
    return jnp.maximum(x, 0.0) + jnp.log1p(jnp.exp(-jnp.abs(x)))


def _pick(n, cands):
    for c in cands:
        if n % c == 0:
            return c
    raise ValueError(f"no tile for {n} in {cands}")


def _mod_kernel(c_ref, w_ref, b_ref, o_ref):
    o_ref[0] = _dot3(_silu(c_ref[...]), w_ref[0]) + b_ref[0]


def _modulation(cc, w_mod, b_mod):
    depth, d, d3 = w_mod.shape
    tn = 768
    return pl.pallas_call(
        _mod_kernel,
        grid=(depth, d3 // tn),
        in_specs=[
            pl.BlockSpec((SUBLANES, d), lambda l, j: (0, 0)),
            pl.BlockSpec((1, d, tn), lambda l, j: (l, 0, j)),
            pl.BlockSpec((1, 1, tn), lambda l, j: (l, 0, j)),
        ],
        out_specs=pl.BlockSpec((1, SUBLANES, tn), lambda l, j: (l, 0, j)),
        out_shape=jax.ShapeDtypeStruct((depth, SUBLANES, d3), F32),
        compiler_params=_cparams("parallel", "parallel"),
        name="modulation",
    )(cc, w_mod, b_mod.reshape(depth, 1, d3))


def _adaln(h_new, row0, g_ref, mod_ref, lc):
    xn = h_new * lax.rsqrt(jnp.mean(h_new * h_new, axis=-1, keepdims=True) + EPS) * g_ref[...]
    row = row0 + lax.broadcasted_iota(jnp.int32, (h_new.shape[0], 1), 0)
    is_ctx = row < lc
    shift = jnp.where(is_ctx, mod_ref[1:2, 0:D_MODEL], mod_ref[0:1, 0:D_MODEL])
    scale = jnp.where(is_ctx, mod_ref[1:2, D_MODEL:2 * D_MODEL], mod_ref[0:1, D_MODEL:2 * D_MODEL])
    return (xn * (1.0 + scale) + shift).astype(BF16)


def _prenorm_kernel(ctx_ref, x_ref, mod_ref, g_ref, h_ref, y_ref, *, lc, tm):
    row0 = pl.program_id(0) * tm
    h = jnp.where(row0 < lc, ctx_ref[0], x_ref[0])
    h_ref[...] = h
    y_ref[...] = _adaln(h, row0, g_ref, mod_ref, lc)


def _prenorm(ctx, x, mod_l, norm_g_l):
    lc, t_len = ctx.shape[1], x.shape[1]
    tm = 256
    assert lc % tm == 0 and t_len % tm == 0
    n_ctx = lc // tm
    n = lc + t_len
    out_block = pl.BlockSpec((tm, D_MODEL), lambda i: (i, 0))
    return pl.pallas_call(
        functools.partial(_prenorm_kernel, lc=lc, tm=tm),
        grid=(n // tm,),
        in_specs=[
            pl.BlockSpec((1, tm, D_MODEL), lambda i: (0, jnp.minimum(i, n_ctx - 1), 0)),
            pl.BlockSpec((1, tm, D_MODEL), lambda i: (0, jnp.maximum(i - n_ctx, 0), 0)),
            pl.BlockSpec((SUBLANES, 3 * D_MODEL), lambda i: (0, 0)),
            pl.BlockSpec((1, D_MODEL), lambda i: (0, 0)),
        ],
        out_specs=[out_block, out_block],
        out_shape=[jax.ShapeDtypeStruct((n, D_MODEL), F32), jax.ShapeDtypeStruct((n, D_MODEL), BF16)],
        compiler_params=_cparams("parallel"),
        name="prenorm",
    )(ctx, x, mod_l, norm_g_l)


def _inproj_kernel(y_ref, wt_ref, z_ref):
    z_ref[...] = _dot_nt(y_ref[...], wt_ref[0])


def _inproj(y, w_in_t, layer):
    n = y.shape[0]
    tm = _pick(n, (1408, 768, 640, 512, 256))
    tn = 1920
    return pl.pallas_call(
        _inproj_kernel,
        grid=(n // tm, D_Z // tn),
        in_specs=[
            pl.BlockSpec((tm, D_MODEL), lambda i, j: (i, 0)),
            pl.BlockSpec((1, tn, D_MODEL), lambda i, j: (layer, j, 0)),
        ],
        out_specs=pl.BlockSpec((tm, tn), lambda i, j: (i, j)),
        out_shape=jax.ShapeDtypeStruct((n, D_Z), F32),
        compiler_params=_cparams("parallel", "parallel"),
        name="inproj",
    )(y, w_in_t)


def _attn_prep_kernel(q_ref, kv_ref, cos_ref, sin_ref, qn_ref, kn_ref, qo_ref, ko_ref, vo_ref):
    cos = cos_ref[...]
    sin = sin_ref[...]

    ones = jnp.ones((HEAD_DIM_C, HEAD_DIM_C), BF16)

    def prep(x, gain, scale):
        sq_hi, sq_lo = _split2(x * x)
        mean_sq = (_dot(sq_hi, ones) + _dot(sq_lo, ones)) * (1.0 / HEAD_DIM_C)
        y = x * lax.rsqrt(mean_sq + EPS) * gain
        rot = pltpu.roll(y, HEAD_DIM_C // 2, 1)
        return (y * cos + rot * sin) * scale

    for hd in range(N_Q_HEADS_C):
        sl = slice(hd * HEAD_DIM_C, (hd + 1) * HEAD_DIM_C)
        qo_ref[:, sl] = prep(q_ref[:, sl], qn_ref[...], HEAD_DIM_C ** -0.5 * LOG2_E).astype(BF16)
    for hd in range(N_KV_HEADS_C):
        sl = slice(hd * HEAD_DIM_C, (hd + 1) * HEAD_DIM_C)
        ko_ref[:, sl] = prep(kv_ref[:, sl], kn_ref[...], 1.0).astype(BF16)
    vt = kv_ref[:, D_KV_C:2 * D_KV_C].T
    ones = jnp.ones((V_ROWS - HEAD_DIM_C, vt.shape[1]), F32)
    for hd in range(N_KV_HEADS_C):
        vo_ref[0, hd] = jnp.concatenate([vt[hd * HEAD_DIM_C:(hd + 1) * HEAD_DIM_C], ones], axis=0).astype(BF16)


def _attn_prep(z, cos2, sin2, qn_l, kn_l):
    n = z.shape[0]
    tm = ATTN_TILE
    return pl.pallas_call(
        _attn_prep_kernel,
        grid=(n // tm,),
        in_specs=[
            pl.BlockSpec((tm, D_Q_C), lambda i: (i, COL_QC // D_Q_C)),
            pl.BlockSpec((tm, 2 * D_KV_C), lambda i: (i, COL_KV // (2 * D_KV_C))),
            pl.BlockSpec((tm, HEAD_DIM_C), lambda i: (i, 0)),
            pl.BlockSpec((tm, HEAD_DIM_C), lambda i: (i, 0)),
            pl.BlockSpec((1, HEAD_DIM_C), lambda i: (0, 0)),
            pl.BlockSpec((1, HEAD_DIM_C), lambda i: (0, 0)),
        ],
        out_specs=[
            pl.BlockSpec((tm, D_Q_C), lambda i: (i, 0)),
            pl.BlockSpec((tm, D_KV_C), lambda i: (i, 0)),
            pl.BlockSpec((1, N_KV_HEADS_C, V_ROWS, tm), lambda i: (i, 0, 0, 0)),
        ],
        out_shape=[
            jax.ShapeDtypeStruct((n, D_Q_C), BF16),
            jax.ShapeDtypeStruct((n, D_KV_C), BF16),
            jax.ShapeDtypeStruct((n // tm, N_KV_HEADS_C, V_ROWS, tm), BF16),
        ],
        compiler_params=_cparams("parallel"),
        name="attn_prep",
    )(z, z, cos2, sin2, qn_l, kn_l)


def _attn_kernel(q_ref, k_ref, vt_ref, o_ref, s0_scr, sa_scr, sb_scr, acc_scr, *, lc, tq, tk, n_super):
    nb = lc // tk

    def scores(row0, nrows, s_scr):
        kb = k_ref[pl.ds(row0, nrows), :]
        for g in range(GROUP_C):
            s_scr[g] = _dot_nt(kb, q_ref[:, g * HEAD_DIM_C:(g + 1) * HEAD_DIM_C])

    def softmax_pv(s_scr, blk0, n_blk, ms):
        vts = [vt_ref[blk0 + b, 0] for b in range(n_blk)]
        new_m = []
        for g in range(GROUP_C):
            st = s_scr[g]
            m_blk = jnp.max(st, axis=0, keepdims=True)
            m_new = m_blk if ms is None else jnp.maximum(ms[g], m_blk)
            p = jnp.exp2(st - m_new).astype(BF16)
            pv = _dot(vts[0], p[0:tk])
            for b in range(1, n_blk):
                pv = pv + _dot(vts[b], p[b * tk:(b + 1) * tk])
            acc_scr[g] = pv if ms is None else jnp.exp2(ms[g] - m_new) * acc_scr[g] + pv
            new_m.append(m_new)
        return tuple(new_m)

    def super_rows(j):
        return pl.multiple_of(lc + j * (2 * tk), tk)

    if n_super:
        def pair(j, ms):
            scores(super_rows(j + 1), 2 * tk, sb_scr)
            ms = softmax_pv(sa_scr, nb + 2 * j, 2, ms)
            scores(super_rows(j + 2), 2 * tk, sa_scr)
            return softmax_pv(sb_scr, nb + 2 * j + 2, 2, ms)

        scores(super_rows(0), 2 * tk, sa_scr)
        ms = None
        if n_super >= 4:
            ms = pair(0, None)
            middle = n_super // 2 - 2
            per_trip = 3 if middle % 3 == 0 else 2

            def trip(i, ms):
                for t in range(per_trip):
                    ms = pair(2 * (per_trip * i + t) + 2, ms)
                return ms

            ms = lax.fori_loop(0, middle // per_trip, trip, ms)
            for j in range(2 * (middle // per_trip * per_trip) + 2, n_super - 2, 2):
                ms = pair(j, ms)
        scores(super_rows(n_super - 1), 2 * tk, sb_scr)
        ms = softmax_pv(sa_scr, nb + 2 * (n_super - 2), 2, ms)
        scores(0, lc, s0_scr)
        ms = softmax_pv(sb_scr, nb + 2 * (n_super - 1), 2, ms)
        softmax_pv(s0_scr, 0, nb, ms)
    else:
        scores(0, lc, s0_scr)
        softmax_pv(s0_scr, 0, nb, None)

    for g in range(GROUP_C):
        acc = acc_scr[g]
        o_ref[:, g * HEAD_DIM_C:(g + 1) * HEAD_DIM_C] = (
            acc[0:HEAD_DIM_C] / acc[HEAD_DIM_C:HEAD_DIM_C + 1]).T


def _attention(qh, kh, vt, lc, latent):
    n = qh.shape[0]
    tq = tk = ATTN_TILE
    assert lc % tq == 0 and n % tq == 0
    if latent:
        n_keys, n_q, q_off = n, n - lc, lc // tq
        n_super = (n - lc) // (2 * tk)
        assert n_super * 2 * tk == n - lc and n_super % 2 == 0
        super_shape = (GROUP_C, 2 * tk, tq)
    else:
        n_keys, n_q, q_off, n_super = lc, lc, 0, 0
        super_shape = (1, SUBLANES, LANES)
    gw = GROUP_C * HEAD_DIM_C
    return pl.pallas_call(
        functools.partial(_attn_kernel, lc=lc, tq=tq, tk=tk, n_super=n_super),
        grid=(N_KV_HEADS_C, n_q // tq),
        in_specs=[
            pl.BlockSpec((tq, gw), lambda j, i: (i + q_off, j)),
            pl.BlockSpec((n_keys, HEAD_DIM_C), lambda j, i: (0, j)),
            pl.BlockSpec((n_keys // tk, 1, V_ROWS, tk), lambda j, i: (0, j, 0, 0)),
        ],
        out_specs=pl.BlockSpec((tq, gw), lambda j, i: (i, j)),
        out_shape=jax.ShapeDtypeStruct((n_q, D_Q_C), F32),
        scratch_shapes=[pltpu.VMEM((GROUP_C, lc, tq), F32),
                        pltpu.VMEM(super_shape, F32), pltpu.VMEM(super_shape, F32),
                        pltpu.VMEM((GROUP_C, V_ROWS, tq), F32)],
        compiler_params=_cparams("parallel", "parallel"),
        name="attention" if latent else "attention_ctx",
    )(qh, kh, vt)


def _conv4(x, prev8, next8, w_ref, prev_ok, next_ok):
    tm = x.shape[0]
    row = lax.broadcasted_iota(jnp.int32, (tm, 1), 0)
    p = jnp.where(prev_ok, prev8, 0.0)
    nx = jnp.where(next_ok, next8, 0.0)
    xm1 = jnp.where(row == 0, p[7:8], pltpu.roll(x, 1, 0))
    xm2 = jnp.where(row == 0, p[6:7], jnp.where(row == 1, p[7:8], pltpu.roll(x, 2, 0)))
    xp1 = jnp.where(row == tm - 1, nx[0:1], pltpu.roll(x, tm - 1, 0))
    return w_ref[0:1] * xm2 + w_ref[1:2] * xm1 + w_ref[2:3] * x + w_ref[3:4] * xp1


def _seg_flags(t, nbc, nt):
    prev_ok = jnp.logical_and(t != 0, t != nbc)
    next_ok = jnp.logical_and(t != nbc - 1, t != nt - 1)
    return prev_ok, next_ok


def _halo_maps(tile_of, tm, n, col_block):
    r = tm // SUBLANES
    last = n // SUBLANES - 1
    main = lambda i: (tile_of(i), col_block)
    prev = lambda i: (jnp.maximum(tile_of(i) * r - 1, 0), col_block)
    nxt = lambda i: (jnp.minimum((tile_of(i) + 1) * r, last), col_block)
    return main, prev, nxt


def _rglru_kernel(xf_ref, pf_ref, nf_ref, xb_ref, pb_ref, nb_ref, cw_ref, cb_ref, wg_ref, bg_ref,
                  lam_ref, of_ref, ob_ref, x_scr, hcar, *, tm, nbc, nt):
    i = pl.program_id(0)
    seg = tm // SUBLANES
    width = D_A // 2
    ncg = width // LANES
    sub = lax.broadcasted_iota(jnp.int32, (SUBLANES, width), 0)

    @pl.when(i == 0)
    def _():
        hcar[...] = jnp.zeros_like(hcar)

    tiles = (i, jnp.where(i < nbc, nbc - 1 - i, nt - 1 - (i - nbc)))
    refs = ((xf_ref, pf_ref, nf_ref, of_ref), (xb_ref, pb_ref, nb_ref, ob_ref))
    for d in range(2):
        x_ref, p_ref, n_ref, o_ref = refs[d]
        prev_ok, next_ok = _seg_flags(tiles[d], nbc, nt)
        for half in range(2):
            cs = slice(half * width, (half + 1) * width)
            ci = slice(D_A + half * width, D_A + (half + 1) * width)
            p = jnp.where(prev_ok, p_ref[:, cs], 0.0)
            nx = jnp.where(next_ok, n_ref[:, cs], 0.0)
            for cg in range(ncg):
                x_scr[cg] = x_ref[:, half * width + cg * LANES:half * width + (cg + 1) * LANES]
            xg = [jnp.concatenate([x_scr[cg, pl.ds(j, SUBLANES, stride=seg), :] for cg in range(ncg)], axis=1)
                  for j in range(seg)]
            before1 = jnp.where(sub == 0, p[7:8], pltpu.roll(xg[seg - 1], 1, 0))
            before2 = jnp.where(sub == 0, p[6:7], pltpu.roll(xg[seg - 2], 1, 0))
            after1 = jnp.where(sub == SUBLANES - 1, nx[0:1], pltpu.roll(xg[0], SUBLANES - 1, 0))
            xp = jnp.concatenate(xg, axis=0)
            xm1 = jnp.concatenate([before1] + xg[:seg - 1], axis=0)
            xm2 = jnp.concatenate([before2, before1] + xg[:seg - 2], axis=0)
            xp1 = jnp.concatenate(xg[1:] + [after1], axis=0)
            u = (cw_ref[0:1, cs] * xm2 + cw_ref[1:2, cs] * xm1 + cw_ref[2:3, cs] * xp
                 + cw_ref[3:4, cs] * xp1) + cb_ref[:, cs]
            ub = u.astype(BF16)
            gate_r = jax.nn.sigmoid(_dot(ub, wg_ref[d, cs, cs]) + bg_ref[d, :, cs])
            gate_i = jax.nn.sigmoid(_dot(ub, wg_ref[d, cs, ci]) + bg_ref[d, :, ci])
            log_a = (-LRU_C) * gate_r * _softplus(-lam_ref[d:d + 1, cs])
            a = jnp.exp(log_a)
            b = jnp.sqrt(-jnp.tanh(log_a) * (a * a + 1.0)) * (gate_i * u)

            order = list(range(seg)) if d == 0 else list(range(seg - 1, -1, -1))
            hloc = jnp.zeros((SUBLANES, width), F32)
            pcum = jnp.ones((SUBLANES, width), F32)
            hs, ps = [None] * seg, [None] * seg
            for j in order:
                aj = a[j * SUBLANES:(j + 1) * SUBLANES]
                hloc = aj * hloc + b[j * SUBLANES:(j + 1) * SUBLANES]
                pcum = aj * pcum
                hs[j], ps[j] = hloc, pcum
            carry = hcar[d:d + 1, cs]
            seg_in = [None] * SUBLANES
            for s in (range(SUBLANES) if d == 0 else range(SUBLANES - 1, -1, -1)):
                seg_in[s] = carry
                carry = pcum[s:s + 1] * carry + hloc[s:s + 1]
            hcar[d:d + 1, cs] = carry
            cin = jnp.concatenate(seg_in, axis=0)
            for j in range(seg):
                hj = hs[j] + ps[j] * cin
                for cg in range(ncg):
                    o_ref[half * ncg + cg, pl.ds(j, SUBLANES, stride=seg), :] = hj[:, cg * LANES:(cg + 1) * LANES]


def _rglru(z, cw, cb, wg, bg, lam, lc):
    n = z.shape[0]
    tm = 256
    assert lc % tm == 0 and n % tm == 0
    nbc, nt = lc // tm, n // tm
    fwd = lambda i: i
    bwd = lambda i: jnp.where(i < nbc, nbc - 1 - i, nt - 1 - (i - nbc))
    cblk = COL_XA // D_A
    fm, fp, fn = _halo_maps(fwd, tm, n, cblk)
    bm, bp, bn = _halo_maps(bwd, tm, n, cblk)
    const2 = lambda i: (0, 0)
    const3 = lambda i: (0, 0, 0)
    return pl.pallas_call(
        functools.partial(_rglru_kernel, tm=tm, nbc=nbc, nt=nt),
        grid=(nt,),
        in_specs=[
            pl.BlockSpec((tm, D_A), fm), pl.BlockSpec((SUBLANES, D_A), fp), pl.BlockSpec((SUBLANES, D_A), fn),
            pl.BlockSpec((tm, D_A), bm), pl.BlockSpec((SUBLANES, D_A), bp), pl.BlockSpec((SUBLANES, D_A), bn),
            pl.BlockSpec((4, D_A), const2), pl.BlockSpec((1, D_A), const2),
            pl.BlockSpec((2, D_A, 2 * D_A), const3), pl.BlockSpec((2, 1, 2 * D_A), const3),
            pl.BlockSpec((2, D_A), const2),
        ],
        out_specs=[pl.BlockSpec((D_A // LANES, tm, LANES), lambda i: (0, fwd(i), 0)),
                   pl.BlockSpec((D_A // LANES, tm, LANES), lambda i: (0, bwd(i), 0))],
        out_shape=[jax.ShapeDtypeStruct((D_A // LANES, n, LANES), F32)] * 2,
        scratch_shapes=[pltpu.VMEM((D_A // LANES, tm, LANES), F32), pltpu.VMEM((SUBLANES, D_A), F32)],
        compiler_params=_cparams("arbitrary"),
        name="rglru",
    )(z, z, z, z, z, z, cw, cb, wg, bg, lam)


def _gdn_prep_kernel(x_ref, p_ref, n_ref, ba_ref, cw_ref, av_ref, dt_ref, q_ref, k_ref, v_ref, bg_ref,
                     *, nbc, nt):
    prev_ok, next_ok = _seg_flags(pl.program_id(0), nbc, nt)
    u = _silu(_conv4(x_ref[...], p_ref[...], n_ref[...], cw_ref, prev_ok, next_ok))
    for hd in range(N_HEADS_B):
        sl = slice(hd * HEAD_DIM_B, (hd + 1) * HEAD_DIM_B)
        qv = u[:, hd * HEAD_DIM_B:(hd + 1) * HEAD_DIM_B]
        kv = u[:, D_B + hd * HEAD_DIM_B:D_B + (hd + 1) * HEAD_DIM_B]
        q_ref[:, sl] = qv * lax.rsqrt(jnp.sum(qv * qv, axis=-1, keepdims=True) + EPS)
        k_ref[:, sl] = kv * lax.rsqrt(jnp.sum(kv * kv, axis=-1, keepdims=True) + EPS)
    v_ref[...] = u[:, 2 * D_B:3 * D_B]
    ba = ba_ref[...]
    lane = lax.broadcasted_iota(jnp.int32, ba.shape, 1)
    beta = jax.nn.sigmoid(ba)
    g = -jnp.exp(av_ref[...]) * _softplus(ba + dt_ref[...])
    bg_ref[...] = jnp.where(lane < 2 * N_HEADS_B, beta, jnp.where(lane < 4 * N_HEADS_B, g, 0.0))


def _gdn_prep(z, cw, avec, dtvec, lc):
    n = z.shape[0]
    tm = 256
    nbc, nt = lc // tm, n // tm
    w3 = 3 * D_B
    assert COL_QKVB % w3 == 0
    m, p, nx = _halo_maps(lambda i: i, tm, n, COL_QKVB // w3)
    const2 = lambda i: (0, 0)
    return pl.pallas_call(
        functools.partial(_gdn_prep_kernel, nbc=nbc, nt=nt),
        grid=(nt,),
        in_specs=[
            pl.BlockSpec((tm, w3), m), pl.BlockSpec((SUBLANES, w3), p), pl.BlockSpec((SUBLANES, w3), nx),
            pl.BlockSpec((tm, LANES), lambda i: (i, COL_BA // LANES)),
            pl.BlockSpec((4, w3), const2), pl.BlockSpec((1, LANES), const2), pl.BlockSpec((1, LANES), const2),
        ],
        out_specs=[pl.BlockSpec((tm, D_B), lambda i: (i, 0))] * 3 + [pl.BlockSpec((tm, LANES), lambda i: (i, 0))],
        out_shape=[jax.ShapeDtypeStruct((n, D_B), F32)] * 3 + [jax.ShapeDtypeStruct((n, LANES), F32)],
        compiler_params=_cparams("parallel"),
        name="gdn_prep",
    )(z, z, z, z, cw, avec, dtvec)


def _unit_tri_inverse(mats, row, col, lane_lo):
    def block_diag(y):
        return jnp.concatenate([jnp.where(lane_lo, y, 0.0), jnp.where(lane_lo, 0.0, y)], axis=0).astype(BF16)

    def mm(xs, ys):
        return [_dot(x.astype(BF16), block_diag(y)) for x, y in zip(xs, ys)]

    def same(shift):
        return (row >> shift) == (col >> shift)

    eye = jnp.where(row == col, 1.0, 0.0)
    rows = row.shape[0]
    m8 = [jnp.where(same(3), m, 0.0) for m in mats]
    xs = [eye - m for m in m8]
    pw = mm(m8, m8)
    both = mm([jnp.concatenate([x, p], axis=0) for x, p in zip(xs, pw)], pw)
    xs = [x + b[0:rows] for x, b in zip(xs, both)]
    pw = [b[rows:2 * rows] for b in both]
    xs = [x + y for x, y in zip(xs, mm(xs, pw))]
    for shift in (4, 5, 6):
        off = [jnp.where(same(shift), jnp.where(same(shift - 1), 0.0, m), 0.0) for m in mats]
        xs = [x - y for x, y in zip(xs, mm(xs, mm(off, xs)))]
    return xs


def _gdn_chunk_kernel(q_ref, k_ref, v_ref, bg_ref, e64_ref, eb_ref, eg_ref, wq_ref, u_ref, l2_ref, egl_ref):
    cs = CHUNK_B
    nh = N_HEADS_B
    ri = lax.broadcasted_iota(jnp.int32, (cs, cs), 0)
    ci = lax.broadcasted_iota(jnp.int32, (cs, cs), 1)
    row = lax.broadcasted_iota(jnp.int32, (cs, nh * cs), 0)
    col = lax.broadcasted_iota(jnp.int32, (cs, nh * cs), 1) & (cs - 1)
    lane_lo = lax.broadcasted_iota(jnp.int32, (cs, LANES), 1) < cs
    lane_lo2 = lax.broadcasted_iota(jnp.int32, (2 * cs, LANES), 1) < cs
    zeros = jnp.zeros((cs, HEAD_DIM_B), F32)
    tri = (jnp.where(ci <= ri, 1.0, 0.0).astype(BF16), jnp.where(ci >= ri, 1.0, 0.0).astype(BF16))
    strict = (row > col, row < col)
    causal = (row >= col, row <= col)

    def dot_parts(lhs, parts):
        return _dot(lhs, parts[0]) + (_dot(lhs, parts[1]) + _dot(lhs, parts[2]))

    stage = {}
    mats = []
    for c in range(GDN_CHUNKS_PER_STEP):
        rows = slice(c * cs, (c + 1) * cs)
        qs = q_ref[rows, :] * (HEAD_DIM_B ** -0.5)
        kn = k_ref[rows, :]
        v = v_ref[rows, :]
        bg_parts = _split3(bg_ref[rows, :])
        for d in range(2):
            expand = lambda e: [_dot(part, e).astype(BF16) for part in bg_parts]
            g64 = [jnp.where(strict[d], part, 0.0).astype(BF16) for part in expand(e64_ref[d])]
            decay = jnp.where(causal[d], jnp.exp(dot_parts(tri[d], g64)), 0.0)
            beta_parts = expand(eb_ref[d])
            beta = beta_parts[0].astype(F32) + (beta_parts[1].astype(F32) + beta_parts[2].astype(F32))
            g_parts = expand(eg_ref[d])
            gc = dot_parts(tri[d], g_parts)
            gl = jnp.broadcast_to(gc[cs - 1:cs] if d == 0 else gc[0:1], gc.shape)
            egc = jnp.exp(gc)
            kb = kn * beta
            lhs = jnp.concatenate([kb, qs], axis=0).astype(BF16)
            prods = []
            for pr in range(2):
                ka = kn[:, (2 * pr) * HEAD_DIM_B:(2 * pr + 1) * HEAD_DIM_B]
                kb_ = kn[:, (2 * pr + 1) * HEAD_DIM_B:(2 * pr + 2) * HEAD_DIM_B]
                rhs_t = jnp.concatenate([jnp.concatenate([ka, zeros], axis=1),
                                         jnp.concatenate([zeros, kb_], axis=1)], axis=0).astype(BF16)
                prods.append(_dot_nt(lhs[:, 2 * pr * HEAD_DIM_B:(2 * pr + 2) * HEAD_DIM_B], rhs_t))
            prod = jnp.concatenate(prods, axis=1)
            mat = jnp.where(strict[d], prod[0:cs] * decay, 0.0)
            mats += [mat[:, 0:LANES], mat[:, LANES:2 * LANES]]
            stage[c, d] = dict(qk=prod[cs:2 * cs] * decay, vb=v * beta, kbe=kb * egc, qhead=qs * egc,
                               ktail=kn * jnp.exp(gl - gc), egl=jnp.exp(gl))

    row2 = lax.broadcasted_iota(jnp.int32, (cs, LANES), 0)
    col2 = lax.broadcasted_iota(jnp.int32, (cs, LANES), 1) & (cs - 1)
    tmats = _unit_tri_inverse(mats, row2, col2, lane_lo)

    for c in range(GDN_CHUNKS_PER_STEP):
        for d in range(2):
            st = stage[c, d]
            for pr in range(2):
                ha, hb = 2 * pr, 2 * pr + 1
                sa = slice(ha * HEAD_DIM_B, (ha + 1) * HEAD_DIM_B)
                sb = slice(hb * HEAD_DIM_B, (hb + 1) * HEAD_DIM_B)
                tp = tmats[(c * 2 + d) * 2 + pr]
                rhs = jnp.concatenate([jnp.concatenate([st["vb"][:, sa], st["kbe"][:, sa]], axis=1),
                                       jnp.concatenate([st["vb"][:, sb], st["kbe"][:, sb]], axis=1)],
                                      axis=0).astype(BF16)
                ra = _dot(jnp.where(lane_lo, tp, 0.0).astype(BF16), rhs)
                rb = _dot(jnp.where(lane_lo, 0.0, tp).astype(BF16), rhs)
                for hh, res, sl in ((ha, ra, sa), (hb, rb, sb)):
                    u_ref[c, d * nh + hh] = res[:, 0:HEAD_DIM_B]
                    wq_ref[c, d * nh + hh] = jnp.concatenate(
                        [res[:, HEAD_DIM_B:2 * HEAD_DIM_B], st["qhead"][:, sl]], axis=0).astype(BF16)
                kt_t = jnp.concatenate([st["ktail"][:, sa], st["ktail"][:, sb]], axis=0).T
                qkp = st["qk"][:, pr * LANES:(pr + 1) * LANES]
                l2_ref[c, d * 2 + pr] = jnp.concatenate(
                    [jnp.where(lane_lo, qkp, 0.0), jnp.where(lane_lo, 0.0, qkp),
                     jnp.where(lane_lo2, kt_t, 0.0), jnp.where(lane_lo2, 0.0, kt_t)], axis=0).astype(BF16)
            egl_ref[c, d] = jnp.concatenate(
                [st["egl"][0:1, hh * HEAD_DIM_B:(hh + 1) * HEAD_DIM_B] for hh in range(nh)]
                + [jnp.zeros((SUBLANES - nh, HEAD_DIM_B), F32)], axis=0)


def _gdn_expanders():
    nh, cs = N_HEADS_B, CHUNK_B
    e64 = np.zeros((2, LANES, nh * cs), np.float32)
    eb = np.zeros((2, LANES, nh * HEAD_DIM_B), np.float32)
    eg = np.zeros((2, LANES, nh * HEAD_DIM_B), np.float32)
    for d in range(2):
        for hh in range(nh):
            p = d * nh + hh
            e64[d, 2 * nh + p, hh * cs:(hh + 1) * cs] = 1.0
            eb[d, p, hh * HEAD_DIM_B:(hh + 1) * HEAD_DIM_B] = 1.0
            eg[d, 2 * nh + p, hh * HEAD_DIM_B:(hh + 1) * HEAD_DIM_B] = 1.0
    return jnp.asarray(e64, BF16), jnp.asarray(eb, BF16), jnp.asarray(eg, BF16)


def _gdn_chunk(qn, kn, v, bg, expanders):
    n = qn.shape[0]
    cs, nh = CHUNK_B, N_HEADS_B
    nc = n // cs
    g = GDN_CHUNKS_PER_STEP
    assert nc % g == 0
    e64, eb, eg = expanders
    const3 = lambda c: (0, 0, 0)
    return pl.pallas_call(
        _gdn_chunk_kernel,
        grid=(nc // g,),
        in_specs=[
            pl.BlockSpec((g * cs, D_B), lambda c: (c, 0)), pl.BlockSpec((g * cs, D_B), lambda c: (c, 0)),
            pl.BlockSpec((g * cs, D_B), lambda c: (c, 0)), pl.BlockSpec((g * cs, LANES), lambda c: (c, 0)),
            pl.BlockSpec(e64.shape, const3), pl.BlockSpec(eb.shape, const3), pl.BlockSpec(eg.shape, const3),
        ],
        out_specs=[
            pl.BlockSpec((g, 2 * nh, 2 * cs, HEAD_DIM_B), lambda c: (c, 0, 0, 0)),
            pl.BlockSpec((g, 2 * nh, cs, HEAD_DIM_B), lambda c: (c, 0, 0, 0)),
            pl.BlockSpec((g, 4, 6 * cs, LANES), lambda c: (c, 0, 0, 0)),
            pl.BlockSpec((g, 2, SUBLANES, HEAD_DIM_B), lambda c: (c, 0, 0, 0)),
        ],
        out_shape=[
            jax.ShapeDtypeStruct((nc, 2 * nh, 2 * cs, HEAD_DIM_B), BF16),
            jax.ShapeDtypeStruct((nc, 2 * nh, cs, HEAD_DIM_B), F32),
            jax.ShapeDtypeStruct((nc, 4, 6 * cs, LANES), BF16),
            jax.ShapeDtypeStruct((nc, 2, SUBLANES, HEAD_DIM_B), F32),
        ],
        compiler_params=_cparams("parallel"),
        name="gdn_chunk",
    )(qn, kn, v, bg, e64, eb, eg)


def _gdn_rec_kernel(wqf_ref, uf_ref, l2f_ref, egf_ref, wqb_ref, ub_ref, l2b_ref, egb_ref,
                    of_ref, ob_ref, s_scr):
    cs, nh = CHUNK_B, N_HEADS_B

    @pl.when(pl.program_id(0) == 0)
    def _():
        s_scr[...] = jnp.zeros_like(s_scr)

    dirs = ((wqf_ref, uf_ref, l2f_ref, egf_ref, of_ref), (wqb_ref, ub_ref, l2b_ref, egb_ref, ob_ref))
    for step in range(GDN_REC_CHUNKS_PER_STEP):
        for d in range(2):
            wq_ref, u_ref, l2_ref, eg_ref, o_ref = dirs[d]
            c = step if d == 0 else GDN_REC_CHUNKS_PER_STEP - 1 - step
            for pr in range(2):
                v_new, o_state = [], []
                for e in range(2):
                    hh = 2 * pr + e
                    r = _dot(wq_ref[c, hh], s_scr[d * nh + hh].astype(BF16))
                    v_new.append(u_ref[c, hh] - r[0:cs])
                    o_state.append(r[cs:2 * cs])
                r2 = _dot(l2_ref[c, pr], jnp.concatenate(v_new, axis=0).astype(BF16))
                for e in range(2):
                    hh = 2 * pr + e
                    o_ref[c * cs:(c + 1) * cs, hh * HEAD_DIM_B:(hh + 1) * HEAD_DIM_B] = (
                        o_state[e] + r2[e * cs:(e + 1) * cs])
                    s_scr[d * nh + hh] = (s_scr[d * nh + hh] * eg_ref[c, 0, hh:hh + 1, :]
                                          + r2[2 * cs + e * 2 * cs:2 * cs + (e + 1) * 2 * cs])


def _gdn_rec(wq, u, l2, egl, lc):
    cs, nh = CHUNK_B, N_HEADS_B
    g = GDN_REC_CHUNKS_PER_STEP
    assert wq.shape[0] % g == 0 and (lc // cs) % g == 0
    nc = wq.shape[0] // g
    ncc = lc // cs // g
    fwd = lambda i: i
    bwd = lambda i: jnp.where(i < ncc, ncc - 1 - i, nc - 1 - (i - ncc))
    specs = []
    for d, order in ((0, fwd), (1, bwd)):
        specs += [
            pl.BlockSpec((g, nh, 2 * cs, HEAD_DIM_B), lambda i, d=d, order=order: (order(i), d, 0, 0)),
            pl.BlockSpec((g, nh, cs, HEAD_DIM_B), lambda i, d=d, order=order: (order(i), d, 0, 0)),
            pl.BlockSpec((g, 2, 6 * cs, LANES), lambda i, d=d, order=order: (order(i), d, 0, 0)),
            pl.BlockSpec((g, 1, SUBLANES, HEAD_DIM_B), lambda i, d=d, order=order: (order(i), d, 0, 0)),
        ]
    return pl.pallas_call(
        _gdn_rec_kernel,
        grid=(nc,),
        in_specs=specs,
        out_specs=[pl.BlockSpec((g * cs, D_B), lambda i: (fwd(i), 0)),
                   pl.BlockSpec((g * cs, D_B), lambda i: (bwd(i), 0))],
        out_shape=[jax.ShapeDtypeStruct((nc * g * cs, D_B), F32)] * 2,
        scratch_shapes=[pltpu.VMEM((2 * nh, HEAD_DIM_B, HEAD_DIM_B), F32)],
        compiler_params=_cparams("arbitrary"),
        name="gdn_rec",
    )(wq, u, l2, egl, wq, u, l2, egl)


def _outproj_kernel(yaf_ref, yab_ref, obf_ref, obb_ref, octx_ref, olat_ref, ga_ref, gb_ref, gc_ref, h_ref,
                    mod_ref, on_ref, w_ref, gn_ref, modn_ref, *out_refs, lc, tm, row_off, last):
    row0 = (pl.program_id(0) + row_off) * tm
    ya = jnp.concatenate([yaf_ref[cg] + yab_ref[cg] for cg in range(D_A // LANES)], axis=1)
    mix_a = (ya * _silu(ga_ref[...])).astype(BF16)
    proj = _dot(mix_a, w_ref[0, 0:D_A, :])
    ob = obf_ref[...] + obb_ref[...]
    gb = _silu(gb_ref[...])
    mix_b = []
    for hd in range(N_HEADS_B):
        sl = slice(hd * HEAD_DIM_B, (hd + 1) * HEAD_DIM_B)
        x = ob[:, sl]
        mix_b.append(x * lax.rsqrt(jnp.mean(x * x, axis=-1, keepdims=True) + EPS) * on_ref[...] * gb[:, sl])
    proj = proj + _dot(jnp.concatenate(mix_b, axis=1).astype(BF16), w_ref[0, D_A:D_A + D_B, :])
    oc = olat_ref[...] if last else jnp.where(row0 < lc, octx_ref[...], olat_ref[...])
    mix_c = (oc * _silu(gc_ref[...])).astype(BF16)
    proj = proj + _dot(mix_c, w_ref[0, D_A + D_B:D_MIX, :])
    row = row0 + lax.broadcasted_iota(jnp.int32, (tm, 1), 0)
    gate = jnp.where(row < lc, mod_ref[1:2, 2 * D_MODEL:3 * D_MODEL], mod_ref[0:1, 2 * D_MODEL:3 * D_MODEL])
    h_new = h_ref[...] + gate * proj
    if last:
        out_refs[0][...] = (h_new * lax.rsqrt(jnp.mean(h_new * h_new, axis=-1, keepdims=True) + EPS)
                            * gn_ref[...])
    else:
        out_refs[0][...] = h_new
        out_refs[1][...] = _adaln(h_new, row0, gn_ref, modn_ref, lc)


def _outproj(ya_f, ya_b, ob_f, ob_b, oc_ctx, oc_lat, z, h, mod_l, onorm_l, w_out_r, layer, gain_next,
             mod_next, lc, last):
    n = h.shape[0]
    tm = ATTN_TILE
    assert lc % tm == 0
    n_ctx = lc // tm
    row_off = n_ctx if last else 0
    rows = lambda width, col: pl.BlockSpec((tm, width), lambda i: (i + row_off, col // width))
    const2 = lambda i: (0, 0)
    out_block = pl.BlockSpec((tm, D_MODEL), lambda i: (i, 0))
    if last:
        out_specs = [out_block]
        out_shape = [jax.ShapeDtypeStruct((n - lc, D_MODEL), F32)]
    else:
        out_specs = [out_block, out_block]
        out_shape = [jax.ShapeDtypeStruct((n, D_MODEL), F32), jax.ShapeDtypeStruct((n, D_MODEL), BF16)]
    return pl.pallas_call(
        functools.partial(_outproj_kernel, lc=lc, tm=tm, row_off=row_off, last=last),
        grid=((n - row_off * tm) // tm,),
        in_specs=[
            pl.BlockSpec((D_A // LANES, tm, LANES), lambda i: (0, i + row_off, 0)),
            pl.BlockSpec((D_A // LANES, tm, LANES), lambda i: (0, i + row_off, 0)),
            rows(D_B, 0), rows(D_B, 0),
            pl.BlockSpec((tm, D_Q_C), lambda i: (jnp.minimum(i + row_off, n_ctx - 1), 0)),
            pl.BlockSpec((tm, D_Q_C), lambda i: (jnp.maximum(i + row_off - n_ctx, 0), 0)),
            rows(D_A, COL_GA), rows(D_B, COL_GB), rows(D_Q_C, COL_GC), rows(D_MODEL, 0),
            pl.BlockSpec((SUBLANES, 3 * D_MODEL), const2), pl.BlockSpec((1, HEAD_DIM_B), const2),
            pl.BlockSpec((1, D_MIX, D_MODEL), lambda i: (layer, 0, 0)),
            pl.BlockSpec((1, D_MODEL), const2), pl.BlockSpec((SUBLANES, 3 * D_MODEL), const2),
        ],
        out_specs=out_specs,
        out_shape=out_shape,
        compiler_params=_cparams("parallel"),
        name="outproj_last" if last else "outproj",
    )(ya_f, ya_b, ob_f, ob_b, oc_ctx, oc_lat, z, z, z, h, mod_l, onorm_l, w_out_r, gain_next, mod_next)


def _rope_tables(t_len, lc):
    rows = t_len // GRID_W
    row = jnp.repeat(jnp.arange(rows, dtype=jnp.int32), GRID_W)
    col = jnp.tile(jnp.arange(GRID_W, dtype=jnp.int32), rows)
    n_freq = HEAD_DIM_C // 4
    inv_freq = ROPE_BASE ** (-jnp.arange(n_freq, dtype=F32) / n_freq)
    ang = jnp.concatenate([row.astype(F32)[:, None] * inv_freq, col.astype(F32)[:, None] * inv_freq], axis=-1)
    cos, sin = jnp.cos(ang), jnp.sin(ang)
    half = HEAD_DIM_C // 2
    cos2 = jnp.concatenate([jnp.ones((lc, 2 * half), F32), jnp.concatenate([cos, cos], axis=-1)], axis=0)
    sin2 = jnp.concatenate([jnp.zeros((lc, 2 * half), F32), jnp.concatenate([-sin, sin], axis=-1)], axis=0)
    return cos2, sin2


def _relayout_w_in_kernel(w_ref, o_ref):
    n_ba = 4 * N_HEADS_B
    src_ba = 2 * D_A + 4 * D_B
    src_qc = src_ba + n_ba
    copies = ((0, COL_XA, D_A), (D_A, COL_GA, D_A), (2 * D_A, COL_QKVB, 3 * D_B),
              (2 * D_A + 3 * D_B, COL_GB, D_B), (src_ba, COL_BA, n_ba),
              (src_qc, COL_QC, D_Q_C), (src_qc + D_Q_C, COL_KV, 2 * D_KV_C),
              (src_qc + D_Q_C + 2 * D_KV_C, COL_GC, D_Q_C))
    for src, dst, rows in copies:
        o_ref[0, dst:dst + rows, :] = w_ref[0, src:src + rows, :].astype(BF16)
    o_ref[0, COL_BA + n_ba:D_Z, :] = jnp.zeros((D_Z - COL_BA - n_ba, o_ref.shape[2]), BF16)


def _relayout_w_in(w_in):
    depth, d, d_in = w_in.shape
    tc = 512
    return pl.pallas_call(
        _relayout_w_in_kernel,
        grid=(depth, d // tc),
        in_specs=[pl.BlockSpec((1, d_in, tc), lambda l, i: (l, 0, i))],
        out_specs=pl.BlockSpec((1, D_Z, tc), lambda l, i: (l, 0, i)),
        out_shape=jax.ShapeDtypeStruct((depth, D_Z, d), BF16),
        compiler_params=_cparams("parallel", "parallel"),
        name="relayout_w_in",
    )(jnp.swapaxes(w_in, 1, 2))


def _block_diag(w):
    eye = jnp.eye(N_BLK_A, dtype=w.dtype)
    out = jnp.einsum('...nij,nm->...nimj', w, eye)
    return out.reshape(w.shape[:-3] + (D_A, D_A))


def kernel(x, c, ctx, c_ctx, norm_g, w_mod, b_mod, w_in, conv_a_w, conv_a_b, w_ra, b_ra, w_ia, b_ia, lam_a,
           conv_b_w, a_log_b, dt_bias_b, onorm_b, qn_c, kn_c, w_out, final_g):
    bsz, t_len, d = x.shape
    lc = ctx.shape[1]
    depth = w_in.shape[0]
    assert bsz == 1 and d == D_MODEL and t_len % GRID_W == 0

    cc = jnp.concatenate([c, c_ctx[None, :], jnp.zeros((SUBLANES - 2, d), F32)], axis=0)
    mod = _modulation(cc, w_mod, b_mod)

    w_in_r = _relayout_w_in(w_in)
    w_out_r = w_out.astype(BF16)
    wg = jnp.concatenate([_block_diag(w_ra), _block_diag(w_ia)], axis=-1).astype(BF16)
    bgate = jnp.concatenate([b_ra, b_ia], axis=-1)[:, :, None, :]
    nh2 = 2 * N_HEADS_B
    lane_pad = lambda v: jnp.pad(v.reshape(depth, 1, nh2), ((0, 0), (0, 0), (nh2, LANES - 2 * nh2)))
    avec = lane_pad(a_log_b)
    dtvec = lane_pad(dt_bias_b)
    cos2, sin2 = _rope_tables(t_len, lc)
    expanders = _gdn_expanders()

    h, y = _prenorm(ctx, x, mod[0], norm_g[0][None, :])
    for l in range(depth):
        last = l == depth - 1
        z = _inproj(y, w_in_r, l)
        qh, kh, vt = _attn_prep(z, cos2, sin2, qn_c[l][None, :], kn_c[l][None, :])
        oc_lat = _attention(qh, kh, vt, lc, latent=True)
        oc_ctx = oc_lat if last else _attention(qh, kh, vt, lc, latent=False)
        ya_f, ya_b = _rglru(z, conv_a_w[l], conv_a_b[l][None, :], wg[l], bgate[l], lam_a[l], lc)
        qn, kn, v, bg = _gdn_prep(z, conv_b_w[l], avec[l], dtvec[l], lc)
        wq, u, l2, egl = _gdn_chunk(qn, kn, v, bg, expanders)
        ob_f, ob_b = _gdn_rec(wq, u, l2, egl, lc)
        gain_next = final_g[None, :] if last else norm_g[l + 1][None, :]
        mod_next = mod[l] if last else mod[l + 1]
        outs = _outproj(ya_f, ya_b, ob_f, ob_b, oc_ctx, oc_lat, z, h, mod[l], onorm_b[l][None, :], w_out_r, l,
                        gain_next, mod_next, lc, last)
        if last:
            return outs[0][None]
        h, y = outs
```
